```python
import math
import jax, jax.numpy as jnp
from jax import lax
import numpy as np

D_MODEL = 1024
BATCH = 4
SEQ = 8192
DEPTH = 4
DEC_BATCH = 32
DEC_SEQ = 32
PAST_LEN = 1024

CHUNK = 64
Q_BLOCK = 128
HEAD_DIM = 64
H_SB = 8
H_DIFF = 4
H_BAND = 16
BAND_CHUNKS = 8
BAND_PAST = BAND_CHUNKS * CHUNK
BAND = BAND_PAST + CHUNK
REL_CLIP = 128
ROPE_THETA = 10000.0
FFN_HIDDEN = ((8 * D_MODEL + 3 * 256 - 1) // (3 * 256)) * 256
PLE_DIM = 256
N_EVEN = (DEPTH + 1) // 2
N_ODD = DEPTH // 2
EPS = 1e-6
SCALE = HEAD_DIM ** -0.5
NEG = -1e30
W_SB = H_SB * HEAD_DIM
W_DIFF_QK = H_DIFF * 2 * HEAD_DIM
W_DIFF_V = H_DIFF * 2 * HEAD_DIM
W_EVEN_IN = 3 * W_SB + 2 * W_DIFF_QK + W_DIFF_V
W_EVEN_OUT = W_SB + W_DIFF_V
W_BAND = H_BAND * HEAD_DIM
W_ODD_IN = 3 * W_BAND

kernel_name = "stickbreak_diff_chunkband_streaming_trunk"


def rmsnorm(x, g):
    xf = x.astype(jnp.float32)
    y = xf * lax.rsqrt(jnp.mean(xf * xf, axis=-1, keepdims=True) + EPS)
    return (y * g.astype(jnp.float32)).astype(x.dtype)


def rope(x, pos):
    half = HEAD_DIM // 2
    inv = ROPE_THETA ** (-jnp.arange(half, dtype=jnp.float32) / half)
    ang = pos.astype(jnp.float32)[:, None] * inv[None, :]
    shape = (1, ang.shape[0]) + (1,) * (x.ndim - 3) + (half,)
    cos = jnp.cos(ang).reshape(shape)
    sin = jnp.sin(ang).reshape(shape)
    xf = x.astype(jnp.float32)
    x1, x2 = xf[..., :half], xf[..., half:]
    return jnp.concatenate([x1 * cos - x2 * sin, x2 * cos + x1 * sin], axis=-1).astype(x.dtype)


def stick_breaking(q, k, v, q_pos, k_pos):
    z = jnp.einsum("bqhd,bkhd->bhqk", q, k).astype(jnp.float32) * SCALE
    mask = k_pos[None, :] < q_pos[:, None]
    log_beta = jax.nn.log_sigmoid(z)
    log_keep = jnp.where(mask, jax.nn.log_sigmoid(-z), 0.0)
    after = lax.cumsum(log_keep, axis=3, reverse=True) - log_keep
    w = jnp.where(mask, jnp.exp(log_beta + after), 0.0)
    return jnp.einsum("bhqk,bkhd->bqhd", w.astype(v.dtype), v)


def diff_attn(q, k, v, q_pos, k_pos, lam, gain, lam_init):
    s = jnp.einsum("bqhmd,bkhmd->bhmqk", q, k).astype(jnp.float32) * SCALE
    mask = (k_pos // CHUNK)[None, :] <= (q_pos // CHUNK)[:, None]
    pr = jax.nn.softmax(jnp.where(mask, s, NEG), axis=-1)
    a = pr[:, :, 0] - lam * pr[:, :, 1]
    o = jnp.einsum("bhqk,bkhe->bqhe", a.astype(v.dtype), v).astype(jnp.float32)
    o = o * lax.rsqrt(jnp.mean(o * o, axis=-1, keepdims=True) + EPS)
    o = o * gain.astype(jnp.float32)[None, None] * (1.0 - lam_init)
    return o.astype(v.dtype)


def band_attn(q, kb, vb, q_pos, k_pos, k_valid, bias):
    s = jnp.einsum("bqhd,bkhd->bhqk", q, kb).astype(jnp.float32) * SCALE
    idx = jnp.clip(q_pos[:, None] - k_pos[None, :], -REL_CLIP, REL_CLIP) + REL_CLIP
    s = s + bias.astype(jnp.float32)[:, idx][None]
    pr = jax.nn.softmax(jnp.where(k_valid[None, None, None, :], s, NEG), axis=-1)
    return jnp.einsum("bhqk,bkhd->bqhd", pr.astype(vb.dtype), vb)


def blocked_queries(fn, q, q_pos):
    B, S = q.shape[0], q.shape[1]
    nb = S // Q_BLOCK
    qb = jnp.moveaxis(q.reshape((B, nb, Q_BLOCK) + q.shape[2:]), 1, 0)
    pb = q_pos.reshape(nb, Q_BLOCK)
    ob = lax.map(lambda a: fn(a[0], a[1]), (qb, pb))
    return jnp.moveaxis(ob, 0, 1).reshape((B, S) + ob.shape[3:])


def band_prompt(q, k, v, bias):
    B, S = q.shape[0], q.shape[1]
    nc = S // CHUNK
    pad = ((0, 0), (BAND_PAST, 0), (0, 0), (0, 0))
    kp = jnp.pad(k, pad)
    vp = jnp.pad(v, pad)
    qc = jnp.moveaxis(q.reshape(B, nc, CHUNK, H_BAND, HEAD_DIM), 1, 0)

    def one(args):
        qi, c = args
        start = c * CHUNK
        kb = lax.dynamic_slice_in_dim(kp, start, BAND, axis=1)
        vb = lax.dynamic_slice_in_dim(vp, start, BAND, axis=1)
        q_pos = start + jnp.arange(CHUNK, dtype=jnp.int32)
        k_pos = start - BAND_PAST + jnp.arange(BAND, dtype=jnp.int32)
        return band_attn(qi, kb, vb, q_pos, k_pos, k_pos >= 0, bias)

    oc = lax.map(one, (qc, jnp.arange(nc, dtype=jnp.int32)))
    return jnp.moveaxis(oc, 0, 1).reshape(B, S, H_BAND, HEAD_DIM)


def even_proj(hn, w_in, pos):
    B, T = hn.shape[0], hn.shape[1]
    z = hn @ w_in
    cuts = [W_SB, 2 * W_SB, 3 * W_SB, 3 * W_SB + W_DIFF_QK, 3 * W_SB + 2 * W_DIFF_QK]
    q_a, k_a, v_a, q_b, k_b, v_b = jnp.split(z, cuts, axis=-1)
    q_a = q_a.reshape(B, T, H_SB, HEAD_DIM)
    k_a = k_a.reshape(B, T, H_SB, HEAD_DIM)
    v_a = v_a.reshape(B, T, H_SB, HEAD_DIM)
    q_b = rope(q_b.reshape(B, T, H_DIFF, 2, HEAD_DIM), pos)
    k_b = rope(k_b.reshape(B, T, H_DIFF, 2, HEAD_DIM), pos)
    v_b = v_b.reshape(B, T, H_DIFF, 2 * HEAD_DIM)
    return q_a, k_a, v_a, q_b, k_b, v_b


def odd_proj(hn, w_in):
    B, T = hn.shape[0], hn.shape[1]
    q, k, v = jnp.split(hn @ w_in, [W_BAND, 2 * W_BAND], axis=-1)
    shp = (B, T, H_BAND, HEAD_DIM)
    return q.reshape(shp), k.reshape(shp), v.reshape(shp)


def trunk(x, p, pos, past, norm_mix, w_in_even, w_out_even, diff_lambda, diff_norm,
          w_in_odd, w_out_odd, rel_bias, norm_ffn, w_gate, w_up, w_down,
          norm_ple, w_ple_gate, w_ple, norm_final):
    B, T = x.shape[0], x.shape[1]
    sb_k, sb_v, df_k, df_v, bd_k, bd_v = [], [], [], [], [], []
    h = x
    for li in range(DEPTH):
        hn = rmsnorm(h, norm_mix[li])
        if li % 2 == 0:
            e = li // 2
            q_a, k_a, v_a, q_b, k_b, v_b = even_proj(hn, w_in_even[e], pos)
            lam_init = 0.8 - 0.6 * math.exp(-0.3 * li)
            lp = diff_lambda[e].astype(jnp.float32)
            lam = jnp.exp(jnp.sum(lp[0] * lp[1])) - jnp.exp(jnp.sum(lp[2] * lp[3])) + lam_init
            gain = diff_norm[e]
            if past is None:
                o_a = blocked_queries(lambda qi, pi: stick_breaking(qi, k_a, v_a, pi, pos), q_a, pos)
                o_b = blocked_queries(lambda qi, pi: diff_attn(qi, k_b, v_b, pi, pos, lam, gain, lam_init), q_b, pos)
            else:
                past_len = past[0].shape[2]
                k_pos = jnp.concatenate([jnp.arange(past_len, dtype=jnp.int32), pos])
                ka = jnp.concatenate([past[0][e], k_a], axis=1)
                va = jnp.concatenate([past[1][e], v_a], axis=1)
                kb = jnp.concatenate([past[2][e], k_b], axis=1)
                vb = jnp.concatenate([past[3][e], v_b], axis=1)
                o_a = stick_breaking(q_a, ka, va, pos, k_pos)
                o_b = diff_attn(q_b, kb, vb, pos, k_pos, lam, gain, lam_init)
            sb_k.append(k_a)
            sb_v.append(v_a)
            df_k.append(k_b)
            df_v.append(v_b)
            o = jnp.concatenate([o_a.reshape(B, T, W_SB), o_b.reshape(B, T, W_DIFF_V)], axis=-1)
            h = h + o @ w_out_even[e]
        else:
            od = li // 2
            q, k, v = odd_proj(hn, w_in_odd[od])
            if past is None:
                o = band_prompt(q, k, v, rel_bias[od])
                keep = min(BAND_PAST, T)
                bd_k.append(k[:, T - keep:])
                bd_v.append(v[:, T - keep:])
            else:
                ck, cv = past[4][od], past[5][od]
                rows = ck.shape[1]
                past_len = past[0].shape[2]
                k_pos = jnp.concatenate([past_len - rows + jnp.arange(rows, dtype=jnp.int32), pos])
                kb = jnp.concatenate([ck, k], axis=1)
                vb = jnp.concatenate([cv, v], axis=1)
                o = band_attn(q, kb, vb, pos, k_pos, jnp.ones((rows + T,), dtype=bool), rel_bias[od])
                bd_k.append(k)
                bd_v.append(v)
            h = h + o.reshape(B, T, W_BAND) @ w_out_odd[od]
        hn = rmsnorm(h, norm_ffn[li])
        h = h + (jax.nn.silu(hn @ w_gate[li]) * (hn @ w_up[li])) @ w_down[li]
        gate = jax.nn.sigmoid(rmsnorm(h, norm_ple[li]) @ w_ple_gate[li])
        h = h + (p[li] @ w_ple[li]) * gate
    y = rmsnorm(h, norm_final)
    return y, (jnp.stack(sb_k), jnp.stack(sb_v), jnp.stack(df_k), jnp.stack(df_v), jnp.stack(bd_k), jnp.stack(bd_v))


def setup_inputs(seed: int = 0) -> dict:
    key = jax.random.key(seed)
    ks = jax.random.split(key, 26)
    f32 = jnp.float32
    nrm = lambda k, shp, s=1.0: jax.random.normal(k, shp, dtype=f32) * s
    band_rows = min(BAND_PAST, PAST_LEN)
    return {
        "x_prompt": nrm(ks[0], (BATCH, SEQ, D_MODEL)),
        "x_sample": nrm(ks[1], (DEC_BATCH, DEC_SEQ, D_MODEL)),
        "cache_sb_k": nrm(ks[2], (N_EVEN, DEC_BATCH, PAST_LEN, H_SB, HEAD_DIM)),
        "cache_sb_v": nrm(ks[3], (N_EVEN, DEC_BATCH, PAST_LEN, H_SB, HEAD_DIM)),
        "cache_diff_k": nrm(ks[4], (N_EVEN, DEC_BATCH, PAST_LEN, H_DIFF, 2, HEAD_DIM)),
        "cache_diff_v": nrm(ks[5], (N_EVEN, DEC_BATCH, PAST_LEN, H_DIFF, 2 * HEAD_DIM)),
        "cache_band_k": nrm(ks[6], (N_ODD, DEC_BATCH, band_rows, H_BAND, HEAD_DIM)),
        "cache_band_v": nrm(ks[7], (N_ODD, DEC_BATCH, band_rows, H_BAND, HEAD_DIM)),
        "p_prompt": nrm(ks[8], (DEPTH, BATCH, SEQ, PLE_DIM)),
        "p_sample": nrm(ks[9], (DEPTH, DEC_BATCH, DEC_SEQ, PLE_DIM)),
        "norm_mix": 1.0 + nrm(ks[10], (DEPTH, D_MODEL), 0.01),
        "w_in_even": nrm(ks[11], (N_EVEN, D_MODEL, W_EVEN_IN), D_MODEL ** -0.5),
        "w_out_even": nrm(ks[12], (N_EVEN, W_EVEN_OUT, D_MODEL), W_EVEN_OUT ** -0.5),
        "diff_lambda": nrm(ks[13], (N_EVEN, 4, HEAD_DIM), 0.1),
        "diff_norm": 1.0 + nrm(ks[14], (N_EVEN, H_DIFF, 2 * HEAD_DIM), 0.01),
        "w_in_odd": nrm(ks[15], (N_ODD, D_MODEL, W_ODD_IN), D_MODEL ** -0.5),
        "w_out_odd": nrm(ks[16], (N_ODD, W_BAND, D_MODEL), W_BAND ** -0.5),
        "rel_bias": nrm(ks[17], (N_ODD, H_BAND, 2 * REL_CLIP + 1), 0.5),
        "norm_ffn": 1.0 + nrm(ks[18], (DEPTH, D_MODEL), 0.01),
        "w_gate": nrm(ks[19], (DEPTH, D_MODEL, FFN_HIDDEN), D_MODEL ** -0.5),
        "w_up": nrm(ks[20], (DEPTH, D_MODEL, FFN_HIDDEN), D_MODEL ** -0.5),
        "w_down": nrm(ks[21], (DEPTH, FFN_HIDDEN, D_MODEL), FFN_HIDDEN ** -0.5),
        "norm_ple": 1.0 + nrm(ks[22], (DEPTH, D_MODEL), 0.01),
        "w_ple_gate": nrm(ks[23], (DEPTH, D_MODEL, D_MODEL), D_MODEL ** -0.5),
        "w_ple": nrm(ks[24], (DEPTH, PLE_DIM, D_MODEL), PLE_DIM ** -0.5),
        "norm_final": 1.0 + nrm(ks[25], (D_MODEL,), 0.01),
    }


def reference(x_prompt, x_sample, cache_sb_k, cache_sb_v, cache_diff_k, cache_diff_v,
              cache_band_k, cache_band_v, p_prompt, p_sample, norm_mix, w_in_even, w_out_even,
              diff_lambda, diff_norm, w_in_odd, w_out_odd, rel_bias, norm_ffn, w_gate, w_up,
              w_down, norm_ple, w_ple_gate, w_ple, norm_final):
    pos_p = jnp.arange(x_prompt.shape[1], dtype=jnp.int32)
    pos_s = cache_sb_k.shape[2] + jnp.arange(x_sample.shape[1], dtype=jnp.int32)
    y_prompt, st_p = trunk(x_prompt, p_prompt, pos_p, None, norm_mix, w_in_even, w_out_even,
                           diff_lambda, diff_norm, w_in_odd, w_out_odd, rel_bias, norm_ffn,
                           w_gate, w_up, w_down, norm_ple, w_ple_gate, w_ple, norm_final)
    past = (cache_sb_k, cache_sb_v, cache_diff_k, cache_diff_v, cache_band_k, cache_band_v)
    y_sample, st_s = trunk(x_sample, p_sample, pos_s, past, norm_mix, w_in_even, w_out_even,
                           diff_lambda, diff_norm, w_in_odd, w_out_odd, rel_bias, norm_ffn,
                           w_gate, w_up, w_down, norm_ple, w_ple_gate, w_ple, norm_final)
    sb_k_p, sb_v_p, df_k_p, df_v_p, bd_k_p, bd_v_p = st_p
    sb_k_s, sb_v_s, df_k_s, df_v_s, bd_k_s, bd_v_s = st_s
    return (y_prompt, y_sample, sb_k_p, sb_v_p, df_k_p, df_v_p, bd_k_p, bd_v_p,
            sb_k_s, sb_v_s, df_k_s, df_v_s, bd_k_s, bd_v_s)
```

```python
import functools
import math

import jax
import jax.numpy as jnp
from jax import lax
from jax.experimental import pallas as pl
from jax.experimental.pallas import tpu as pltpu

CHUNK = 64
HEAD_DIM = 64
H_SB = 8
H_DIFF = 4
H_BAND = 16
BAND_PAST = 8 * CHUNK
REL_CLIP = 128
ROPE_THETA = 10000.0
EPS = 1e-6
NEG = -1e30
SCALE = HEAD_DIM ** -0.5
LOG2E = 1.4426950408889634
QSCALE = SCALE * LOG2E

LANES = 128
VMEM_LIMIT = 56 * 1024 * 1024

SB_KEY_BLOCK = LANES
BAND_WINDOW = BAND_PAST + LANES

F32 = jnp.float32
BF16 = jnp.bfloat16


def _rms(x, g):
    return x * lax.rsqrt(jnp.mean(x * x, axis=-1, keepdims=True) + EPS) * g


def _sigmoid(x):
    return 1.0 / (1.0 + jnp.exp(-x))


def _dot(a, b):
    return jnp.dot(a, b, preferred_element_type=F32)


def _dot_nt(a, b):
    return lax.dot_general(a, b, (((1,), (1,)), ((), ())), preferred_element_type=F32)


def _params(n_axes):
    return pltpu.CompilerParams(dimension_semantics=("arbitrary",) * n_axes,
                                vmem_limit_bytes=VMEM_LIMIT)


def _rope(x, cos, sin_signed, first_half):
    outs = []
    for j in range(x.shape[1] // LANES):
        xj = x[:, j * LANES:(j + 1) * LANES]
        partner = jnp.where(first_half, pltpu.roll(xj, LANES - HEAD_DIM // 2, 1),
                            pltpu.roll(xj, HEAD_DIM // 2, 1))
        outs.append(xj * cos + partner * sin_signed)
    return jnp.concatenate(outs, axis=1)


def _pre_even_kernel(h_ref, g_ref, w_ref, cos_ref, sin_ref,
                     qa_ref, ka_ref, kab_ref, va_ref, vab_ref,
                     qb_ref, kb_ref, kbb_ref, vb_ref, vbb_ref):
    hn = _rms(h_ref[...], g_ref[...]).astype(BF16)
    width = qa_ref.shape[1]

    def proj(c):
        return _dot(hn, w_ref[:, c * width:(c + 1) * width])

    cos = cos_ref[...]
    sin_signed = sin_ref[...]
    lane = lax.broadcasted_iota(jnp.int32, cos.shape, 1)
    first_half = (lane % HEAD_DIM) < (HEAD_DIM // 2)

    qa_ref[...] = (proj(0) * QSCALE).astype(BF16)
    ka = proj(1)
    ka_ref[...] = ka
    kab_ref[...] = ka.astype(BF16)
    va = proj(2)
    va_ref[...] = va
    vab_ref[...] = va.astype(BF16)
    qb_ref[...] = (_rope(proj(3), cos, sin_signed, first_half) * QSCALE).astype(BF16)
    kb = _rope(proj(4), cos, sin_signed, first_half)
    kb_ref[...] = kb
    kbb_ref[...] = kb.astype(BF16)
    vb = proj(5)
    vb_ref[...] = vb
    vbb_ref[...] = vb.astype(BF16)


def _pre_odd_kernel(h_ref, g_ref, w_ref, q_ref, k_ref, kb_ref, v_ref, vb_ref):
    hn = _rms(h_ref[...], g_ref[...]).astype(BF16)
    width = q_ref.shape[1]
    q_ref[...] = (_dot(hn, w_ref[:, :width]) * QSCALE).astype(BF16)
    k = _dot(hn, w_ref[:, width:2 * width])
    k_ref[...] = k
    kb_ref[...] = k.astype(BF16)
    v = _dot(hn, w_ref[:, 2 * width:])
    v_ref[...] = v
    vb_ref[...] = v.astype(BF16)


def _pre_even(h, g, w, cos, sin_signed, tm):
    n, d = h.shape
    width = w.shape[1] // 6
    n_pos_blocks = cos.shape[0] // tm
    tok = lambda wd: pl.BlockSpec((tm, wd), lambda i: (i, 0))
    const = lambda a: pl.BlockSpec(a.shape, lambda i: (0, 0))
    pos = pl.BlockSpec((tm, LANES), lambda i: (i % n_pos_blocks, 0))
    f32o = jax.ShapeDtypeStruct((n, width), F32)
    bf16o = jax.ShapeDtypeStruct((n, width), BF16)
    return pl.pallas_call(
        _pre_even_kernel,
        grid=(n // tm,),
        in_specs=[tok(d), const(g), const(w), pos, pos],
        out_specs=[tok(width)] * 10,
        out_shape=[bf16o, f32o, bf16o, f32o, bf16o, bf16o, f32o, bf16o, f32o, bf16o],
        compiler_params=_params(1),
    )(h, g, w, cos, sin_signed)


def _pre_odd(h, g, w, tm):
    n, d = h.shape
    width = w.shape[1] // 3
    tok = lambda wd: pl.BlockSpec((tm, wd), lambda i: (i, 0))
    const = lambda a: pl.BlockSpec(a.shape, lambda i: (0, 0))
    f32o = jax.ShapeDtypeStruct((n, width), F32)
    bf16o = jax.ShapeDtypeStruct((n, width), BF16)
    return pl.pallas_call(
        _pre_odd_kernel,
        grid=(n // tm,),
        in_specs=[tok(d), const(g), const(w)],
        out_specs=[tok(width)] * 5,
        out_shape=[bf16o, f32o, bf16o, f32o, bf16o],
        compiler_params=_params(1),
    )(h, g, w)


def _post_kernel(*refs, n_o, ffn_chunk, final):
    h_ref = refs[0]
    o_refs = refs[1:1 + n_o]
    (p_ref, wout_ref, gffn_ref, wg_ref, wu_ref, wd_ref, gple_ref, wpg_ref, wp_ref) = refs[1 + n_o:10 + n_o]
    gfin_ref = refs[10 + n_o] if final else None
    out_ref = refs[-1]

    h = h_ref[...]
    off = 0
    for o_ref in o_refs:
        wd = o_ref.shape[1]
        h = h + _dot(o_ref[...], wout_ref[off:off + wd, :])
        off += wd

    hn = _rms(h, gffn_ref[...]).astype(BF16)
    hidden = wg_ref.shape[1]
    for c in range(hidden // ffn_chunk):
        cols = slice(c * ffn_chunk, (c + 1) * ffn_chunk)
        gt = _dot(hn, wg_ref[:, cols])
        up = _dot(hn, wu_ref[:, cols])
        act = (gt * _sigmoid(gt) * up).astype(BF16)
        h = h + _dot(act, wd_ref[cols, :])

    gate = _sigmoid(_dot(_rms(h, gple_ref[...]).astype(BF16), wpg_ref[...]))
    h = h + _dot(p_ref[...].astype(BF16), wp_ref[...]) * gate
    if final:
        h = _rms(h, gfin_ref[...])
    out_ref[...] = h


def _post(h, o_parts, p, wout, gffn, wg, wu, wd, gple, wpg, wp, gfin, tm):
    n, d = h.shape
    final = gfin is not None
    hidden = wg.shape[1]
    ffn_chunk = hidden
    tok = lambda a: pl.BlockSpec((tm, a.shape[1]), lambda i: (i, 0))
    const = lambda a: pl.BlockSpec(a.shape, lambda i: (0, 0), pipeline_mode=pl.Buffered(1))
    consts = [wout, gffn, wg, wu, wd, gple, wpg, wp] + ([gfin] if final else [])
    return pl.pallas_call(
        functools.partial(_post_kernel, n_o=len(o_parts), ffn_chunk=ffn_chunk, final=final),
        grid=(n // tm,),
        in_specs=[tok(h)] + [tok(o) for o in o_parts] + [tok(p)] + [const(c) for c in consts],
        out_specs=tok(h),
        out_shape=jax.ShapeDtypeStruct((n, d), F32),
        compiler_params=_params(1),
    )(h, *o_parts, p, *consts)


def _sb_kernel(q_ref, k_ref, v_ref, tt_ref, o_ref, acc_ref, run_ref, *, tq, q_off, n_diag):
    qi = pl.program_id(2)
    q2 = q_ref[...]
    lane = lax.broadcasted_iota(jnp.int32, (tq, LANES), 1)
    row_pos = q_off + qi * tq + lax.broadcasted_iota(jnp.int32, (tq, LANES), 0)
    first = lane < HEAD_DIM
    blk0 = (q_off + qi * tq) // SB_KEY_BLOCK
    tt = tt_ref[...]
    zero = jnp.zeros_like(q2)

    outs = []
    for head in range(2):
        qh = jnp.where(first if head == 0 else jnp.logical_not(first), q2, zero)
        acc_ref[...] = jnp.zeros((tq, LANES), F32)
        run_ref[...] = jnp.zeros((tq, LANES), F32)

        def block(kb, masked, qh=qh):
            start = pl.multiple_of(kb * SB_KEY_BLOCK, SB_KEY_BLOCK)
            kblk = k_ref[pl.ds(start, SB_KEY_BLOCK), :]
            vblk = v_ref[pl.ds(start, SB_KEY_BLOCK), :]
            s = _dot_nt(qh, kblk)
            soft = jnp.log(1.0 + jnp.exp2(-jnp.abs(s))) * LOG2E
            log_beta = jnp.minimum(s, 0.0) - soft
            log_keep = log_beta - s
            if masked:
                mask = (kb * SB_KEY_BLOCK + lane) < row_pos
                log_keep = jnp.where(mask, log_keep, 0.0)
            hi = log_keep.astype(BF16)
            lo = (log_keep - hi.astype(F32)).astype(BF16)
            cs = _dot(jnp.concatenate([hi, lo], axis=1), tt)
            w = jnp.exp2(log_beta + cs[:, :LANES] + run_ref[...])
            if masked:
                w = jnp.where(mask, w, 0.0)
            acc_ref[...] += _dot(w.astype(BF16), vblk)
            run_ref[...] += cs[:, LANES:]

        for j in reversed(range(n_diag)):
            block(blk0 + j, True)

        def body(i, carry):
            block(blk0 - 1 - i, False)
            return carry

        lax.fori_loop(0, blk0, body, 0)
        outs.append(acc_ref[...])

    o_ref[...] = jnp.where(first, outs[0], outs[1]).astype(o_ref.dtype)


def _suffix_matrix():
    r = lax.broadcasted_iota(jnp.int32, (2 * LANES, 2 * LANES), 0) % LANES
    c = lax.broadcasted_iota(jnp.int32, (2 * LANES, 2 * LANES), 1)
    return jnp.where((c >= LANES) | (r > c), 1.0, 0.0).astype(BF16)


def _sb_attn(q, k, v, tq, q_off):
    b, t, w = q.shape
    keys = k.shape[1]
    n_diag = -(-tq // SB_KEY_BLOCK)
    assert (q_off % SB_KEY_BLOCK == 0) and (tq % SB_KEY_BLOCK == 0 or t == tq)
    assert q_off + (t - tq) + n_diag * SB_KEY_BLOCK <= keys
    tt = _suffix_matrix()
    qspec = pl.BlockSpec((None, tq, LANES), lambda bi, hp, qi: (bi, qi, hp))
    kspec = pl.BlockSpec((None, keys, LANES), lambda bi, hp, qi: (bi, 0, hp))
    return pl.pallas_call(
        functools.partial(_sb_kernel, tq=tq, q_off=q_off, n_diag=n_diag),
        grid=(b, w // LANES, t // tq),
        in_specs=[qspec, kspec, kspec, pl.BlockSpec(tt.shape, lambda bi, hp, qi: (0, 0))],
        out_specs=qspec,
        out_shape=jax.ShapeDtypeStruct((b, t, w), BF16),
        scratch_shapes=[pltpu.VMEM((tq, LANES), F32), pltpu.VMEM((tq, LANES), F32)],
        compiler_params=_params(3),
    )(q, k, v, tt)


def _diff_kernel(lam_ref, gain_ref, q_ref, k_ref, v_ref, o_ref, m_ref, l_ref, acc_ref,
                 *, tq, tk, q_off, kv_len, n_diag, lam_init):
    qi = pl.program_id(2)
    q2 = q_ref[...]
    lane = lax.broadcasted_iota(jnp.int32, (tq, LANES), 1)
    first = lane < HEAD_DIM
    zero = jnp.zeros_like(q2)
    q_maps = (jnp.where(first, q2, zero), jnp.where(first, zero, q2))
    q_chunk = (q_off + qi * tq + lax.broadcasted_iota(jnp.int32, (tq, tk), 0)) // CHUNK
    col = lax.broadcasted_iota(jnp.int32, (tq, tk), 1)
    blk0 = (q_off + qi * tq) // tk
    ones = jnp.ones((tk, LANES), BF16)

    m_ref[...] = jnp.full(m_ref.shape, NEG, F32)
    l_ref[...] = jnp.zeros(l_ref.shape, F32)
    acc_ref[...] = jnp.zeros(acc_ref.shape, F32)

    def block(kb, masked):
        start = pl.multiple_of(kb * tk, tk)
        kblk = k_ref[pl.ds(start, tk), :]
        vext = jnp.concatenate([v_ref[pl.ds(start, tk), :], ones], axis=1)
        if masked:
            kpos = kb * tk + col
            mask = (kpos // CHUNK) <= q_chunk
            if kv_len % tk:
                mask = mask & (kpos < kv_len)
        for mp in range(2):
            s = _dot_nt(q_maps[mp], kblk)
            if masked:
                s = jnp.where(mask, s, NEG)
            m_old = m_ref[mp]
            m_new = jnp.maximum(m_old, jnp.max(s, axis=-1, keepdims=True))
            alpha = jnp.exp2(m_old - m_new)
            pv = _dot(jnp.exp2(s - m_new).astype(BF16), vext)
            acc_ref[mp] = acc_ref[mp] * alpha + pv[:, :LANES]
            l_ref[mp] = l_ref[mp] * alpha + pv[:, LANES:]
            m_ref[mp] = m_new

    for j in reversed(range(n_diag)):
        block(blk0 + j, True)

    def body(i, carry):
        block(i, False)
        return carry

    lax.fori_loop(0, blk0, body, 0)

    lp = lam_ref[...]
    lam = (jnp.exp(jnp.sum(lp[0:1] * lp[1:2], axis=-1, keepdims=True))
           - jnp.exp(jnp.sum(lp[2:3] * lp[3:4], axis=-1, keepdims=True)) + lam_init)
    o = acc_ref[0] / l_ref[0] - lam * (acc_ref[1] / l_ref[1])
    o = o * lax.rsqrt(jnp.mean(o * o, axis=-1, keepdims=True) + EPS)
    o_ref[...] = (o * gain_ref[...] * (1.0 - lam_init)).astype(o_ref.dtype)


def _diff_attn(q, k, v, lam_params, gain, tq, tk, q_off, kv_len, lam_init):
    b, t, w = q.shape
    keys = k.shape[1]
    n_diag = max(1, tq // tk)
    assert q_off % tk == 0 and (tq % tk == 0 or t == tq) and keys % tk == 0 and CHUNK % 8 == 0
    assert (q_off + t + tk - 1) // tk * tk <= keys
    gain3 = gain.reshape(gain.shape[0], 1, gain.shape[1])
    qspec = pl.BlockSpec((None, tq, LANES), lambda bi, hd, qi: (bi, qi, hd))
    kspec = pl.BlockSpec((None, keys, LANES), lambda bi, hd, qi: (bi, 0, hd))
    return pl.pallas_call(
        functools.partial(_diff_kernel, tq=tq, tk=tk, q_off=q_off, kv_len=kv_len, n_diag=n_diag,
                          lam_init=lam_init),
        grid=(b, w // LANES, t // tq),
        in_specs=[pl.BlockSpec(lam_params.shape, lambda bi, hd, qi: (0, 0)),
                  pl.BlockSpec((None, 1, LANES), lambda bi, hd, qi: (hd, 0, 0)),
                  qspec, kspec, kspec],
        out_specs=qspec,
        out_shape=jax.ShapeDtypeStruct((b, t, w), BF16),
        scratch_shapes=[pltpu.VMEM((2, tq, 1), F32), pltpu.VMEM((2, tq, LANES), F32),
                        pltpu.VMEM((2, tq, LANES), F32)],
        compiler_params=_params(3),
    )(lam_params, gain3, q, k, v)


def _band_kernel(q_ref, k_ref, v_ref, bias_ref, o_ref, *, rows, n_sub, valid_lo, valid_hi):
    qi = pl.program_id(2)
    lane = lax.broadcasted_iota(jnp.int32, (rows, LANES), 1)
    first = lane < HEAD_DIM
    ones = jnp.ones((BAND_WINDOW, LANES), BF16)
    kcol = lax.broadcasted_iota(jnp.int32, (1, BAND_WINDOW), 1)

    def sub(j, carry):
        r0 = pl.multiple_of(j * rows, rows)
        ws = pl.multiple_of((qi * n_sub + j) * rows, rows)
        q2 = q_ref[pl.ds(r0, rows), :]
        zero = jnp.zeros_like(q2)
        kw = k_ref[pl.ds(ws, BAND_WINDOW), :]
        vext = jnp.concatenate([v_ref[pl.ds(ws, BAND_WINDOW), :], ones], axis=1)
        krow = ws + kcol
        penalty = jnp.where((krow >= valid_lo) & (krow < valid_hi), 0.0, NEG)
        outs = []
        for head in range(2):
            qh = jnp.where(first if head == 0 else jnp.logical_not(first), q2, zero)
            s = _dot_nt(qh, kw) + bias_ref[head] + penalty
            p = jnp.exp2(s - jnp.max(s, axis=-1, keepdims=True))
            pv = _dot(p.astype(BF16), vext)
            outs.append(pv[:, :LANES] / pv[:, LANES:])
        o_ref[pl.ds(r0, rows), :] = jnp.where(first, outs[0], outs[1]).astype(o_ref.dtype)
        return carry

    lax.fori_loop(0, n_sub, sub, 0)


def _band_bias_table(rel_bias, rows):
    i = jnp.arange(rows, dtype=jnp.int32)[:, None]
    rel_key = jnp.arange(BAND_WINDOW, dtype=jnp.int32)[None, :] - BAND_PAST
    idx = jnp.clip(i - rel_key, -REL_CLIP, REL_CLIP) + REL_CLIP
    kc = jnp.floor_divide(rel_key, CHUNK)
    qc = i // CHUNK
    in_band = (kc <= qc) & (kc >= qc - BAND_PAST // CHUNK)
    return jnp.where(in_band[None], rel_bias.astype(F32)[:, idx] * LOG2E, NEG)


def _band_attn(q, k, v, bias_tab, rows, n_sub, valid_lo, valid_hi):
    b, t, w = q.shape
    keys = k.shape[1]
    tq = rows * n_sub
    assert t % tq == 0 and (t - rows) + BAND_WINDOW <= keys
    qspec = pl.BlockSpec((None, tq, LANES), lambda bi, hp, qi: (bi, qi, hp))
    kspec = pl.BlockSpec((None, keys, LANES), lambda bi, hp, qi: (bi, 0, hp))
    return pl.pallas_call(
        functools.partial(_band_kernel, rows=rows, n_sub=n_sub, valid_lo=valid_lo, valid_hi=valid_hi),
        grid=(b, w // LANES, t // tq),
        in_specs=[qspec, kspec, kspec,
                  pl.BlockSpec((2, rows, BAND_WINDOW), lambda bi, hp, qi: (hp, 0, 0))],
        out_specs=qspec,
        out_shape=jax.ShapeDtypeStruct((b, t, w), BF16),
        compiler_params=_params(3),
    )(q, k, v, bias_tab)


def _rope_tables(pos, reps):
    half = HEAD_DIM // 2
    inv = ROPE_THETA ** (-jnp.arange(half, dtype=F32) / half)
    ang = pos.astype(F32)[:, None] * inv[None, :]
    cos, sin = jnp.cos(ang), jnp.sin(ang)
    cos = jnp.tile(cos, (reps, LANES // half))
    sin_signed = jnp.tile(jnp.concatenate([-sin, sin], axis=1), (reps, LANES // HEAD_DIM))
    return cos, sin_signed


def _with_past(past, new, total):
    b, t, w = new.shape
    pad = jnp.zeros((b, total - past.shape[1] - t, w), BF16)
    return jnp.concatenate([past.astype(BF16), new, pad], axis=1)


def _trunk(x, p, past, weights, cfg):
    (norm_mix, w_in_even, w_out_even, diff_lambda, diff_norm, w_in_odd, w_out_odd, rel_bias,
     norm_ffn, w_gate, w_up, w_down, norm_ple, w_ple_gate, w_ple, norm_final) = weights
    b, t, d = x.shape
    n = b * t
    depth = norm_mix.shape[0]
    tm = cfg["tm"]
    q_off = 0 if past is None else past[0].shape[2]
    pos = q_off + jnp.arange(t, dtype=jnp.int32)
    if past is None:
        cos, sin_signed = _rope_tables(pos, 1)
    else:
        cos, sin_signed = _rope_tables(pos, tm // t)
    row = lambda a: a.reshape(1, -1)
    seq = lambda a: a.reshape(b, t, a.shape[-1])

    h = x.reshape(n, d)
    sb_k, sb_v, df_k, df_v, bd_k, bd_v = [], [], [], [], [], []
    for li in range(depth):
        if li % 2 == 0:
            e = li // 2
            qa, ka, kab, va, vab, qb, kb, kbb, vb, vbb = _pre_even(
                h, row(norm_mix[li]), w_in_even[e], cos, sin_signed, tm)
            lam_init = 0.8 - 0.6 * math.exp(-0.3 * li)
            if past is None:
                keys = [seq(a) for a in (kab, vab, kbb, vbb)]
                kv_len = t
            else:
                total = cfg["keys_total"]
                flat = lambda c: c.reshape(c.shape[0], c.shape[1], -1)
                keys = [_with_past(flat(c[e]), seq(a), total)
                        for c, a in zip(past[:4], (kab, vab, kbb, vbb))]
                kv_len = q_off + t
            o_a = _sb_attn(seq(qa), keys[0], keys[1], cfg["sb_tq"], q_off)
            o_b = _diff_attn(seq(qb), keys[2], keys[3], diff_lambda[e], diff_norm[e],
                             cfg["diff_tq"], cfg["diff_tk"], q_off, kv_len, lam_init)
            sb_k.append(ka)
            sb_v.append(va)
            df_k.append(kb)
            df_v.append(vb)
            o_parts = [o_a.reshape(n, -1), o_b.reshape(n, -1)]
            w_out = w_out_even[e]
        else:
            od = li // 2
            q, k, kbf, v, vbf = _pre_odd(h, row(norm_mix[li]), w_in_odd[od], tm)
            rows = cfg["band_rows"]
            bias_tab = _band_bias_table(rel_bias[od], rows)
            if past is None:
                front = jnp.zeros((b, BAND_PAST, kbf.shape[1]), BF16)
                kk = jnp.concatenate([front, seq(kbf)], axis=1)
                vv = jnp.concatenate([front, seq(vbf)], axis=1)
                valid_lo, valid_hi = BAND_PAST, BAND_PAST + t
                keep = min(BAND_PAST, t)
                bd_k.append(seq(k)[:, t - keep:])
                bd_v.append(seq(v)[:, t - keep:])
            else:
                ck, cv = past[4][od], past[5][od]
                cache_rows = ck.shape[1]
                assert cache_rows == BAND_PAST
                flat = lambda c: c.reshape(c.shape[0], c.shape[1], -1)
                kk = _with_past(flat(ck), seq(kbf), BAND_WINDOW)
                vv = _with_past(flat(cv), seq(vbf), BAND_WINDOW)
                valid_lo, valid_hi = 0, cache_rows + t
                bd_k.append(seq(k))
                bd_v.append(seq(v))
            o = _band_attn(seq(q), kk, vv, bias_tab, rows, cfg["band_sub"], valid_lo, valid_hi)
            o_parts = [o.reshape(n, -1)]
            w_out = w_out_odd[od]
        gfin = row(norm_final) if li == depth - 1 else None
        h = _post(h, o_parts, p[li].reshape(n, -1), w_out, row(norm_ffn[li]), w_gate[li], w_up[li],
                  w_down[li], row(norm_ple[li]), w_ple_gate[li], w_ple[li], gfin, tm)
    y = h.reshape(b, t, d)
    heads = lambda xs, shp: jnp.stack([a.reshape((b, -1) + shp) for a in xs])
    state = (heads(sb_k, (H_SB, HEAD_DIM)), heads(sb_v, (H_SB, HEAD_DIM)),
             heads(df_k, (H_DIFF, 2, HEAD_DIM)), heads(df_v, (H_DIFF, 2 * HEAD_DIM)),
             heads(bd_k, (H_BAND, HEAD_DIM)), heads(bd_v, (H_BAND, HEAD_DIM)))
    return y, state


def kernel(x_prompt, x_sample, cache_sb_k, cache_sb_v, cache_diff_k, cache_diff_v, cache_band_k, cache_band_v, p_prompt, p_sample, norm_mix, w_in_even, w_out_even, diff_lambda, diff_norm, w_in_odd, w_out_odd, rel_bias, norm_ffn, w_gate, w_up, w_down, norm_ple, w_ple_gate, w_ple, norm_final):
    bf = lambda a: a.astype(BF16)
    weights = (norm_mix, bf(w_in_even), bf(w_out_even), diff_lambda, diff_norm, bf(w_in_odd),
               bf(w_out_odd), rel_bias, norm_ffn, bf(w_gate), bf(w_up), bf(w_down), norm_ple,
               bf(w_ple_gate), bf(w_ple), norm_final)
    t_p = x_prompt.shape[1]
    t_s = x_sample.shape[1]
    past_len = cache_sb_k.shape[2]
    diff_tk = 256
    cfg_p = dict(tm=512, sb_tq=512, diff_tq=256, diff_tk=diff_tk, band_rows=LANES,
                 band_sub=min(8, t_p // LANES))
    keys_total = -(-(past_len + t_s) // diff_tk) * diff_tk
    cfg_s = dict(tm=min(512, x_sample.shape[0] * t_s), sb_tq=t_s, diff_tq=t_s, diff_tk=diff_tk, band_rows=t_s, band_sub=1,
                 keys_total=keys_total)
    y_p, st_p = _trunk(x_prompt, p_prompt, None, weights, cfg_p)
    past = (cache_sb_k, cache_sb_v, cache_diff_k, cache_diff_v, cache_band_k, cache_band_v)
    y_s, st_s = _trunk(x_sample, p_sample, past, weights, cfg_s)
    return (y_p, y_s) + tuple(st_p) + tuple(st_s)
```

```python
import functools
import math

import jax
import jax.numpy as jnp
from jax import lax
from jax.experimental import pallas as pl
from jax.experimental.pallas import tpu as pltpu

CHUNK = 64
HEAD_DIM = 64
H_SB = 8
H_DIFF = 4
H_BAND = 16
BAND_PAST = 8 * CHUNK
REL_CLIP = 128
ROPE_THETA = 10000.0
EPS = 1e-6
NEG = -1e30
SCALE = HEAD_DIM ** -0.5
LOG2E = 1.4426950408889634
QSCALE = SCALE * LOG2E

LANES = 128
VMEM_LIMIT = 56 * 1024 * 1024

SB_KEY_BLOCK = LANES
SB_EXIT_LOG2 = -160.0
BAND_WINDOW = BAND_PAST + LANES

F32 = jnp.float32
BF16 = jnp.bfloat16


def _rms(x, g):
    return x * lax.rsqrt(jnp.mean(x * x, axis=-1, keepdims=True) + EPS) * g


def _sigmoid(x):
    return 1.0 / (1.0 + jnp.exp(-x))


def _dot(a, b):
    return jnp.dot(a, b, preferred_element_type=F32)


def _dot_nt(a, b):
    return lax.dot_general(a, b, (((1,), (1,)), ((), ())), preferred_element_type=F32)


def _params(n_axes):
    return pltpu.CompilerParams(dimension_semantics=("arbitrary",) * n_axes,
                                vmem_limit_bytes=VMEM_LIMIT)


def _rope(x, cos, sin_signed, first_half):
    outs = []
    for j in range(x.shape[1] // LANES):
        xj = x[:, j * LANES:(j + 1) * LANES]
        partner = jnp.where(first_half, pltpu.roll(xj, LANES - HEAD_DIM // 2, 1),
                            pltpu.roll(xj, HEAD_DIM // 2, 1))
        outs.append(xj * cos + partner * sin_signed)
    return jnp.concatenate(outs, axis=1)


def _pre_even_kernel(h_ref, g_ref, w_ref, cos_ref, sin_ref,
                     qa_ref, ka_ref, kab_ref, va_ref, vab_ref,
                     qb_ref, kb_ref, kbb_ref, vb_ref, vbb_ref):
    hn = _rms(h_ref[...], g_ref[...]).astype(BF16)
    width = qa_ref.shape[1]

    def proj(c):
        return _dot(hn, w_ref[:, c * width:(c + 1) * width])

    cos = cos_ref[...]
    sin_signed = sin_ref[...]
    lane = lax.broadcasted_iota(jnp.int32, cos.shape, 1)
    first_half = (lane % HEAD_DIM) < (HEAD_DIM // 2)

    qa_ref[...] = (proj(0) * QSCALE).astype(BF16)
    ka = proj(1)
    ka_ref[...] = ka
    kab_ref[...] = ka.astype(BF16)
    va = proj(2)
    va_ref[...] = va
    vab_ref[...] = va.astype(BF16)
    qb_ref[...] = (_rope(proj(3), cos, sin_signed, first_half) * QSCALE).astype(BF16)
    kb = _rope(proj(4), cos, sin_signed, first_half)
    kb_ref[...] = kb
    kbb_ref[...] = kb.astype(BF16)
    vb = proj(5)
    vb_ref[...] = vb
    vbb_ref[...] = vb.astype(BF16)


def _pre_odd_kernel(h_ref, g_ref, w_ref, q_ref, k_ref, kb_ref, v_ref, vb_ref):
    hn = _rms(h_ref[...], g_ref[...]).astype(BF16)
    width = q_ref.shape[1]
    q_ref[...] = (_dot(hn, w_ref[:, :width]) * QSCALE).astype(BF16)
    k = _dot(hn, w_ref[:, width:2 * width])
    k_ref[...] = k
    kb_ref[...] = k.astype(BF16)
    v = _dot(hn, w_ref[:, 2 * width:])
    v_ref[...] = v
    vb_ref[...] = v.astype(BF16)


def _pre_even(h, g, w, cos, sin_signed, tm):
    n, d = h.shape
    width = w.shape[1] // 6
    n_pos_blocks = cos.shape[0] // tm
    tok = lambda wd: pl.BlockSpec((tm, wd), lambda i: (i, 0))
    const = lambda a: pl.BlockSpec(a.shape, lambda i: (0, 0))
    pos = pl.BlockSpec((tm, LANES), lambda i: (i % n_pos_blocks, 0))
    f32o = jax.ShapeDtypeStruct((n, width), F32)
    bf16o = jax.ShapeDtypeStruct((n, width), BF16)
    return pl.pallas_call(
        _pre_even_kernel,
        grid=(n // tm,),
        in_specs=[tok(d), const(g), const(w), pos, pos],
        out_specs=[tok(width)] * 10,
        out_shape=[bf16o, f32o, bf16o, f32o, bf16o, bf16o, f32o, bf16o, f32o, bf16o],
        compiler_params=_params(1),
    )(h, g, w, cos, sin_signed)


def _pre_odd(h, g, w, tm):
    n, d = h.shape
    width = w.shape[1] // 3
    tok = lambda wd: pl.BlockSpec((tm, wd), lambda i: (i, 0))
    const = lambda a: pl.BlockSpec(a.shape, lambda i: (0, 0))
    f32o = jax.ShapeDtypeStruct((n, width), F32)
    bf16o = jax.ShapeDtypeStruct((n, width), BF16)
    return pl.pallas_call(
        _pre_odd_kernel,
        grid=(n // tm,),
        in_specs=[tok(d), const(g), const(w)],
        out_specs=[tok(width)] * 5,
        out_shape=[bf16o, f32o, bf16o, f32o, bf16o],
        compiler_params=_params(1),
    )(h, g, w)


def _post_kernel(*refs, n_o, ffn_chunk, final):
    h_ref = refs[0]
    o_refs = refs[1:1 + n_o]
    (p_ref, wout_ref, gffn_ref, wg_ref, wu_ref, wd_ref, gple_ref, wpg_ref, wp_ref) = refs[1 + n_o:10 + n_o]
    gfin_ref = refs[10 + n_o] if final else None
    out_ref = refs[-1]

    h = h_ref[...]
    off = 0
    for o_ref in o_refs:
        wd = o_ref.shape[1]
        h = h + _dot(o_ref[...], wout_ref[off:off + wd, :])
        off += wd

    hn = _rms(h, gffn_ref[...]).astype(BF16)
    hidden = wg_ref.shape[1]
    for c in range(hidden // ffn_chunk):
        cols = slice(c * ffn_chunk, (c + 1) * ffn_chunk)
        gt = _dot(hn, wg_ref[:, cols])
        up = _dot(hn, wu_ref[:, cols])
        act = (gt * _sigmoid(gt) * up).astype(BF16)
        h = h + _dot(act, wd_ref[cols, :])

    gate = _sigmoid(_dot(_rms(h, gple_ref[...]).astype(BF16), wpg_ref[...]))
    h = h + _dot(p_ref[...].astype(BF16), wp_ref[...]) * gate
    if final:
        h = _rms(h, gfin_ref[...])
    out_ref[...] = h


def _post(h, o_parts, p, wout, gffn, wg, wu, wd, gple, wpg, wp, gfin, tm):
    n, d = h.shape
    final = gfin is not None
    hidden = wg.shape[1]
    ffn_chunk = hidden
    tok = lambda a: pl.BlockSpec((tm, a.shape[1]), lambda i: (i, 0))
    const = lambda a: pl.BlockSpec(a.shape, lambda i: (0, 0), pipeline_mode=pl.Buffered(1))
    consts = [wout, gffn, wg, wu, wd, gple, wpg, wp] + ([gfin] if final else [])
    return pl.pallas_call(
        functools.partial(_post_kernel, n_o=len(o_parts), ffn_chunk=ffn_chunk, final=final),
        grid=(n // tm,),
        in_specs=[tok(h)] + [tok(o) for o in o_parts] + [tok(p)] + [const(c) for c in consts],
        out_specs=tok(h),
        out_shape=jax.ShapeDtypeStruct((n, d), F32),
        compiler_params=_params(1),
    )(h, *o_parts, p, *consts)


def _sb_kernel(q_ref, k_ref, v_ref, tt_ref, o_ref, acc_ref, run_ref, *, tq, q_off, n_diag):
    qi = pl.program_id(2)
    q2 = q_ref[...]
    lane = lax.broadcasted_iota(jnp.int32, (tq, LANES), 1)
    row_pos = q_off + qi * tq + lax.broadcasted_iota(jnp.int32, (tq, LANES), 0)
    first = lane < HEAD_DIM
    blk0 = (q_off + qi * tq) // SB_KEY_BLOCK
    tt = tt_ref[...]
    zero = jnp.zeros_like(q2)
    heads = (0, 1)
    q_heads = (jnp.where(first, q2, zero), jnp.where(first, zero, q2))
    acc_ref[...] = jnp.zeros(acc_ref.shape, F32)
    run_ref[...] = jnp.zeros(run_ref.shape, F32)

    def block(kb, masked):
        start = pl.multiple_of(kb * SB_KEY_BLOCK, SB_KEY_BLOCK)
        kblk = k_ref[pl.ds(start, SB_KEY_BLOCK), :]
        vblk = v_ref[pl.ds(start, SB_KEY_BLOCK), :]
        s = [_dot_nt(q_heads[h], kblk) for h in heads]
        soft = [jnp.log(1.0 + jnp.exp2(-jnp.abs(s[h]))) * LOG2E for h in heads]
        log_beta = [jnp.minimum(s[h], 0.0) - soft[h] for h in heads]
        log_keep = [log_beta[h] - s[h] for h in heads]
        if masked:
            mask = (kb * SB_KEY_BLOCK + lane) < row_pos
            log_keep = [jnp.where(mask, log_keep[h], 0.0) for h in heads]
        hi = [log_keep[h].astype(BF16) for h in heads]
        lo = [(log_keep[h] - hi[h].astype(F32)).astype(BF16) for h in heads]
        cs = [_dot(jnp.concatenate([hi[h], lo[h]], axis=1), tt) for h in heads]
        w = [jnp.exp2(log_beta[h] + cs[h][:, :LANES] + run_ref[h]) for h in heads]
        if masked:
            w = [jnp.where(mask, w[h], 0.0) for h in heads]
        for h in heads:
            acc_ref[h] += _dot(w[h].astype(BF16), vblk)
            run_ref[h] += cs[h][:, LANES:]

    for j in reversed(range(n_diag)):
        block(blk0 + j, True)

    def cond(carry):
        i, live = carry
        return jnp.logical_and(i < blk0, live > SB_EXIT_LOG2)

    def body(carry):
        i, _ = carry
        live = jnp.max(jnp.maximum(run_ref[0], run_ref[1]))
        block(blk0 - 1 - i, False)
        return i + 1, live

    lax.while_loop(cond, body, (jnp.int32(0), jnp.float32(0.0)))
    o_ref[...] = jnp.where(first, acc_ref[0], acc_ref[1]).astype(o_ref.dtype)


def _suffix_matrix():
    r = lax.broadcasted_iota(jnp.int32, (2 * LANES, 2 * LANES), 0) % LANES
    c = lax.broadcasted_iota(jnp.int32, (2 * LANES, 2 * LANES), 1)
    return jnp.where((c >= LANES) | (r > c), 1.0, 0.0).astype(BF16)


def _sb_attn(q, k, v, tq, q_off):
    b, t, w = q.shape
    keys = k.shape[1]
    n_diag = -(-tq // SB_KEY_BLOCK)
    assert (q_off % SB_KEY_BLOCK == 0) and (tq % SB_KEY_BLOCK == 0 or t == tq)
    assert q_off + (t - tq) + n_diag * SB_KEY_BLOCK <= keys
    tt = _suffix_matrix()
    qspec = pl.BlockSpec((None, tq, LANES), lambda bi, hp, qi: (bi, qi, hp))
    kspec = pl.BlockSpec((None, keys, LANES), lambda bi, hp, qi: (bi, 0, hp))
    return pl.pallas_call(
        functools.partial(_sb_kernel, tq=tq, q_off=q_off, n_diag=n_diag),
        grid=(b, w // LANES, t // tq),
        in_specs=[qspec, kspec, kspec, pl.BlockSpec(tt.shape, lambda bi, hp, qi: (0, 0))],
        out_specs=qspec,
        out_shape=jax.ShapeDtypeStruct((b, t, w), BF16),
        scratch_shapes=[pltpu.VMEM((2, tq, LANES), F32), pltpu.VMEM((2, tq, LANES), F32)],
        compiler_params=_params(3),
    )(q, k, v, tt)


def _diff_kernel(lam_ref, gain_ref, q_ref, k_ref, v_ref, o_ref, m_ref, l_ref, acc_ref,
                 *, tq, tk, q_off, kv_len, n_diag, lam_init):
    qi = pl.program_id(2)
    q2 = q_ref[...]
    lane = lax.broadcasted_iota(jnp.int32, (tq, LANES), 1)
    first = lane < HEAD_DIM
    zero = jnp.zeros_like(q2)
    q_maps = (jnp.where(first, q2, zero), jnp.where(first, zero, q2))
    q_chunk = (q_off + qi * tq + lax.broadcasted_iota(jnp.int32, (tq, tk), 0)) // CHUNK
    col = lax.broadcasted_iota(jnp.int32, (tq, tk), 1)
    blk0 = (q_off + qi * tq) // tk
    ones = jnp.ones((tk, LANES), BF16)

    m_ref[...] = jnp.full(m_ref.shape, NEG, F32)
    l_ref[...] = jnp.zeros(l_ref.shape, F32)
    acc_ref[...] = jnp.zeros(acc_ref.shape, F32)

    def block(kb, masked):
        start = pl.multiple_of(kb * tk, tk)
        kblk = k_ref[pl.ds(start, tk), :]
        vext = jnp.concatenate([v_ref[pl.ds(start, tk), :], ones], axis=1)
        if masked:
            kpos = kb * tk + col
            mask = (kpos // CHUNK) <= q_chunk
            if kv_len % tk:
                mask = mask & (kpos < kv_len)
        maps = (0, 1)
        slabs = range(tk // LANES)
        s = [_dot_nt(q_maps[mp], kblk) for mp in maps]
        if masked:
            s = [jnp.where(mask, s[mp], NEG) for mp in maps]
        m_old = [m_ref[mp] for mp in maps]
        m_new = []
        for mp in maps:
            mx = s[mp][:, :LANES]
            for j in slabs[1:]:
                mx = jnp.maximum(mx, s[mp][:, j * LANES:(j + 1) * LANES])
            m_new.append(jnp.maximum(m_old[mp], jnp.max(mx, axis=-1, keepdims=True)))
        alpha = [jnp.exp2(m_old[mp] - m_new[mp]) for mp in maps]
        p = [jnp.concatenate([jnp.exp2(s[mp][:, j * LANES:(j + 1) * LANES] - m_new[mp]).astype(BF16)
                              for j in slabs], axis=1) for mp in maps]
        pv = [_dot(p[mp], vext) for mp in maps]
        for mp in maps:
            acc_ref[mp] = acc_ref[mp] * alpha[mp] + pv[mp][:, :LANES]
            l_ref[mp] = l_ref[mp] * alpha[mp] + pv[mp][:, LANES:]
            m_ref[mp] = m_new[mp]

    for j in reversed(range(n_diag)):
        block(blk0 + j, True)

    def body(i, carry):
        block(i, False)
        return carry

    lax.fori_loop(0, blk0, body, 0)

    lp = lam_ref[...]
    lam = (jnp.exp(jnp.sum(lp[0:1] * lp[1:2], axis=-1, keepdims=True))
           - jnp.exp(jnp.sum(lp[2:3] * lp[3:4], axis=-1, keepdims=True)) + lam_init)
    o = acc_ref[0] / l_ref[0] - lam * (acc_ref[1] / l_ref[1])
    o = o * lax.rsqrt(jnp.mean(o * o, axis=-1, keepdims=True) + EPS)
    o_ref[...] = (o * gain_ref[...] * (1.0 - lam_init)).astype(o_ref.dtype)


def _diff_attn(q, k, v, lam_params, gain, tq, tk, q_off, kv_len, lam_init):
    b, t, w = q.shape
    keys = k.shape[1]
    n_diag = max(1, tq // tk)
    assert q_off % tk == 0 and (tq % tk == 0 or t == tq) and keys % tk == 0 and CHUNK % 8 == 0
    assert (q_off + t + tk - 1) // tk * tk <= keys
    gain3 = gain.reshape(gain.shape[0], 1, gain.shape[1])
    qspec = pl.BlockSpec((None, tq, LANES), lambda bi, hd, qi: (bi, qi, hd))
    kspec = pl.BlockSpec((None, keys, LANES), lambda bi, hd, qi: (bi, 0, hd))
    return pl.pallas_call(
        functools.partial(_diff_kernel, tq=tq, tk=tk, q_off=q_off, kv_len=kv_len, n_diag=n_diag,
                          lam_init=lam_init),
        grid=(b, w // LANES, t // tq),
        in_specs=[pl.BlockSpec(lam_params.shape, lambda bi, hd, qi: (0, 0)),
                  pl.BlockSpec((None, 1, LANES), lambda bi, hd, qi: (hd, 0, 0)),
                  qspec, kspec, kspec],
        out_specs=qspec,
        out_shape=jax.ShapeDtypeStruct((b, t, w), BF16),
        scratch_shapes=[pltpu.VMEM((2, tq, LANES), F32)] * 3,
        compiler_params=_params(3),
    )(lam_params, gain3, q, k, v)


def _band_kernel(q_ref, k_ref, v_ref, bias_ref, o_ref, *, rows, n_sub, valid_lo, valid_hi):
    qi = pl.program_id(2)
    lane = lax.broadcasted_iota(jnp.int32, (rows, LANES), 1)
    first = lane < HEAD_DIM
    ones = jnp.ones((BAND_WINDOW, LANES), BF16)
    kcol = lax.broadcasted_iota(jnp.int32, (1, BAND_WINDOW), 1)

    def sub(j, carry):
        r0 = pl.multiple_of(j * rows, rows)
        ws = pl.multiple_of((qi * n_sub + j) * rows, rows)
        q2 = q_ref[pl.ds(r0, rows), :]
        zero = jnp.zeros_like(q2)
        kw = k_ref[pl.ds(ws, BAND_WINDOW), :]
        vext = jnp.concatenate([v_ref[pl.ds(ws, BAND_WINDOW), :], ones], axis=1)
        krow = ws + kcol
        penalty = jnp.where((krow >= valid_lo) & (krow < valid_hi), 0.0, NEG)
        outs = []
        for head in range(2):
            qh = jnp.where(first if head == 0 else jnp.logical_not(first), q2, zero)
            s = _dot_nt(qh, kw) + bias_ref[head] + penalty
            p = jnp.exp2(s - jnp.max(s, axis=-1, keepdims=True))
            pv = _dot(p.astype(BF16), vext)
            outs.append(pv[:, :LANES] / pv[:, LANES:])
        o_ref[pl.ds(r0, rows), :] = jnp.where(first, outs[0], outs[1]).astype(o_ref.dtype)
        return carry

    lax.fori_loop(0, n_sub, sub, 0)


def _band_bias_table(rel_bias, rows):
    i = jnp.arange(rows, dtype=jnp.int32)[:, None]
    rel_key = jnp.arange(BAND_WINDOW, dtype=jnp.int32)[None, :] - BAND_PAST
    idx = jnp.clip(i - rel_key, -REL_CLIP, REL_CLIP) + REL_CLIP
    kc = jnp.floor_divide(rel_key, CHUNK)
    qc = i // CHUNK
    in_band = (kc <= qc) & (kc >= qc - BAND_PAST // CHUNK)
    return jnp.where(in_band[None], rel_bias.astype(F32)[:, idx] * LOG2E, NEG)


def _band_attn(q, k, v, bias_tab, rows, n_sub, valid_lo, valid_hi):
    b, t, w = q.shape
    keys = k.shape[1]
    tq = rows * n_sub
    assert t % tq == 0 and (t - rows) + BAND_WINDOW <= keys
    qspec = pl.BlockSpec((None, tq, LANES), lambda bi, hp, qi: (bi, qi, hp))
    kspec = pl.BlockSpec((None, keys, LANES), lambda bi, hp, qi: (bi, 0, hp))
    return pl.pallas_call(
        functools.partial(_band_kernel, rows=rows, n_sub=n_sub, valid_lo=valid_lo, valid_hi=valid_hi),
        grid=(b, w // LANES, t // tq),
        in_specs=[qspec, kspec, kspec,
                  pl.BlockSpec((2, rows, BAND_WINDOW), lambda bi, hp, qi: (hp, 0, 0))],
        out_specs=qspec,
        out_shape=jax.ShapeDtypeStruct((b, t, w), BF16),
        compiler_params=_params(3),
    )(q, k, v, bias_tab)


def _rope_tables(pos, reps):
    half = HEAD_DIM // 2
    inv = ROPE_THETA ** (-jnp.arange(half, dtype=F32) / half)
    ang = pos.astype(F32)[:, None] * inv[None, :]
    cos, sin = jnp.cos(ang), jnp.sin(ang)
    cos = jnp.tile(cos, (reps, LANES // half))
    sin_signed = jnp.tile(jnp.concatenate([-sin, sin], axis=1), (reps, LANES // HEAD_DIM))
    return cos, sin_signed


def _with_past(past, new, total):
    b, t, w = new.shape
    pad = jnp.zeros((b, total - past.shape[1] - t, w), BF16)
    return jnp.concatenate([past.astype(BF16), new, pad], axis=1)


def _trunk(x, p, past, weights, cfg):
    (norm_mix, w_in_even, w_out_even, diff_lambda, diff_norm, w_in_odd, w_out_odd, rel_bias,
     norm_ffn, w_gate, w_up, w_down, norm_ple, w_ple_gate, w_ple, norm_final) = weights
    b, t, d = x.shape
    n = b * t
    depth = norm_mix.shape[0]
    tm = cfg["tm"]
    q_off = 0 if past is None else past[0].shape[2]
    pos = q_off + jnp.arange(t, dtype=jnp.int32)
    if past is None:
        cos, sin_signed = _rope_tables(pos, 1)
    else:
        cos, sin_signed = _rope_tables(pos, tm // t)
    row = lambda a: a.reshape(1, -1)
    seq = lambda a: a.reshape(b, t, a.shape[-1])

    h = x.reshape(n, d)
    sb_k, sb_v, df_k, df_v, bd_k, bd_v = [], [], [], [], [], []
    for li in range(depth):
        if li % 2 == 0:
            e = li // 2
            qa, ka, kab, va, vab, qb, kb, kbb, vb, vbb = _pre_even(
                h, row(norm_mix[li]), w_in_even[e], cos, sin_signed, tm)
            lam_init = 0.8 - 0.6 * math.exp(-0.3 * li)
            if past is None:
                keys = [seq(a) for a in (kab, vab, kbb, vbb)]
                kv_len = t
            else:
                total = cfg["keys_total"]
                flat = lambda c: c.reshape(c.shape[0], c.shape[1], -1)
                keys = [_with_past(flat(c[e]), seq(a), total)
                        for c, a in zip(past[:4], (kab, vab, kbb, vbb))]
                kv_len = q_off + t
            o_a = _sb_attn(seq(qa), keys[0], keys[1], cfg["sb_tq"], q_off)
            o_b = _diff_attn(seq(qb), keys[2], keys[3], diff_lambda[e], diff_norm[e],
                             cfg["diff_tq"], cfg["diff_tk"], q_off, kv_len, lam_init)
            sb_k.append(ka)
            sb_v.append(va)
            df_k.append(kb)
            df_v.append(vb)
            o_parts = [o_a.reshape(n, -1), o_b.reshape(n, -1)]
            w_out = w_out_even[e]
        else:
            od = li // 2
            q, k, kbf, v, vbf = _pre_odd(h, row(norm_mix[li]), w_in_odd[od], tm)
            rows = cfg["band_rows"]
            bias_tab = _band_bias_table(rel_bias[od], rows)
            if past is None:
                front = jnp.zeros((b, BAND_PAST, kbf.shape[1]), BF16)
                kk = jnp.concatenate([front, seq(kbf)], axis=1)
                vv = jnp.concatenate([front, seq(vbf)], axis=1)
                valid_lo, valid_hi = BAND_PAST, BAND_PAST + t
                keep = min(BAND_PAST, t)
                bd_k.append(seq(k)[:, t - keep:])
                bd_v.append(seq(v)[:, t - keep:])
            else:
                ck, cv = past[4][od], past[5][od]
                cache_rows = ck.shape[1]
                assert cache_rows == BAND_PAST
                flat = lambda c: c.reshape(c.shape[0], c.shape[1], -1)
                kk = _with_past(flat(ck), seq(kbf), BAND_WINDOW)
                vv = _with_past(flat(cv), seq(vbf), BAND_WINDOW)
                valid_lo, valid_hi = 0, cache_rows + t
                bd_k.append(seq(k))
                bd_v.append(seq(v))
            o = _band_attn(seq(q), kk, vv, bias_tab, rows, cfg["band_sub"], valid_lo, valid_hi)
            o_parts = [o.reshape(n, -1)]
            w_out = w_out_odd[od]
        gfin = row(norm_final) if li == depth - 1 else None
        h = _post(h, o_parts, p[li].reshape(n, -1), w_out, row(norm_ffn[li]), w_gate[li], w_up[li],
                  w_down[li], row(norm_ple[li]), w_ple_gate[li], w_ple[li], gfin, tm)
    y = h.reshape(b, t, d)
    heads = lambda xs, shp: jnp.stack([a.reshape((b, -1) + shp) for a in xs])
    state = (heads(sb_k, (H_SB, HEAD_DIM)), heads(sb_v, (H_SB, HEAD_DIM)),
             heads(df_k, (H_DIFF, 2, HEAD_DIM)), heads(df_v, (H_DIFF, 2 * HEAD_DIM)),
             heads(bd_k, (H_BAND, HEAD_DIM)), heads(bd_v, (H_BAND, HEAD_DIM)))
    return y, state


def kernel(x_prompt, x_sample, cache_sb_k, cache_sb_v, cache_diff_k, cache_diff_v, cache_band_k, cache_band_v, p_prompt, p_sample, norm_mix, w_in_even, w_out_even, diff_lambda, diff_norm, w_in_odd, w_out_odd, rel_bias, norm_ffn, w_gate, w_up, w_down, norm_ple, w_ple_gate, w_ple, norm_final):
    bf = lambda a: a.astype(BF16)
    weights = (norm_mix, bf(w_in_even), bf(w_out_even), diff_lambda, diff_norm, bf(w_in_odd),
               bf(w_out_odd), rel_bias, norm_ffn, bf(w_gate), bf(w_up), bf(w_down), norm_ple,
               bf(w_ple_gate), bf(w_ple), norm_final)
    t_p = x_prompt.shape[1]
    t_s = x_sample.shape[1]
    past_len = cache_sb_k.shape[2]
    diff_tk = 512
    cfg_p = dict(tm=512, sb_tq=256, diff_tq=512, diff_tk=diff_tk, band_rows=LANES,
                 band_sub=min(8, t_p // LANES))
    keys_total = -(-(past_len + t_s) // diff_tk) * diff_tk
    cfg_s = dict(tm=min(512, x_sample.shape[0] * t_s), sb_tq=t_s, diff_tq=t_s, diff_tk=diff_tk, band_rows=t_s, band_sub=1,
                 keys_total=keys_total)
    y_p, st_p = _trunk(x_prompt, p_prompt, None, weights, cfg_p)
    past = (cache_sb_k, cache_sb_v, cache_diff_k, cache_diff_v, cache_band_k, cache_band_v)
    y_s, st_s = _trunk(x_sample, p_sample, past, weights, cfg_s)
    return (y_p, y_s) + tuple(st_p) + tuple(st_s)
```

```python
import functools
import math

import jax
import jax.numpy as jnp
from jax import lax
from jax.experimental import pallas as pl
from jax.experimental.pallas import tpu as pltpu

CHUNK = 64
HEAD_DIM = 64
H_SB = 8
H_DIFF = 4
H_BAND = 16
BAND_PAST = 8 * CHUNK
REL_CLIP = 128
ROPE_THETA = 10000.0
EPS = 1e-6
NEG = -1e30
SCALE = HEAD_DIM ** -0.5
LOG2E = 1.4426950408889634
QSCALE = SCALE * LOG2E

LANES = 128
VMEM_LIMIT = 56 * 1024 * 1024

SB_KEY_BLOCK = 2 * LANES
SB_EXIT_LOG2 = -160.0
BAND_WINDOW = BAND_PAST + LANES
BAND_UNROLL = 4

F32 = jnp.float32
BF16 = jnp.bfloat16


def _rms(x, g):
    return x * lax.rsqrt(jnp.mean(x * x, axis=-1, keepdims=True) + EPS) * g


def _sigmoid(x):
    return 1.0 / (1.0 + jnp.exp(-x))


def _dot(a, b):
    return jnp.dot(a, b, preferred_element_type=F32)


def _dot_nt(a, b):
    return lax.dot_general(a, b, (((1,), (1,)), ((), ())), preferred_element_type=F32)


def _params(n_axes):
    return pltpu.CompilerParams(dimension_semantics=("arbitrary",) * n_axes,
                                vmem_limit_bytes=VMEM_LIMIT)


def _rope(x, cos, sin_signed, first_half):
    outs = []
    for j in range(x.shape[1] // LANES):
        xj = x[:, j * LANES:(j + 1) * LANES]
        partner = jnp.where(first_half, pltpu.roll(xj, LANES - HEAD_DIM // 2, 1),
                            pltpu.roll(xj, HEAD_DIM // 2, 1))
        outs.append(xj * cos + partner * sin_signed)
    return jnp.concatenate(outs, axis=1)


def _pre_even_kernel(h_ref, g_ref, w_ref, cos_ref, sin_ref,
                     qa_ref, ka_ref, kab_ref, va_ref, vab_ref,
                     qb_ref, kb_ref, kbb_ref, vb_ref, vbb_ref):
    hn = _rms(h_ref[...], g_ref[...]).astype(BF16)
    width = qa_ref.shape[1]

    def proj(c):
        return _dot(hn, w_ref[:, c * width:(c + 1) * width])

    cos = cos_ref[...]
    sin_signed = sin_ref[...]
    lane = lax.broadcasted_iota(jnp.int32, cos.shape, 1)
    first_half = (lane % HEAD_DIM) < (HEAD_DIM // 2)

    qa_ref[...] = (proj(0) * QSCALE).astype(BF16)
    ka = proj(1)
    ka_ref[...] = ka
    kab_ref[...] = ka.astype(BF16)
    va = proj(2)
    va_ref[...] = va
    vab_ref[...] = va.astype(BF16)
    qb_ref[...] = (_rope(proj(3), cos, sin_signed, first_half) * QSCALE).astype(BF16)
    kb = _rope(proj(4), cos, sin_signed, first_half)
    kb_ref[...] = kb
    kbb_ref[...] = kb.astype(BF16)
    vb = proj(5)
    vb_ref[...] = vb
    vbb_ref[...] = vb.astype(BF16)


def _pre_odd_kernel(h_ref, g_ref, w_ref, q_ref, k_ref, kb_ref, v_ref, vb_ref):
    hn = _rms(h_ref[...], g_ref[...]).astype(BF16)
    width = q_ref.shape[1]
    q_ref[...] = (_dot(hn, w_ref[:, :width]) * QSCALE).astype(BF16)
    k = _dot(hn, w_ref[:, width:2 * width])
    k_ref[...] = k
    kb_ref[...] = k.astype(BF16)
    v = _dot(hn, w_ref[:, 2 * width:])
    v_ref[...] = v
    vb_ref[...] = v.astype(BF16)


def _pre_even(h, g, w, cos, sin_signed, tm):
    n, d = h.shape
    width = w.shape[1] // 6
    n_pos_blocks = cos.shape[0] // tm
    tok = lambda wd: pl.BlockSpec((tm, wd), lambda i: (i, 0))
    const = lambda a: pl.BlockSpec(a.shape, lambda i: (0, 0))
    pos = pl.BlockSpec((tm, LANES), lambda i: (i % n_pos_blocks, 0))
    f32o = jax.ShapeDtypeStruct((n, width), F32)
    bf16o = jax.ShapeDtypeStruct((n, width), BF16)
    return pl.pallas_call(
        _pre_even_kernel,
        grid=(n // tm,),
        in_specs=[tok(d), const(g), const(w), pos, pos],
        out_specs=[tok(width)] * 10,
        out_shape=[bf16o, f32o, bf16o, f32o, bf16o, bf16o, f32o, bf16o, f32o, bf16o],
        compiler_params=_params(1),
    )(h, g, w, cos, sin_signed)


def _pre_odd(h, g, w, tm):
    n, d = h.shape
    width = w.shape[1] // 3
    tok = lambda wd: pl.BlockSpec((tm, wd), lambda i: (i, 0))
    const = lambda a: pl.BlockSpec(a.shape, lambda i: (0, 0))
    f32o = jax.ShapeDtypeStruct((n, width), F32)
    bf16o = jax.ShapeDtypeStruct((n, width), BF16)
    return pl.pallas_call(
        _pre_odd_kernel,
        grid=(n // tm,),
        in_specs=[tok(d), const(g), const(w)],
        out_specs=[tok(width)] * 5,
        out_shape=[bf16o, f32o, bf16o, f32o, bf16o],
        compiler_params=_params(1),
    )(h, g, w)


def _post_kernel(*refs, n_o, ffn_chunk, final):
    h_ref = refs[0]
    o_refs = refs[1:1 + n_o]
    (p_ref, wout_ref, gffn_ref, wg_ref, wu_ref, wd_ref, gple_ref, wpg_ref, wp_ref) = refs[1 + n_o:10 + n_o]
    gfin_ref = refs[10 + n_o] if final else None
    out_ref = refs[-1]

    h = h_ref[...]
    off = 0
    for o_ref in o_refs:
        wd = o_ref.shape[1]
        h = h + _dot(o_ref[...], wout_ref[off:off + wd, :])
        off += wd

    hn = _rms(h, gffn_ref[...]).astype(BF16)
    hidden = wg_ref.shape[1]
    for c in range(hidden // ffn_chunk):
        cols = slice(c * ffn_chunk, (c + 1) * ffn_chunk)
        gt = _dot(hn, wg_ref[:, cols])
        up = _dot(hn, wu_ref[:, cols])
        act = (gt * _sigmoid(gt) * up).astype(BF16)
        h = h + _dot(act, wd_ref[cols, :])

    gate = _sigmoid(_dot(_rms(h, gple_ref[...]).astype(BF16), wpg_ref[...]))
    h = h + _dot(p_ref[...].astype(BF16), wp_ref[...]) * gate
    if final:
        h = _rms(h, gfin_ref[...])
    out_ref[...] = h


def _post(h, o_parts, p, wout, gffn, wg, wu, wd, gple, wpg, wp, gfin, tm):
    n, d = h.shape
    final = gfin is not None
    hidden = wg.shape[1]
    ffn_chunk = hidden
    tok = lambda a: pl.BlockSpec((tm, a.shape[1]), lambda i: (i, 0))
    const = lambda a: pl.BlockSpec(a.shape, lambda i: (0, 0), pipeline_mode=pl.Buffered(1))
    consts = [wout, gffn, wg, wu, wd, gple, wpg, wp] + ([gfin] if final else [])
    return pl.pallas_call(
        functools.partial(_post_kernel, n_o=len(o_parts), ffn_chunk=ffn_chunk, final=final),
        grid=(n // tm,),
        in_specs=[tok(h)] + [tok(o) for o in o_parts] + [tok(p)] + [const(c) for c in consts],
        out_specs=tok(h),
        out_shape=jax.ShapeDtypeStruct((n, d), F32),
        compiler_params=_params(1),
    )(h, *o_parts, p, *consts)


def _sb_kernel(q_ref, k_ref, v_ref, tt_ref, o_ref, acc_ref, run_ref, *, tq, q_off, n_diag):
    qi = pl.program_id(2)
    q2 = q_ref[...]
    lane = lax.broadcasted_iota(jnp.int32, (tq, LANES), 1)
    col = lax.broadcasted_iota(jnp.int32, (tq, SB_KEY_BLOCK), 1)
    row_pos = q_off + qi * tq + lax.broadcasted_iota(jnp.int32, (tq, SB_KEY_BLOCK), 0)
    first = lane < HEAD_DIM
    blk0 = (q_off + qi * tq) // SB_KEY_BLOCK
    tt = tt_ref[...]
    zero = jnp.zeros_like(q2)
    heads = (0, 1)
    q_heads = (jnp.where(first, q2, zero), jnp.where(first, zero, q2))
    acc_ref[...] = jnp.zeros(acc_ref.shape, F32)
    run_ref[...] = jnp.zeros(run_ref.shape, F32)

    def block(kb, masked):
        start = pl.multiple_of(kb * SB_KEY_BLOCK, SB_KEY_BLOCK)
        kblk = k_ref[pl.ds(start, SB_KEY_BLOCK), :]
        vblk = v_ref[pl.ds(start, SB_KEY_BLOCK), :]
        s = [_dot_nt(q_heads[h], kblk) for h in heads]
        soft = [jnp.log(1.0 + jnp.exp2(-jnp.abs(s[h]))) * LOG2E for h in heads]
        log_beta = [jnp.minimum(s[h], 0.0) - soft[h] for h in heads]
        log_keep = [log_beta[h] - s[h] for h in heads]
        if masked:
            mask = (kb * SB_KEY_BLOCK + col) < row_pos
            log_keep = [jnp.where(mask, log_keep[h], 0.0) for h in heads]
        hi = [log_keep[h].astype(BF16) for h in heads]
        lo = [(log_keep[h] - hi[h].astype(F32)).astype(BF16) for h in heads]
        old = slice(0, LANES)
        new = slice(LANES, SB_KEY_BLOCK)
        cs_new = [_dot(jnp.concatenate([hi[h][:, new], lo[h][:, new]], axis=1), tt) for h in heads]
        cs_old = [_dot(jnp.concatenate([hi[h][:, old], lo[h][:, old]], axis=1), tt) for h in heads]
        for h in heads:
            run = run_ref[h]
            run_old = run + cs_new[h][:, LANES:]
            w = jnp.concatenate([jnp.exp2(log_beta[h][:, old] + cs_old[h][:, :LANES] + run_old),
                                 jnp.exp2(log_beta[h][:, new] + cs_new[h][:, :LANES] + run)], axis=1)
            if masked:
                w = jnp.where(mask, w, 0.0)
            acc_ref[h] += _dot(w.astype(BF16), vblk)
            run_ref[h] = run_old + cs_old[h][:, LANES:]

    for j in reversed(range(n_diag)):
        block(blk0 + j, True)

    def cond(carry):
        i, live = carry
        return jnp.logical_and(i < blk0, live > SB_EXIT_LOG2)

    def body(carry):
        i, _ = carry
        live = jnp.max(jnp.maximum(run_ref[0], run_ref[1]))
        block(blk0 - 1 - i, False)
        return i + 1, live

    lax.while_loop(cond, body, (jnp.int32(0), jnp.float32(0.0)))
    o_ref[...] = jnp.where(first, acc_ref[0], acc_ref[1]).astype(o_ref.dtype)


def _suffix_matrix():
    r = lax.broadcasted_iota(jnp.int32, (2 * LANES, 2 * LANES), 0) % LANES
    c = lax.broadcasted_iota(jnp.int32, (2 * LANES, 2 * LANES), 1)
    return jnp.where((c >= LANES) | (r > c), 1.0, 0.0).astype(BF16)


def _sb_attn(q, k, v, tq, q_off):
    b, t, w = q.shape
    keys = k.shape[1]
    n_diag = -(-tq // SB_KEY_BLOCK)
    assert (q_off % SB_KEY_BLOCK == 0) and (tq % SB_KEY_BLOCK == 0 or t == tq)
    assert q_off + (t - tq) + n_diag * SB_KEY_BLOCK <= keys
    tt = _suffix_matrix()
    qspec = pl.BlockSpec((None, tq, LANES), lambda bi, hp, qi: (bi, qi, hp))
    kspec = pl.BlockSpec((None, keys, LANES), lambda bi, hp, qi: (bi, 0, hp))
    return pl.pallas_call(
        functools.partial(_sb_kernel, tq=tq, q_off=q_off, n_diag=n_diag),
        grid=(b, w // LANES, t // tq),
        in_specs=[qspec, kspec, kspec, pl.BlockSpec(tt.shape, lambda bi, hp, qi: (0, 0))],
        out_specs=qspec,
        out_shape=jax.ShapeDtypeStruct((b, t, w), BF16),
        scratch_shapes=[pltpu.VMEM((2, tq, LANES), F32), pltpu.VMEM((2, tq, LANES), F32)],
        compiler_params=_params(3),
    )(q, k, v, tt)


def _diff_kernel(lam_ref, gain_ref, q_ref, k_ref, v_ref, o_ref, m_ref, l_ref, acc_ref,
                 *, tq, tk, q_off, kv_len, n_diag, lam_init):
    qi = pl.program_id(2)
    q2 = q_ref[...]
    lane = lax.broadcasted_iota(jnp.int32, (tq, LANES), 1)
    first = lane < HEAD_DIM
    zero = jnp.zeros_like(q2)
    q_maps = (jnp.where(first, q2, zero), jnp.where(first, zero, q2))
    q_chunk = (q_off + qi * tq + lax.broadcasted_iota(jnp.int32, (tq, tk), 0)) // CHUNK
    col = lax.broadcasted_iota(jnp.int32, (tq, tk), 1)
    blk0 = (q_off + qi * tq) // tk
    ones = jnp.ones((tk, LANES), BF16)

    m_ref[...] = jnp.full(m_ref.shape, NEG, F32)
    l_ref[...] = jnp.zeros(l_ref.shape, F32)
    acc_ref[...] = jnp.zeros(acc_ref.shape, F32)

    def block(kb, masked):
        start = pl.multiple_of(kb * tk, tk)
        kblk = k_ref[pl.ds(start, tk), :]
        vext = jnp.concatenate([v_ref[pl.ds(start, tk), :], ones], axis=1)
        if masked:
            kpos = kb * tk + col
            mask = (kpos // CHUNK) <= q_chunk
            if kv_len % tk:
                mask = mask & (kpos < kv_len)
        maps = (0, 1)
        slabs = range(tk // LANES)
        s = [_dot_nt(q_maps[mp], kblk) for mp in maps]
        if masked:
            s = [jnp.where(mask, s[mp], NEG) for mp in maps]
        m_old = [m_ref[mp] for mp in maps]
        m_new = []
        for mp in maps:
            mx = s[mp][:, :LANES]
            for j in slabs[1:]:
                mx = jnp.maximum(mx, s[mp][:, j * LANES:(j + 1) * LANES])
            m_new.append(jnp.maximum(m_old[mp], jnp.max(mx, axis=-1, keepdims=True)))
        alpha = [jnp.exp2(m_old[mp] - m_new[mp]) for mp in maps]
        p = [jnp.concatenate([jnp.exp2(s[mp][:, j * LANES:(j + 1) * LANES] - m_new[mp]).astype(BF16)
                              for j in slabs], axis=1) for mp in maps]
        pv = [_dot(p[mp], vext) for mp in maps]
        for mp in maps:
            acc_ref[mp] = acc_ref[mp] * alpha[mp] + pv[mp][:, :LANES]
            l_ref[mp] = l_ref[mp] * alpha[mp] + pv[mp][:, LANES:]
            m_ref[mp] = m_new[mp]

    for j in reversed(range(n_diag)):
        block(blk0 + j, True)

    def body(i, carry):
        block(2 * i, False)
        block(2 * i + 1, False)
        return carry

    lax.fori_loop(0, blk0 // 2, body, 0)

    @pl.when(blk0 % 2 == 1)
    def _():
        block(blk0 - 1, False)

    lp = lam_ref[...]
    lam = (jnp.exp(jnp.sum(lp[0:1] * lp[1:2], axis=-1, keepdims=True))
           - jnp.exp(jnp.sum(lp[2:3] * lp[3:4], axis=-1, keepdims=True)) + lam_init)
    o = acc_ref[0] / l_ref[0] - lam * (acc_ref[1] / l_ref[1])
    o = o * lax.rsqrt(jnp.mean(o * o, axis=-1, keepdims=True) + EPS)
    o_ref[...] = (o * gain_ref[...] * (1.0 - lam_init)).astype(o_ref.dtype)


def _diff_attn(q, k, v, lam_params, gain, tq, tk, q_off, kv_len, lam_init):
    b, t, w = q.shape
    keys = k.shape[1]
    n_diag = max(1, tq // tk)
    assert q_off % tk == 0 and (tq % tk == 0 or t == tq) and keys % tk == 0 and CHUNK % 8 == 0
    assert (q_off + t + tk - 1) // tk * tk <= keys
    gain3 = gain.reshape(gain.shape[0], 1, gain.shape[1])
    qspec = pl.BlockSpec((None, tq, LANES), lambda bi, hd, qi: (bi, qi, hd))
    kspec = pl.BlockSpec((None, keys, LANES), lambda bi, hd, qi: (bi, 0, hd))
    return pl.pallas_call(
        functools.partial(_diff_kernel, tq=tq, tk=tk, q_off=q_off, kv_len=kv_len, n_diag=n_diag,
                          lam_init=lam_init),
        grid=(b, w // LANES, t // tq),
        in_specs=[pl.BlockSpec(lam_params.shape, lambda bi, hd, qi: (0, 0)),
                  pl.BlockSpec((None, 1, LANES), lambda bi, hd, qi: (hd, 0, 0)),
                  qspec, kspec, kspec],
        out_specs=qspec,
        out_shape=jax.ShapeDtypeStruct((b, t, w), BF16),
        scratch_shapes=[pltpu.VMEM((2, tq, LANES), F32)] * 3,
        compiler_params=_params(3),
    )(lam_params, gain3, q, k, v)


def _band_kernel(q_ref, k_ref, v_ref, bias_ref, o_ref, *, rows, n_sub, valid_lo, valid_hi):
    qi = pl.program_id(2)
    lane = lax.broadcasted_iota(jnp.int32, (rows, LANES), 1)
    first = lane < HEAD_DIM
    ones = jnp.ones((BAND_WINDOW, LANES), BF16)
    kcol = lax.broadcasted_iota(jnp.int32, (1, BAND_WINDOW), 1)

    def sub(j, carry):
        r0 = pl.multiple_of(j * rows, rows)
        ws = pl.multiple_of((qi * n_sub + j) * rows, rows)
        q2 = q_ref[pl.ds(r0, rows), :]
        zero = jnp.zeros_like(q2)
        kw = k_ref[pl.ds(ws, BAND_WINDOW), :]
        vext = jnp.concatenate([v_ref[pl.ds(ws, BAND_WINDOW), :], ones], axis=1)
        krow = ws + kcol
        penalty = jnp.where((krow >= valid_lo) & (krow < valid_hi), 0.0, NEG)
        outs = []
        for head in range(2):
            qh = jnp.where(first if head == 0 else jnp.logical_not(first), q2, zero)
            s = _dot_nt(qh, kw) + bias_ref[head] + penalty
            p = jnp.exp2(s - jnp.max(s, axis=-1, keepdims=True))
            pv = _dot(p.astype(BF16), vext)
            outs.append(pv[:, :LANES] / pv[:, LANES:])
        o_ref[pl.ds(r0, rows), :] = jnp.where(first, outs[0], outs[1]).astype(o_ref.dtype)
        return carry

    lax.fori_loop(0, n_sub, sub, 0, unroll=BAND_UNROLL if n_sub % BAND_UNROLL == 0 else 1)


def _band_bias_table(rel_bias, rows):
    heads = rel_bias.shape[0]
    span = BAND_WINDOW + rows - 1
    n_far = BAND_PAST - REL_CLIP + rows
    rb = rel_bias.astype(F32) * LOG2E
    near = rb[:, 2 * REL_CLIP - 1:0:-1]
    assert n_far + near.shape[1] == span
    e = jnp.concatenate([jnp.broadcast_to(rb[:, -1:], (heads, n_far)), near,
                         jnp.zeros((heads, 1), F32)], axis=1)
    skew = jnp.broadcast_to(e[:, None, :], (heads, rows, span + 1)).reshape(heads, rows * (span + 1))
    skew = skew[:, :rows * span].reshape(heads, rows, span)
    bias = skew[:, :, rows - 1:]
    i = jnp.arange(rows, dtype=jnp.int32)[:, None]
    rel_key = jnp.arange(BAND_WINDOW, dtype=jnp.int32)[None, :] - BAND_PAST
    kc = jnp.floor_divide(rel_key, CHUNK)
    qc = i // CHUNK
    in_band = (kc <= qc) & (kc >= qc - BAND_PAST // CHUNK)
    return jnp.where(in_band[None], bias, NEG)


def _band_attn(q, k, v, bias_tab, rows, n_sub, valid_lo, valid_hi):
    b, t, w = q.shape
    keys = k.shape[1]
    tq = rows * n_sub
    assert t % tq == 0 and (t - rows) + BAND_WINDOW <= keys
    qspec = pl.BlockSpec((None, tq, LANES), lambda bi, hp, qi: (bi, qi, hp))
    kspec = pl.BlockSpec((None, keys, LANES), lambda bi, hp, qi: (bi, 0, hp))
    return pl.pallas_call(
        functools.partial(_band_kernel, rows=rows, n_sub=n_sub, valid_lo=valid_lo, valid_hi=valid_hi),
        grid=(b, w // LANES, t // tq),
        in_specs=[qspec, kspec, kspec,
                  pl.BlockSpec((2, rows, BAND_WINDOW), lambda bi, hp, qi: (hp, 0, 0))],
        out_specs=qspec,
        out_shape=jax.ShapeDtypeStruct((b, t, w), BF16),
        compiler_params=_params(3),
    )(q, k, v, bias_tab)


def _rope_tables(pos, reps):
    half = HEAD_DIM // 2
    inv = ROPE_THETA ** (-jnp.arange(half, dtype=F32) / half)
    ang = pos.astype(F32)[:, None] * inv[None, :]
    cos, sin = jnp.cos(ang), jnp.sin(ang)
    cos = jnp.tile(cos, (reps, LANES // half))
    sin_signed = jnp.tile(jnp.concatenate([-sin, sin], axis=1), (reps, LANES // HEAD_DIM))
    return cos, sin_signed


def _with_past(past, new, total):
    b, t, w = new.shape
    pad = jnp.zeros((b, total - past.shape[1] - t, w), BF16)
    return jnp.concatenate([past.astype(BF16), new, pad], axis=1)


def _trunk(x, p, past, weights, cfg):
    (norm_mix, w_in_even, w_out_even, diff_lambda, diff_norm, w_in_odd, w_out_odd, rel_bias,
     norm_ffn, w_gate, w_up, w_down, norm_ple, w_ple_gate, w_ple, norm_final) = weights
    b, t, d = x.shape
    n = b * t
    depth = norm_mix.shape[0]
    tm = cfg["tm"]
    q_off = 0 if past is None else past[0].shape[2]
    pos = q_off + jnp.arange(t, dtype=jnp.int32)
    if past is None:
        cos, sin_signed = _rope_tables(pos, 1)
    else:
        cos, sin_signed = _rope_tables(pos, tm // t)
    row = lambda a: a.reshape(1, -1)
    seq = lambda a: a.reshape(b, t, a.shape[-1])

    h = x.reshape(n, d)
    sb_k, sb_v, df_k, df_v, bd_k, bd_v = [], [], [], [], [], []
    for li in range(depth):
        if li % 2 == 0:
            e = li // 2
            qa, ka, kab, va, vab, qb, kb, kbb, vb, vbb = _pre_even(
                h, row(norm_mix[li]), w_in_even[e], cos, sin_signed, tm)
            lam_init = 0.8 - 0.6 * math.exp(-0.3 * li)
            if past is None:
                keys = [seq(a) for a in (kab, vab, kbb, vbb)]
                kv_len = t
            else:
                total = cfg["keys_total"]
                flat = lambda c: c.reshape(c.shape[0], c.shape[1], -1)
                keys = [_with_past(flat(c[e]), seq(a), total)
                        for c, a in zip(past[:4], (kab, vab, kbb, vbb))]
                kv_len = q_off + t
            o_a = _sb_attn(seq(qa), keys[0], keys[1], cfg["sb_tq"], q_off)
            o_b = _diff_attn(seq(qb), keys[2], keys[3], diff_lambda[e], diff_norm[e],
                             cfg["diff_tq"], cfg["diff_tk"], q_off, kv_len, lam_init)
            sb_k.append(ka)
            sb_v.append(va)
            df_k.append(kb)
            df_v.append(vb)
            o_parts = [o_a.reshape(n, -1), o_b.reshape(n, -1)]
            w_out = w_out_even[e]
        else:
            od = li // 2
            q, k, kbf, v, vbf = _pre_odd(h, row(norm_mix[li]), w_in_odd[od], tm)
            rows = cfg["band_rows"]
            bias_tab = _band_bias_table(rel_bias[od], rows)
            if past is None:
                front = jnp.zeros((b, BAND_PAST, kbf.shape[1]), BF16)
                kk = jnp.concatenate([front, seq(kbf)], axis=1)
                vv = jnp.concatenate([front, seq(vbf)], axis=1)
                valid_lo, valid_hi = BAND_PAST, BAND_PAST + t
                keep = min(BAND_PAST, t)
                bd_k.append(seq(k)[:, t - keep:])
                bd_v.append(seq(v)[:, t - keep:])
            else:
                ck, cv = past[4][od], past[5][od]
                cache_rows = ck.shape[1]
                assert cache_rows == BAND_PAST
                flat = lambda c: c.reshape(c.shape[0], c.shape[1], -1)
                kk = _with_past(flat(ck), seq(kbf), BAND_WINDOW)
                vv = _with_past(flat(cv), seq(vbf), BAND_WINDOW)
                valid_lo, valid_hi = 0, cache_rows + t
                bd_k.append(seq(k))
                bd_v.append(seq(v))
            o = _band_attn(seq(q), kk, vv, bias_tab, rows, cfg["band_sub"], valid_lo, valid_hi)
            o_parts = [o.reshape(n, -1)]
            w_out = w_out_odd[od]
        gfin = row(norm_final) if li == depth - 1 else None
        h = _post(h, o_parts, p[li].reshape(n, -1), w_out, row(norm_ffn[li]), w_gate[li], w_up[li],
                  w_down[li], row(norm_ple[li]), w_ple_gate[li], w_ple[li], gfin, tm)
    y = h.reshape(b, t, d)
    heads = lambda xs, shp: jnp.stack([a.reshape((b, -1) + shp) for a in xs])
    state = (heads(sb_k, (H_SB, HEAD_DIM)), heads(sb_v, (H_SB, HEAD_DIM)),
             heads(df_k, (H_DIFF, 2, HEAD_DIM)), heads(df_v, (H_DIFF, 2 * HEAD_DIM)),
             heads(bd_k, (H_BAND, HEAD_DIM)), heads(bd_v, (H_BAND, HEAD_DIM)))
    return y, state


def kernel(x_prompt, x_sample, cache_sb_k, cache_sb_v, cache_diff_k, cache_diff_v, cache_band_k, cache_band_v, p_prompt, p_sample, norm_mix, w_in_even, w_out_even, diff_lambda, diff_norm, w_in_odd, w_out_odd, rel_bias, norm_ffn, w_gate, w_up, w_down, norm_ple, w_ple_gate, w_ple, norm_final):
    bf = lambda a: a.astype(BF16)
    weights = (norm_mix, bf(w_in_even), bf(w_out_even), diff_lambda, diff_norm, bf(w_in_odd),
               bf(w_out_odd), rel_bias, norm_ffn, bf(w_gate), bf(w_up), bf(w_down), norm_ple,
               bf(w_ple_gate), bf(w_ple), norm_final)
    t_p = x_prompt.shape[1]
    t_s = x_sample.shape[1]
    past_len = cache_sb_k.shape[2]
    diff_tk = 512
    cfg_p = dict(tm=512, sb_tq=256, diff_tq=512, diff_tk=diff_tk, band_rows=LANES,
                 band_sub=min(8, t_p // LANES))
    keys_total = -(-(past_len + t_s) // diff_tk) * diff_tk
    cfg_s = dict(tm=min(512, x_sample.shape[0] * t_s), sb_tq=t_s, diff_tq=t_s, diff_tk=diff_tk, band_rows=t_s, band_sub=1,
                 keys_total=keys_total)
    y_p, st_p = _trunk(x_prompt, p_prompt, None, weights, cfg_p)
    past = (cache_sb_k, cache_sb_v, cache_diff_k, cache_diff_v, cache_band_k, cache_band_v)
    y_s, st_s = _trunk(x_sample, p_sample, past, weights, cfg_s)
    return (y_p, y_s) + tuple(st_p) + tuple(st_s)
```

```python
import functools
import math

import jax
import jax.numpy as jnp
from jax import lax
from jax.experimental import pallas as pl
from jax.experimental.pallas import tpu as pltpu

CHUNK = 64
HEAD_DIM = 64
H_SB = 8
H_DIFF = 4
H_BAND = 16
BAND_PAST = 8 * CHUNK
REL_CLIP = 128
ROPE_THETA = 10000.0
EPS = 1e-6
NEG = -1e30
SCALE = HEAD_DIM ** -0.5
LOG2E = 1.4426950408889634
QSCALE = SCALE * LOG2E

LANES = 128
VMEM_LIMIT = 56 * 1024 * 1024

SB_KEY_BLOCK = 2 * LANES
SB_EXIT_LOG2 = -160.0
BAND_WINDOW = BAND_PAST + LANES
BAND_UNROLL = 4

F32 = jnp.float32
BF16 = jnp.bfloat16


def _rms(x, g):
    return x * lax.rsqrt(jnp.mean(x * x, axis=-1, keepdims=True) + EPS) * g


def _sigmoid(x):
    return 1.0 / (1.0 + jnp.exp(-x))


def _dot(a, b):
    return jnp.dot(a, b, preferred_element_type=F32)


def _dot_nt(a, b):
    return lax.dot_general(a, b, (((1,), (1,)), ((), ())), preferred_element_type=F32)


def _params(n_axes):
    return pltpu.CompilerParams(dimension_semantics=("arbitrary",) * n_axes,
                                vmem_limit_bytes=VMEM_LIMIT)


def _rope(x, cos, sin_signed, first_half):
    outs = []
    for j in range(x.shape[1] // LANES):
        xj = x[:, j * LANES:(j + 1) * LANES]
        partner = jnp.where(first_half, pltpu.roll(xj, LANES - HEAD_DIM // 2, 1),
                            pltpu.roll(xj, HEAD_DIM // 2, 1))
        outs.append(xj * cos + partner * sin_signed)
    return jnp.concatenate(outs, axis=1)


def _pre_even_kernel(h_ref, g_ref, w_ref, cos_ref, sin_ref, *rest):
    (qa_ref, ka_ref, kab_ref, va_ref, vab_ref, qb_ref, kb_ref, kbb_ref, vb_ref, vbb_ref) = rest[-10:]
    hn = _rms(h_ref[...], g_ref[...]).astype(BF16)
    width = qa_ref.shape[1]

    def proj(c):
        return _dot(hn, w_ref[:, c * width:(c + 1) * width])

    cos = cos_ref[...]
    sin_signed = sin_ref[...]
    lane = lax.broadcasted_iota(jnp.int32, cos.shape, 1)
    first_half = (lane % HEAD_DIM) < (HEAD_DIM // 2)

    qa_ref[...] = (proj(0) * QSCALE).astype(BF16)
    ka = proj(1)
    ka_ref[...] = ka
    kab_ref[...] = ka.astype(BF16)
    va = proj(2)
    va_ref[...] = va
    vab_ref[...] = va.astype(BF16)
    qb_ref[...] = (_rope(proj(3), cos, sin_signed, first_half) * QSCALE).astype(BF16)
    kb = _rope(proj(4), cos, sin_signed, first_half)
    kb_ref[...] = kb
    kbb_ref[...] = kb.astype(BF16)
    vb = proj(5)
    vb_ref[...] = vb
    vbb_ref[...] = vb.astype(BF16)


def _pre_odd_kernel(h_ref, g_ref, w_ref, *rest, padded):
    q_ref, k_ref, kb_ref, v_ref, vb_ref = rest[-5:]
    width = q_ref.shape[1]

    def project(keep_f32):
        hn = _rms(h_ref[...], g_ref[...]).astype(BF16)
        q_ref[...] = (_dot(hn, w_ref[:, :width]) * QSCALE).astype(BF16)
        k = _dot(hn, w_ref[:, width:2 * width])
        kb_ref[...] = k.astype(BF16)
        v = _dot(hn, w_ref[:, 2 * width:])
        vb_ref[...] = v.astype(BF16)
        keep_f32(k, v)

    def store_f32(k, v):
        k_ref[...] = k
        v_ref[...] = v

    if not padded:
        project(store_f32)
        return

    j = pl.program_id(1)

    @pl.when(j == 0)
    def _():
        kb_ref[...] = jnp.zeros(kb_ref.shape, BF16)
        vb_ref[...] = jnp.zeros(vb_ref.shape, BF16)

    @pl.when(j > 0)
    def _():
        project(lambda k, v: pl.when(j == pl.num_programs(1) - 1)(lambda: store_f32(k, v)))


def _stack_alias(prev, n_fixed_inputs, out_positions):
    if prev is None:
        return [], [], {}
    specs = [pl.BlockSpec(memory_space=pl.ANY)] * len(prev)
    aliases = {n_fixed_inputs + i: o for i, o in enumerate(out_positions)}
    return list(prev), specs, aliases


def _pre_even(h, g, w, cos, sin_signed, tm, slot, n_slots, prev):
    n, d = h.shape
    width = w.shape[1] // 6
    n_pos_blocks = cos.shape[0] // tm
    tok = lambda wd: pl.BlockSpec((tm, wd), lambda i: (i, 0))
    const = lambda a: pl.BlockSpec(a.shape, lambda i: (0, 0))
    pos = pl.BlockSpec((tm, LANES), lambda i: (i % n_pos_blocks, 0))
    stk = pl.BlockSpec((None, tm, width), lambda i: (slot, i, 0))
    f32s = jax.ShapeDtypeStruct((n_slots, n, width), F32)
    bf16o = jax.ShapeDtypeStruct((n, width), BF16)
    t, s = tok(width), stk
    prev_args, prev_specs, aliases = _stack_alias(prev, 5, (1, 3, 6, 8))
    return pl.pallas_call(
        _pre_even_kernel,
        grid=(n // tm,),
        in_specs=[tok(d), const(g), const(w), pos, pos] + prev_specs,
        out_specs=[t, s, t, s, t, t, s, t, s, t],
        out_shape=[bf16o, f32s, bf16o, f32s, bf16o, bf16o, f32s, bf16o, f32s, bf16o],
        input_output_aliases=aliases,
        compiler_params=_params(1),
    )(h, g, w, cos, sin_signed, *prev_args)


def _pre_odd(h, g, w, tm, slot, n_slots, prev, batch=None):
    n, d = h.shape
    width = w.shape[1] // 3
    padded = batch is not None
    bf16o = jax.ShapeDtypeStruct((n, width), BF16)
    if padded:
        assert tm == BAND_PAST
        tiles = n // batch // tm
        grid = (batch, tiles + 1)
        tile = lambda bi, j: bi * tiles + jnp.maximum(j - 1, 0)
        tok = lambda wd: pl.BlockSpec((tm, wd), lambda bi, j: (tile(bi, j), 0))
        const = lambda a: pl.BlockSpec(a.shape, lambda bi, j: (0, 0))
        stk = pl.BlockSpec((None, tm, width), lambda bi, j: (slot, bi, 0))
        pad = pl.BlockSpec((None, tm, width), lambda bi, j: (bi, j, 0))
        f32s = jax.ShapeDtypeStruct((n_slots, batch * tm, width), F32)
        bf16p = jax.ShapeDtypeStruct((batch, (tiles + 1) * tm, width), BF16)
        out_specs, out_shape = [tok(width), stk, pad, stk, pad], [bf16o, f32s, bf16p, f32s, bf16p]
    else:
        grid = (n // tm,)
        tok = lambda wd: pl.BlockSpec((tm, wd), lambda i: (i, 0))
        const = lambda a: pl.BlockSpec(a.shape, lambda i: (0, 0))
        stk = pl.BlockSpec((None, tm, width), lambda i: (slot, i, 0))
        f32s = jax.ShapeDtypeStruct((n_slots, n, width), F32)
        out_specs, out_shape = [tok(width), stk, tok(width), stk, tok(width)], [bf16o, f32s, bf16o, f32s, bf16o]
    prev_args, prev_specs, aliases = _stack_alias(prev, 3, (1, 3))
    return pl.pallas_call(
        functools.partial(_pre_odd_kernel, padded=padded),
        grid=grid,
        in_specs=[tok(d), const(g), const(w)] + prev_specs,
        out_specs=out_specs,
        out_shape=out_shape,
        input_output_aliases=aliases,
        compiler_params=_params(len(grid)),
    )(h, g, w, *prev_args)


def _cast_kernel(x_ref, o_ref):
    o_ref[...] = x_ref[...].astype(o_ref.dtype)


def _to_bf16(w):
    layers, rows, cols = w.shape
    tr = rows // 4
    assert rows % 4 == 0 and tr % 16 == 0
    spec = pl.BlockSpec((None, tr, cols), lambda i, j: (i, j, 0))
    return pl.pallas_call(
        _cast_kernel,
        grid=(layers, rows // tr),
        in_specs=[spec],
        out_specs=spec,
        out_shape=jax.ShapeDtypeStruct(w.shape, BF16),
        compiler_params=_params(2),
    )(w)


def _post_kernel(*refs, n_o, ffn_chunk, final):
    h_ref = refs[0]
    o_refs = refs[1:1 + n_o]
    (p_ref, wout_ref, gffn_ref, wg_ref, wu_ref, wd_ref, gple_ref, wpg_ref, wp_ref) = refs[1 + n_o:10 + n_o]
    gfin_ref = refs[10 + n_o] if final else None
    out_ref = refs[-1]

    h = h_ref[...]
    off = 0
    for o_ref in o_refs:
        wd = o_ref.shape[1]
        h = h + _dot(o_ref[...], wout_ref[off:off + wd, :])
        off += wd

    hn = _rms(h, gffn_ref[...]).astype(BF16)
    hidden = wg_ref.shape[1]
    for c in range(hidden // ffn_chunk):
        cols = slice(c * ffn_chunk, (c + 1) * ffn_chunk)
        gt = _dot(hn, wg_ref[:, cols])
        up = _dot(hn, wu_ref[:, cols])
        act = (gt * _sigmoid(gt) * up).astype(BF16)
        h = h + _dot(act, wd_ref[cols, :])

    gate = _sigmoid(_dot(_rms(h, gple_ref[...]).astype(BF16), wpg_ref[...]))
    h = h + _dot(p_ref[...].astype(BF16), wp_ref[...]) * gate
    if final:
        h = _rms(h, gfin_ref[...])
    out_ref[...] = h


def _post(h, o_parts, p, wout, gffn, wg, wu, wd, gple, wpg, wp, gfin, tm):
    n, d = h.shape
    final = gfin is not None
    hidden = wg.shape[1]
    ffn_chunk = hidden
    tok = lambda a: pl.BlockSpec((tm, a.shape[1]), lambda i: (i, 0))
    const = lambda a: pl.BlockSpec(a.shape, lambda i: (0, 0), pipeline_mode=pl.Buffered(1))
    consts = [wout, gffn, wg, wu, wd, gple, wpg, wp] + ([gfin] if final else [])
    return pl.pallas_call(
        functools.partial(_post_kernel, n_o=len(o_parts), ffn_chunk=ffn_chunk, final=final),
        grid=(n // tm,),
        in_specs=[tok(h)] + [tok(o) for o in o_parts] + [tok(p)] + [const(c) for c in consts],
        out_specs=tok(h),
        out_shape=jax.ShapeDtypeStruct((n, d), F32),
        compiler_params=_params(1),
    )(h, *o_parts, p, *consts)


def _sb_kernel(q_ref, k_ref, v_ref, tt_ref, o_ref, acc_ref, run_ref, *, tq, q_off, n_diag):
    qi = pl.program_id(2)
    q2 = q_ref[...]
    lane = lax.broadcasted_iota(jnp.int32, (tq, LANES), 1)
    col = lax.broadcasted_iota(jnp.int32, (tq, SB_KEY_BLOCK), 1)
    row_pos = q_off + qi * tq + lax.broadcasted_iota(jnp.int32, (tq, SB_KEY_BLOCK), 0)
    first = lane < HEAD_DIM
    blk0 = (q_off + qi * tq) // SB_KEY_BLOCK
    tt = tt_ref[...]
    zero = jnp.zeros_like(q2)
    heads = (0, 1)
    q_heads = (jnp.where(first, q2, zero), jnp.where(first, zero, q2))
    acc_ref[...] = jnp.zeros(acc_ref.shape, F32)
    run_ref[...] = jnp.zeros(run_ref.shape, F32)

    def block(kb, masked):
        start = pl.multiple_of(kb * SB_KEY_BLOCK, SB_KEY_BLOCK)
        kblk = k_ref[pl.ds(start, SB_KEY_BLOCK), :]
        vblk = v_ref[pl.ds(start, SB_KEY_BLOCK), :]
        s = [_dot_nt(q_heads[h], kblk) for h in heads]
        soft = [jnp.log(1.0 + jnp.exp2(-jnp.abs(s[h]))) * LOG2E for h in heads]
        log_beta = [jnp.minimum(s[h], 0.0) - soft[h] for h in heads]
        log_keep = [log_beta[h] - s[h] for h in heads]
        if masked:
            mask = (kb * SB_KEY_BLOCK + col) < row_pos
            log_keep = [jnp.where(mask, log_keep[h], 0.0) for h in heads]
        hi = [log_keep[h].astype(BF16) for h in heads]
        lo = [(log_keep[h] - hi[h].astype(F32)).astype(BF16) for h in heads]
        old = slice(0, LANES)
        new = slice(LANES, SB_KEY_BLOCK)
        cs_new = [_dot(jnp.concatenate([hi[h][:, new], lo[h][:, new]], axis=1), tt) for h in heads]
        cs_old = [_dot(jnp.concatenate([hi[h][:, old], lo[h][:, old]], axis=1), tt) for h in heads]
        for h in heads:
            run = run_ref[h]
            run_old = run + cs_new[h][:, LANES:]
            w = jnp.concatenate([jnp.exp2(log_beta[h][:, old] + cs_old[h][:, :LANES] + run_old),
                                 jnp.exp2(log_beta[h][:, new] + cs_new[h][:, :LANES] + run)], axis=1)
            if masked:
                w = jnp.where(mask, w, 0.0)
            acc_ref[h] += _dot(w.astype(BF16), vblk)
            run_ref[h] = run_old + cs_old[h][:, LANES:]

    for j in reversed(range(n_diag)):
        block(blk0 + j, True)

    def cond(carry):
        i, live = carry
        return jnp.logical_and(i < blk0, live > SB_EXIT_LOG2)

    def live_mass():
        return jnp.max(jnp.maximum(run_ref[0], run_ref[1]))

    def body(carry):
        i, _ = carry
        block(blk0 - 1 - i, False)
        return i + 1, live_mass()

    lax.while_loop(cond, body, (jnp.int32(0), live_mass()))
    o_ref[...] = jnp.where(first, acc_ref[0], acc_ref[1]).astype(o_ref.dtype)


def _suffix_matrix():
    r = lax.broadcasted_iota(jnp.int32, (2 * LANES, 2 * LANES), 0) % LANES
    c = lax.broadcasted_iota(jnp.int32, (2 * LANES, 2 * LANES), 1)
    return jnp.where((c >= LANES) | (r > c), 1.0, 0.0).astype(BF16)


def _sb_attn(q, k, v, tq, q_off):
    b, t, w = q.shape
    keys = k.shape[1]
    n_diag = -(-tq // SB_KEY_BLOCK)
    assert (q_off % SB_KEY_BLOCK == 0) and (tq % SB_KEY_BLOCK == 0 or t == tq)
    assert q_off + (t - tq) + n_diag * SB_KEY_BLOCK <= keys
    tt = _suffix_matrix()
    qspec = pl.BlockSpec((None, tq, LANES), lambda bi, hp, qi: (bi, qi, hp))
    kspec = pl.BlockSpec((None, keys, LANES), lambda bi, hp, qi: (bi, 0, hp))
    return pl.pallas_call(
        functools.partial(_sb_kernel, tq=tq, q_off=q_off, n_diag=n_diag),
        grid=(b, w // LANES, t // tq),
        in_specs=[qspec, kspec, kspec, pl.BlockSpec(tt.shape, lambda bi, hp, qi: (0, 0))],
        out_specs=qspec,
        out_shape=jax.ShapeDtypeStruct((b, t, w), BF16),
        scratch_shapes=[pltpu.VMEM((2, tq, LANES), F32), pltpu.VMEM((2, tq, LANES), F32)],
        compiler_params=_params(3),
    )(q, k, v, tt)


def _diff_kernel(lam_ref, gain_ref, q_ref, k_ref, v_ref, o_ref, m_ref, l_ref, acc_ref,
                 *, tq, tk, q_off, kv_len, n_diag, lam_init):
    qi = pl.program_id(2)
    q2 = q_ref[...]
    lane = lax.broadcasted_iota(jnp.int32, (tq, LANES), 1)
    first = lane < HEAD_DIM
    zero = jnp.zeros_like(q2)
    q_maps = (jnp.where(first, q2, zero), jnp.where(first, zero, q2))
    q_chunk = (q_off + qi * tq + lax.broadcasted_iota(jnp.int32, (tq, tk), 0)) // CHUNK
    col = lax.broadcasted_iota(jnp.int32, (tq, tk), 1)
    blk0 = (q_off + qi * tq) // tk
    ones = jnp.ones((tk, LANES), BF16)

    m_ref[...] = jnp.full(m_ref.shape, NEG, F32)
    l_ref[...] = jnp.zeros(l_ref.shape, F32)
    acc_ref[...] = jnp.zeros(acc_ref.shape, F32)

    def block(kb, masked):
        start = pl.multiple_of(kb * tk, tk)
        kblk = k_ref[pl.ds(start, tk), :]
        vext = jnp.concatenate([v_ref[pl.ds(start, tk), :], ones], axis=1)
        if masked:
            kpos = kb * tk + col
            mask = (kpos // CHUNK) <= q_chunk
            if kv_len % tk:
                mask = mask & (kpos < kv_len)
        maps = (0, 1)
        slabs = range(tk // LANES)
        s = [_dot_nt(q_maps[mp], kblk) for mp in maps]
        if masked:
            s = [jnp.where(mask, s[mp], NEG) for mp in maps]
        m_old = [m_ref[mp] for mp in maps]
        m_new = []
        for mp in maps:
            mx = s[mp][:, :LANES]
            for j in slabs[1:]:
                mx = jnp.maximum(mx, s[mp][:, j * LANES:(j + 1) * LANES])
            m_new.append(jnp.maximum(m_old[mp], jnp.max(mx, axis=-1, keepdims=True)))
        alpha = [jnp.exp2(m_old[mp] - m_new[mp]) for mp in maps]
        p = [jnp.concatenate([jnp.exp2(s[mp][:, j * LANES:(j + 1) * LANES] - m_new[mp]).astype(BF16)
                              for j in slabs], axis=1) for mp in maps]
        pv = [_dot(p[mp], vext) for mp in maps]
        for mp in maps:
            acc_ref[mp] = acc_ref[mp] * alpha[mp] + pv[mp][:, :LANES]
            l_ref[mp] = l_ref[mp] * alpha[mp] + pv[mp][:, LANES:]
            m_ref[mp] = m_new[mp]

    for j in reversed(range(n_diag)):
        block(blk0 + j, True)

    def body(i, carry):
        block(2 * i, False)
        block(2 * i + 1, False)
        return carry

    lax.fori_loop(0, blk0 // 2, body, 0)

    @pl.when(blk0 % 2 == 1)
    def _():
        block(blk0 - 1, False)

    lp = lam_ref[...]
    lam = (jnp.exp(jnp.sum(lp[0:1] * lp[1:2], axis=-1, keepdims=True))
           - jnp.exp(jnp.sum(lp[2:3] * lp[3:4], axis=-1, keepdims=True)) + lam_init)
    o = acc_ref[0] / l_ref[0] - lam * (acc_ref[1] / l_ref[1])
    o = o * lax.rsqrt(jnp.mean(o * o, axis=-1, keepdims=True) + EPS)
    o_ref[...] = (o * gain_ref[...] * (1.0 - lam_init)).astype(o_ref.dtype)


def _diff_attn(q, k, v, lam_params, gain, tq, tk, q_off, kv_len, lam_init):
    b, t, w = q.shape
    keys = k.shape[1]
    n_diag = max(1, tq // tk)
    assert q_off % tk == 0 and (tq % tk == 0 or t == tq) and keys % tk == 0 and CHUNK % 8 == 0
    assert (q_off + t + tk - 1) // tk * tk <= keys
    gain3 = gain.reshape(gain.shape[0], 1, gain.shape[1])
    qspec = pl.BlockSpec((None, tq, LANES), lambda bi, hd, qi: (bi, qi, hd))
    kspec = pl.BlockSpec((None, keys, LANES), lambda bi, hd, qi: (bi, 0, hd))
    return pl.pallas_call(
        functools.partial(_diff_kernel, tq=tq, tk=tk, q_off=q_off, kv_len=kv_len, n_diag=n_diag,
                          lam_init=lam_init),
        grid=(b, w // LANES, t // tq),
        in_specs=[pl.BlockSpec(lam_params.shape, lambda bi, hd, qi: (0, 0)),
                  pl.BlockSpec((None, 1, LANES), lambda bi, hd, qi: (hd, 0, 0)),
                  qspec, kspec, kspec],
        out_specs=qspec,
        out_shape=jax.ShapeDtypeStruct((b, t, w), BF16),
        scratch_shapes=[pltpu.VMEM((2, tq, LANES), F32)] * 3,
        compiler_params=_params(3),
    )(lam_params, gain3, q, k, v)


def _band_kernel(q_ref, k_ref, v_ref, bias_ref, o_ref, *, rows, n_sub, valid_lo, valid_hi):
    qi = pl.program_id(2)
    lane = lax.broadcasted_iota(jnp.int32, (rows, LANES), 1)
    first = lane < HEAD_DIM
    ones = jnp.ones((BAND_WINDOW, LANES), BF16)
    kcol = lax.broadcasted_iota(jnp.int32, (1, BAND_WINDOW), 1)

    def sub(j, carry):
        r0 = pl.multiple_of(j * rows, rows)
        ws = pl.multiple_of((qi * n_sub + j) * rows, rows)
        q2 = q_ref[pl.ds(r0, rows), :]
        zero = jnp.zeros_like(q2)
        kw = k_ref[pl.ds(ws, BAND_WINDOW), :]
        vext = jnp.concatenate([v_ref[pl.ds(ws, BAND_WINDOW), :], ones], axis=1)
        krow = ws + kcol
        penalty = jnp.where((krow >= valid_lo) & (krow < valid_hi), 0.0, NEG)
        outs = []
        for head in range(2):
            qh = jnp.where(first if head == 0 else jnp.logical_not(first), q2, zero)
            s = _dot_nt(qh, kw) + bias_ref[head] + penalty
            p = jnp.exp2(s - jnp.max(s, axis=-1, keepdims=True))
            pv = _dot(p.astype(BF16), vext)
            outs.append(pv[:, :LANES] / pv[:, LANES:])
        o_ref[pl.ds(r0, rows), :] = jnp.where(first, outs[0], outs[1]).astype(o_ref.dtype)
        return carry

    lax.fori_loop(0, n_sub, sub, 0, unroll=BAND_UNROLL if n_sub % BAND_UNROLL == 0 else 1)


def _band_bias_table(rel_bias, rows):
    heads = rel_bias.shape[0]
    span = BAND_WINDOW + rows - 1
    n_far = BAND_PAST - REL_CLIP + rows
    rb = rel_bias.astype(F32) * LOG2E
    near = rb[:, 2 * REL_CLIP - 1:0:-1]
    assert n_far + near.shape[1] == span
    e = jnp.concatenate([jnp.broadcast_to(rb[:, -1:], (heads, n_far)), near,
                         jnp.zeros((heads, 1), F32)], axis=1)
    skew = jnp.broadcast_to(e[:, None, :], (heads, rows, span + 1)).reshape(heads, rows * (span + 1))
    skew = skew[:, :rows * span].reshape(heads, rows, span)
    bias = skew[:, :, rows - 1:]
    i = jnp.arange(rows, dtype=jnp.int32)[:, None]
    rel_key = jnp.arange(BAND_WINDOW, dtype=jnp.int32)[None, :] - BAND_PAST
    kc = jnp.floor_divide(rel_key, CHUNK)
    qc = i // CHUNK
    in_band = (kc <= qc) & (kc >= qc - BAND_PAST // CHUNK)
    return jnp.where(in_band[None], bias, NEG)


def _band_attn(q, k, v, bias_tab, rows, n_sub, valid_lo, valid_hi):
    b, t, w = q.shape
    keys = k.shape[1]
    tq = rows * n_sub
    assert t % tq == 0 and (t - rows) + BAND_WINDOW <= keys
    qspec = pl.BlockSpec((None, tq, LANES), lambda bi, hp, qi: (bi, qi, hp))
    kspec = pl.BlockSpec((None, keys, LANES), lambda bi, hp, qi: (bi, 0, hp))
    return pl.pallas_call(
        functools.partial(_band_kernel, rows=rows, n_sub=n_sub, valid_lo=valid_lo, valid_hi=valid_hi),
        grid=(b, w // LANES, t // tq),
        in_specs=[qspec, kspec, kspec,
                  pl.BlockSpec((2, rows, BAND_WINDOW), lambda bi, hp, qi: (hp, 0, 0))],
        out_specs=qspec,
        out_shape=jax.ShapeDtypeStruct((b, t, w), BF16),
        compiler_params=_params(3),
    )(q, k, v, bias_tab)


def _rope_tables(pos, reps):
    half = HEAD_DIM // 2
    inv = ROPE_THETA ** (-jnp.arange(half, dtype=F32) / half)
    ang = pos.astype(F32)[:, None] * inv[None, :]
    cos, sin = jnp.cos(ang), jnp.sin(ang)
    cos = jnp.tile(cos, (reps, LANES // half))
    sin_signed = jnp.tile(jnp.concatenate([-sin, sin], axis=1), (reps, LANES // HEAD_DIM))
    return cos, sin_signed


def _with_past(past, new, total):
    b, t, w = new.shape
    pad = jnp.zeros((b, total - past.shape[1] - t, w), BF16)
    return jnp.concatenate([past.astype(BF16), new, pad], axis=1)


def _trunk(x, p, past, weights, cfg):
    (norm_mix, w_in_even, w_out_even, diff_lambda, diff_norm, w_in_odd, w_out_odd, rel_bias,
     norm_ffn, w_gate, w_up, w_down, norm_ple, w_ple_gate, w_ple, norm_final) = weights
    b, t, d = x.shape
    n = b * t
    depth = norm_mix.shape[0]
    tm = cfg["tm"]
    q_off = 0 if past is None else past[0].shape[2]
    pos = q_off + jnp.arange(t, dtype=jnp.int32)
    if past is None:
        cos, sin_signed = _rope_tables(pos, 1)
    else:
        cos, sin_signed = _rope_tables(pos, tm // t)
    row = lambda a: a.reshape(1, -1)
    seq = lambda a: a.reshape(b, t, a.shape[-1])

    h = x.reshape(n, d)
    n_even, n_odd = w_in_even.shape[0], w_in_odd.shape[0]
    even_state = None
    odd_state = None
    for li in range(depth):
        if li % 2 == 0:
            e = li // 2
            qa, ka, kab, va, vab, qb, kb, kbb, vb, vbb = _pre_even(
                h, row(norm_mix[li]), w_in_even[e], cos, sin_signed, tm, e, n_even, even_state)
            even_state = (ka, va, kb, vb)
            lam_init = 0.8 - 0.6 * math.exp(-0.3 * li)
            if past is None:
                keys = [seq(a) for a in (kab, vab, kbb, vbb)]
                kv_len = t
            else:
                total = cfg["keys_total"]
                flat = lambda c: c.reshape(c.shape[0], c.shape[1], -1)
                keys = [_with_past(flat(c[e]), seq(a), total)
                        for c, a in zip(past[:4], (kab, vab, kbb, vbb))]
                kv_len = q_off + t
            o_a = _sb_attn(seq(qa), keys[0], keys[1], cfg["sb_tq"], q_off)
            o_b = _diff_attn(seq(qb), keys[2], keys[3], diff_lambda[e], diff_norm[e],
                             cfg["diff_tq"], cfg["diff_tk"], q_off, kv_len, lam_init)
            o_parts = [o_a.reshape(n, -1), o_b.reshape(n, -1)]
            w_out = w_out_even[e]
        else:
            od = li // 2
            rows = cfg["band_rows"]
            bias_tab = _band_bias_table(rel_bias[od], rows)
            if past is None:
                assert t >= BAND_PAST
                q, k, kk, v, vv = _pre_odd(h, row(norm_mix[li]), w_in_odd[od], tm, od, n_odd,
                                           odd_state, batch=b)
                valid_lo, valid_hi = BAND_PAST, BAND_PAST + t
            else:
                q, k, kbf, v, vbf = _pre_odd(h, row(norm_mix[li]), w_in_odd[od], tm, od, n_odd,
                                             odd_state)
                ck, cv = past[4][od], past[5][od]
                cache_rows = ck.shape[1]
                assert cache_rows == BAND_PAST
                flat = lambda c: c.reshape(c.shape[0], c.shape[1], -1)
                kk = _with_past(flat(ck), seq(kbf), BAND_WINDOW)
                vv = _with_past(flat(cv), seq(vbf), BAND_WINDOW)
                valid_lo, valid_hi = 0, cache_rows + t
            odd_state = (k, v)
            o = _band_attn(seq(q), kk, vv, bias_tab, rows, cfg["band_sub"], valid_lo, valid_hi)
            o_parts = [o.reshape(n, -1)]
            w_out = w_out_odd[od]
        gfin = row(norm_final) if li == depth - 1 else None
        h = _post(h, o_parts, p[li].reshape(n, -1), w_out, row(norm_ffn[li]), w_gate[li], w_up[li],
                  w_down[li], row(norm_ple[li]), w_ple_gate[li], w_ple[li], gfin, tm)
    y = h.reshape(b, t, d)
    heads = lambda a, shp: a.reshape((a.shape[0], b, -1) + shp)
    sb_k, sb_v, df_k, df_v = even_state
    bd_k, bd_v = odd_state
    state = (heads(sb_k, (H_SB, HEAD_DIM)), heads(sb_v, (H_SB, HEAD_DIM)),
             heads(df_k, (H_DIFF, 2, HEAD_DIM)), heads(df_v, (H_DIFF, 2 * HEAD_DIM)),
             heads(bd_k, (H_BAND, HEAD_DIM)), heads(bd_v, (H_BAND, HEAD_DIM)))
    return y, state


def kernel(x_prompt, x_sample, cache_sb_k, cache_sb_v, cache_diff_k, cache_diff_v, cache_band_k, cache_band_v, p_prompt, p_sample, norm_mix, w_in_even, w_out_even, diff_lambda, diff_norm, w_in_odd, w_out_odd, rel_bias, norm_ffn, w_gate, w_up, w_down, norm_ple, w_ple_gate, w_ple, norm_final):
    bf = _to_bf16
    weights = (norm_mix, bf(w_in_even), bf(w_out_even), diff_lambda, diff_norm, bf(w_in_odd),
               bf(w_out_odd), rel_bias, norm_ffn, bf(w_gate), bf(w_up), bf(w_down), norm_ple,
               bf(w_ple_gate), bf(w_ple), norm_final)
    t_p = x_prompt.shape[1]
    t_s = x_sample.shape[1]
    past_len = cache_sb_k.shape[2]
    diff_tk = 512
    cfg_p = dict(tm=512, sb_tq=256, diff_tq=512, diff_tk=diff_tk, band_rows=LANES,
                 band_sub=min(8, t_p // LANES))
    keys_total = -(-(past_len + t_s) // diff_tk) * diff_tk
    cfg_s = dict(tm=min(512, x_sample.shape[0] * t_s), sb_tq=t_s, diff_tq=t_s, diff_tk=diff_tk, band_rows=t_s, band_sub=1,
                 keys_total=keys_total)
    y_p, st_p = _trunk(x_prompt, p_prompt, None, weights, cfg_p)
    past = (cache_sb_k, cache_sb_v, cache_diff_k, cache_diff_v, cache_band_k, cache_band_v)
    y_s, st_s = _trunk(x_sample, p_sample, past, weights, cfg_s)
    return (y_p, y_s) + tuple(st_p) + tuple(st_s)
```

```python
import functools
import math

import jax
import jax.numpy as jnp
from jax import lax
from jax.experimental import pallas as pl
from jax.experimental.pallas import tpu as pltpu

CHUNK = 64
HEAD_DIM = 64
H_SB = 8
H_DIFF = 4
H_BAND = 16
BAND_PAST = 8 * CHUNK
REL_CLIP = 128
ROPE_THETA = 10000.0
EPS = 1e-6
NEG = -1e30
SCALE = HEAD_DIM ** -0.5
LOG2E = 1.4426950408889634
QSCALE = SCALE * LOG2E

LANES = 128
VMEM_LIMIT = 56 * 1024 * 1024

SB_KEY_BLOCK = 2 * LANES
SB_EXIT_LOG2 = -160.0
BAND_WINDOW = BAND_PAST + LANES
BAND_UNROLL = 8

F32 = jnp.float32
BF16 = jnp.bfloat16


def _rms(x, g):
    return x * lax.rsqrt(jnp.mean(x * x, axis=-1, keepdims=True) + EPS) * g


def _sigmoid(x):
    return 1.0 / (1.0 + jnp.exp(-x))


def _dot(a, b):
    return jnp.dot(a, b, preferred_element_type=F32)


def _dot_nt(a, b):
    return lax.dot_general(a, b, (((1,), (1,)), ((), ())), preferred_element_type=F32)


def _params(n_axes):
    return pltpu.CompilerParams(dimension_semantics=("arbitrary",) * n_axes,
                                vmem_limit_bytes=VMEM_LIMIT)


def _rope(x, cos, sin_signed, first_half):
    outs = []
    for j in range(x.shape[1] // LANES):
        xj = x[:, j * LANES:(j + 1) * LANES]
        partner = jnp.where(first_half, pltpu.roll(xj, LANES - HEAD_DIM // 2, 1),
                            pltpu.roll(xj, HEAD_DIM // 2, 1))
        outs.append(xj * cos + partner * sin_signed)
    return jnp.concatenate(outs, axis=1)


def _pre_even_kernel(h_ref, g_ref, w_ref, cos_ref, sin_ref, *rest):
    (qa_ref, ka_ref, kab_ref, va_ref, vab_ref, qb_ref, kb_ref, kbb_ref, vb_ref, vbb_ref) = rest[-10:]
    hn = _rms(h_ref[...], g_ref[...]).astype(BF16)
    width = qa_ref.shape[1]

    def proj(c):
        return _dot(hn, w_ref[:, c * width:(c + 1) * width])

    cos = cos_ref[...]
    sin_signed = sin_ref[...]
    lane = lax.broadcasted_iota(jnp.int32, cos.shape, 1)
    first_half = (lane % HEAD_DIM) < (HEAD_DIM // 2)

    qa_ref[...] = (proj(0) * QSCALE).astype(BF16)
    ka = proj(1)
    ka_ref[...] = ka
    kab_ref[...] = ka.astype(BF16)
    va = proj(2)
    va_ref[...] = va
    vab_ref[...] = va.astype(BF16)
    qb_ref[...] = (_rope(proj(3), cos, sin_signed, first_half) * QSCALE).astype(BF16)
    kb = _rope(proj(4), cos, sin_signed, first_half)
    kb_ref[...] = kb
    kbb_ref[...] = kb.astype(BF16)
    vb = proj(5)
    vb_ref[...] = vb
    vbb_ref[...] = vb.astype(BF16)


def _pre_odd_kernel(h_ref, g_ref, w_ref, *rest, padded):
    q_ref, k_ref, kb_ref, v_ref, vb_ref = rest[-5:]
    width = q_ref.shape[1]

    def project(keep_f32):
        hn = _rms(h_ref[...], g_ref[...]).astype(BF16)
        q_ref[...] = (_dot(hn, w_ref[:, :width]) * QSCALE).astype(BF16)
        k = _dot(hn, w_ref[:, width:2 * width])
        kb_ref[...] = k.astype(BF16)
        v = _dot(hn, w_ref[:, 2 * width:])
        vb_ref[...] = v.astype(BF16)
        keep_f32(k, v)

    def store_f32(k, v):
        k_ref[...] = k
        v_ref[...] = v

    if not padded:
        project(store_f32)
        return

    j = pl.program_id(1)

    @pl.when(j == 0)
    def _():
        kb_ref[...] = jnp.zeros(kb_ref.shape, BF16)
        vb_ref[...] = jnp.zeros(vb_ref.shape, BF16)

    @pl.when(j > 0)
    def _():
        project(lambda k, v: pl.when(j == pl.num_programs(1) - 1)(lambda: store_f32(k, v)))


def _stack_alias(prev, n_fixed_inputs, out_positions):
    if prev is None:
        return [], [], {}
    specs = [pl.BlockSpec(memory_space=pl.ANY)] * len(prev)
    aliases = {n_fixed_inputs + i: o for i, o in enumerate(out_positions)}
    return list(prev), specs, aliases


def _pre_even(h, g, w, cos, sin_signed, tm, slot, n_slots, prev):
    n, d = h.shape
    width = w.shape[1] // 6
    n_pos_blocks = cos.shape[0] // tm
    tok = lambda wd: pl.BlockSpec((tm, wd), lambda i: (i, 0))
    const = lambda a: pl.BlockSpec(a.shape, lambda i: (0, 0))
    pos = pl.BlockSpec((tm, LANES), lambda i: (i % n_pos_blocks, 0))
    stk = pl.BlockSpec((None, tm, width), lambda i: (slot, i, 0))
    f32s = jax.ShapeDtypeStruct((n_slots, n, width), F32)
    bf16o = jax.ShapeDtypeStruct((n, width), BF16)
    t, s = tok(width), stk
    prev_args, prev_specs, aliases = _stack_alias(prev, 5, (1, 3, 6, 8))
    return pl.pallas_call(
        _pre_even_kernel,
        grid=(n // tm,),
        in_specs=[tok(d), const(g), const(w), pos, pos] + prev_specs,
        out_specs=[t, s, t, s, t, t, s, t, s, t],
        out_shape=[bf16o, f32s, bf16o, f32s, bf16o, bf16o, f32s, bf16o, f32s, bf16o],
        input_output_aliases=aliases,
        compiler_params=_params(1),
    )(h, g, w, cos, sin_signed, *prev_args)


def _pre_odd(h, g, w, tm, slot, n_slots, prev, batch=None):
    n, d = h.shape
    width = w.shape[1] // 3
    padded = batch is not None
    bf16o = jax.ShapeDtypeStruct((n, width), BF16)
    if padded:
        assert tm == BAND_PAST
        tiles = n // batch // tm
        grid = (batch, tiles + 1)
        tile = lambda bi, j: bi * tiles + jnp.maximum(j - 1, 0)
        tok = lambda wd: pl.BlockSpec((tm, wd), lambda bi, j: (tile(bi, j), 0))
        const = lambda a: pl.BlockSpec(a.shape, lambda bi, j: (0, 0))
        stk = pl.BlockSpec((None, tm, width), lambda bi, j: (slot, bi, 0))
        pad = pl.BlockSpec((None, tm, width), lambda bi, j: (bi, j, 0))
        f32s = jax.ShapeDtypeStruct((n_slots, batch * tm, width), F32)
        bf16p = jax.ShapeDtypeStruct((batch, (tiles + 1) * tm, width), BF16)
        out_specs, out_shape = [tok(width), stk, pad, stk, pad], [bf16o, f32s, bf16p, f32s, bf16p]
    else:
        grid = (n // tm,)
        tok = lambda wd: pl.BlockSpec((tm, wd), lambda i: (i, 0))
        const = lambda a: pl.BlockSpec(a.shape, lambda i: (0, 0))
        stk = pl.BlockSpec((None, tm, width), lambda i: (slot, i, 0))
        f32s = jax.ShapeDtypeStruct((n_slots, n, width), F32)
        out_specs, out_shape = [tok(width), stk, tok(width), stk, tok(width)], [bf16o, f32s, bf16o, f32s, bf16o]
    prev_args, prev_specs, aliases = _stack_alias(prev, 3, (1, 3))
    return pl.pallas_call(
        functools.partial(_pre_odd_kernel, padded=padded),
        grid=grid,
        in_specs=[tok(d), const(g), const(w)] + prev_specs,
        out_specs=out_specs,
        out_shape=out_shape,
        input_output_aliases=aliases,
        compiler_params=_params(len(grid)),
    )(h, g, w, *prev_args)


def _cast_kernel(x_ref, o_ref):
    o_ref[...] = x_ref[...].astype(o_ref.dtype)


def _to_bf16(w):
    layers, rows, cols = w.shape
    tr = rows // 4
    assert rows % 4 == 0 and tr % 16 == 0
    spec = pl.BlockSpec((None, tr, cols), lambda i, j: (i, j, 0))
    return pl.pallas_call(
        _cast_kernel,
        grid=(layers, rows // tr),
        in_specs=[spec],
        out_specs=spec,
        out_shape=jax.ShapeDtypeStruct(w.shape, BF16),
        compiler_params=_params(2),
    )(w)


def _post_kernel(*refs, n_o, ffn_chunk, final):
    h_ref = refs[0]
    o_refs = refs[1:1 + n_o]
    (p_ref, wout_ref, gffn_ref, wg_ref, wu_ref, wd_ref, gple_ref, wpg_ref, wp_ref) = refs[1 + n_o:10 + n_o]
    gfin_ref = refs[10 + n_o] if final else None
    out_ref = refs[-1]

    h = h_ref[...]
    off = 0
    for o_ref in o_refs:
        wd = o_ref.shape[1]
        h = h + _dot(o_ref[...], wout_ref[off:off + wd, :])
        off += wd

    hn = _rms(h, gffn_ref[...]).astype(BF16)
    hidden = wg_ref.shape[1]
    for c in range(hidden // ffn_chunk):
        cols = slice(c * ffn_chunk, (c + 1) * ffn_chunk)
        gt = _dot(hn, wg_ref[:, cols])
        up = _dot(hn, wu_ref[:, cols])
        act = (gt * _sigmoid(gt) * up).astype(BF16)
        h = h + _dot(act, wd_ref[cols, :])

    gate = _sigmoid(_dot(_rms(h, gple_ref[...]).astype(BF16), wpg_ref[...]))
    h = h + _dot(p_ref[...].astype(BF16), wp_ref[...]) * gate
    if final:
        h = _rms(h, gfin_ref[...])
    out_ref[...] = h


def _post(h, o_parts, p, wout, gffn, wg, wu, wd, gple, wpg, wp, gfin, tm):
    n, d = h.shape
    final = gfin is not None
    hidden = wg.shape[1]
    ffn_chunk = hidden
    tok = lambda a: pl.BlockSpec((tm, a.shape[1]), lambda i: (i, 0))
    const = lambda a: pl.BlockSpec(a.shape, lambda i: (0, 0), pipeline_mode=pl.Buffered(1))
    consts = [wout, gffn, wg, wu, wd, gple, wpg, wp] + ([gfin] if final else [])
    return pl.pallas_call(
        functools.partial(_post_kernel, n_o=len(o_parts), ffn_chunk=ffn_chunk, final=final),
        grid=(n // tm,),
        in_specs=[tok(h)] + [tok(o) for o in o_parts] + [tok(p)] + [const(c) for c in consts],
        out_specs=tok(h),
        out_shape=jax.ShapeDtypeStruct((n, d), F32),
        compiler_params=_params(1),
    )(h, *o_parts, p, *consts)


def _sb_kernel(q_ref, k_ref, v_ref, tt_ref, o_ref, acc_ref, run_ref, *, tq, q_off, n_diag):
    qi = pl.program_id(2)
    q2 = q_ref[...]
    lane = lax.broadcasted_iota(jnp.int32, (tq, LANES), 1)
    col = lax.broadcasted_iota(jnp.int32, (tq, SB_KEY_BLOCK), 1)
    row_pos = q_off + qi * tq + lax.broadcasted_iota(jnp.int32, (tq, SB_KEY_BLOCK), 0)
    first = lane < HEAD_DIM
    blk0 = (q_off + qi * tq) // SB_KEY_BLOCK
    tt = tt_ref[...]
    zero = jnp.zeros_like(q2)
    heads = (0, 1)
    q_heads = (jnp.where(first, q2, zero), jnp.where(first, zero, q2))
    acc_ref[...] = jnp.zeros(acc_ref.shape, F32)
    run_ref[...] = jnp.zeros(run_ref.shape, F32)

    def block(kb, masked):
        start = pl.multiple_of(kb * SB_KEY_BLOCK, SB_KEY_BLOCK)
        kblk = k_ref[pl.ds(start, SB_KEY_BLOCK), :]
        vblk = v_ref[pl.ds(start, SB_KEY_BLOCK), :]
        s = [_dot_nt(q_heads[h], kblk) for h in heads]
        soft = [jnp.log(1.0 + jnp.exp2(-jnp.abs(s[h]))) * LOG2E for h in heads]
        log_beta = [jnp.minimum(s[h], 0.0) - soft[h] for h in heads]
        log_keep = [log_beta[h] - s[h] for h in heads]
        if masked:
            mask = (kb * SB_KEY_BLOCK + col) < row_pos
            log_keep = [jnp.where(mask, log_keep[h], 0.0) for h in heads]
        hi = [log_keep[h].astype(BF16) for h in heads]
        lo = [(log_keep[h] - hi[h].astype(F32)).astype(BF16) for h in heads]
        old = slice(0, LANES)
        new = slice(LANES, SB_KEY_BLOCK)
        cs_new = [_dot(jnp.concatenate([hi[h][:, new], lo[h][:, new]], axis=1), tt) for h in heads]
        cs_old = [_dot(jnp.concatenate([hi[h][:, old], lo[h][:, old]], axis=1), tt) for h in heads]
        for h in heads:
            run = run_ref[h]
            run_old = run + cs_new[h][:, LANES:]
            w = jnp.concatenate([jnp.exp2(log_beta[h][:, old] + cs_old[h][:, :LANES] + run_old),
                                 jnp.exp2(log_beta[h][:, new] + cs_new[h][:, :LANES] + run)], axis=1)
            if masked:
                w = jnp.where(mask, w, 0.0)
            acc_ref[h] += _dot(w.astype(BF16), vblk)
            run_ref[h] = run_old + cs_old[h][:, LANES:]

    for j in reversed(range(n_diag)):
        block(blk0 + j, True)

    def cond(carry):
        i, live = carry
        return jnp.logical_and(i < blk0, live > SB_EXIT_LOG2)

    def live_mass():
        return jnp.max(jnp.maximum(run_ref[0], run_ref[1]))

    def body(carry):
        i, _ = carry
        block(blk0 - 1 - i, False)
        return i + 1, live_mass()

    lax.while_loop(cond, body, (jnp.int32(0), live_mass()))
    o_ref[...] = jnp.where(first, acc_ref[0], acc_ref[1]).astype(o_ref.dtype)


def _suffix_matrix():
    r = lax.broadcasted_iota(jnp.int32, (2 * LANES, 2 * LANES), 0) % LANES
    c = lax.broadcasted_iota(jnp.int32, (2 * LANES, 2 * LANES), 1)
    return jnp.where((c >= LANES) | (r > c), 1.0, 0.0).astype(BF16)


def _keys_from_past(kernel_fn, n_before):
    def wrapped(*refs):
        past_new = refs[n_before:n_before + 4]
        k_buf, v_buf = refs[-2:]
        for past, new, buf in ((past_new[0], past_new[2], k_buf), (past_new[1], past_new[3], v_buf)):
            p, t = past.shape[0], new.shape[0]
            buf[0:p, :] = past[...]
            buf[p:p + t, :] = new[...]
            buf[p + t:, :] = jnp.zeros((buf.shape[0] - p - t, buf.shape[1]), buf.dtype)
        kernel_fn(*refs[:n_before], k_buf, v_buf, *refs[n_before + 4:-2])
    return wrapped


def _key_operands(kernel_fn, n_before, k, v, past, total):
    index = lambda bi, hp, qi: (bi, 0, hp)
    spec = lambda a: pl.BlockSpec((None, a.shape[1], LANES), index)
    if past is None:
        return kernel_fn, k.shape[1], [k, v], [spec(k), spec(v)], []
    buf = pltpu.VMEM((total, LANES), BF16)
    operands = [past[0], past[1], k, v]
    return _keys_from_past(kernel_fn, n_before), total, operands, [spec(a) for a in operands], [buf, buf]


def _sb_attn(q, k, v, tq, q_off, past=None, total=None):
    b, t, w = q.shape
    n_diag = -(-tq // SB_KEY_BLOCK)
    body, keys, key_args, key_specs, key_scratch = _key_operands(
        functools.partial(_sb_kernel, tq=tq, q_off=q_off, n_diag=n_diag), 1, k, v, past, total)
    assert (q_off % SB_KEY_BLOCK == 0) and (tq % SB_KEY_BLOCK == 0 or t == tq)
    assert q_off + (t - tq) + n_diag * SB_KEY_BLOCK <= keys
    tt = _suffix_matrix()
    qspec = pl.BlockSpec((None, tq, LANES), lambda bi, hp, qi: (bi, qi, hp))
    return pl.pallas_call(
        body,
        grid=(b, w // LANES, t // tq),
        in_specs=[qspec] + key_specs + [pl.BlockSpec(tt.shape, lambda bi, hp, qi: (0, 0))],
        out_specs=qspec,
        out_shape=jax.ShapeDtypeStruct((b, t, w), BF16),
        scratch_shapes=[pltpu.VMEM((2, tq, LANES), F32), pltpu.VMEM((2, tq, LANES), F32)] + key_scratch,
        compiler_params=_params(3),
    )(q, *key_args, tt)


def _diff_kernel(lam_ref, gain_ref, q_ref, k_ref, v_ref, o_ref, m_ref, l_ref, acc_ref,
                 *, tq, tk, q_off, kv_len, n_diag, lam_init):
    qi = pl.program_id(2)
    q2 = q_ref[...]
    lane = lax.broadcasted_iota(jnp.int32, (tq, LANES), 1)
    first = lane < HEAD_DIM
    zero = jnp.zeros_like(q2)
    q_maps = (jnp.where(first, q2, zero), jnp.where(first, zero, q2))
    q_chunk = (q_off + qi * tq + lax.broadcasted_iota(jnp.int32, (tq, tk), 0)) // CHUNK
    col = lax.broadcasted_iota(jnp.int32, (tq, tk), 1)
    blk0 = (q_off + qi * tq) // tk
    ones = jnp.ones((tk, LANES), BF16)

    m_ref[...] = jnp.full(m_ref.shape, NEG, F32)
    l_ref[...] = jnp.zeros(l_ref.shape, F32)
    acc_ref[...] = jnp.zeros(acc_ref.shape, F32)

    def block(kb, masked):
        start = pl.multiple_of(kb * tk, tk)
        kblk = k_ref[pl.ds(start, tk), :]
        vext = jnp.concatenate([v_ref[pl.ds(start, tk), :], ones], axis=1)
        if masked:
            kpos = kb * tk + col
            mask = (kpos // CHUNK) <= q_chunk
            if kv_len % tk:
                mask = mask & (kpos < kv_len)
        maps = (0, 1)
        slabs = range(tk // LANES)
        s = [_dot_nt(q_maps[mp], kblk) for mp in maps]
        if masked:
            s = [jnp.where(mask, s[mp], NEG) for mp in maps]
        m_old = [m_ref[mp] for mp in maps]
        m_new = []
        for mp in maps:
            mx = s[mp][:, :LANES]
            for j in slabs[1:]:
                mx = jnp.maximum(mx, s[mp][:, j * LANES:(j + 1) * LANES])
            m_new.append(jnp.maximum(m_old[mp], jnp.max(mx, axis=-1, keepdims=True)))
        alpha = [jnp.exp2(m_old[mp] - m_new[mp]) for mp in maps]
        p = [jnp.concatenate([jnp.exp2(s[mp][:, j * LANES:(j + 1) * LANES] - m_new[mp]).astype(BF16)
                              for j in slabs], axis=1) for mp in maps]
        pv = [_dot(p[mp], vext) for mp in maps]
        for mp in maps:
            acc_ref[mp] = acc_ref[mp] * alpha[mp] + pv[mp][:, :LANES]
            l_ref[mp] = l_ref[mp] * alpha[mp] + pv[mp][:, LANES:]
            m_ref[mp] = m_new[mp]

    for j in reversed(range(n_diag)):
        block(blk0 + j, True)

    def body(i, carry):
        block(2 * i, False)
        block(2 * i + 1, False)
        return carry

    lax.fori_loop(0, blk0 // 2, body, 0)

    @pl.when(blk0 % 2 == 1)
    def _():
        block(blk0 - 1, False)

    lp = lam_ref[...]
    lam = (jnp.exp(jnp.sum(lp[0:1] * lp[1:2], axis=-1, keepdims=True))
           - jnp.exp(jnp.sum(lp[2:3] * lp[3:4], axis=-1, keepdims=True)) + lam_init)
    o = acc_ref[0] / l_ref[0] - lam * (acc_ref[1] / l_ref[1])
    o = o * lax.rsqrt(jnp.mean(o * o, axis=-1, keepdims=True) + EPS)
    o_ref[...] = (o * gain_ref[...] * (1.0 - lam_init)).astype(o_ref.dtype)


def _diff_attn(q, k, v, lam_params, gain, tq, tk, q_off, kv_len, lam_init, past=None, total=None):
    b, t, w = q.shape
    n_diag = max(1, tq // tk)
    body, keys, key_args, key_specs, key_scratch = _key_operands(
        functools.partial(_diff_kernel, tq=tq, tk=tk, q_off=q_off, kv_len=kv_len, n_diag=n_diag,
                          lam_init=lam_init), 3, k, v, past, total)
    assert q_off % tk == 0 and (tq % tk == 0 or t == tq) and keys % tk == 0 and CHUNK % 8 == 0
    assert (q_off + t + tk - 1) // tk * tk <= keys
    gain3 = gain.reshape(gain.shape[0], 1, gain.shape[1])
    qspec = pl.BlockSpec((None, tq, LANES), lambda bi, hd, qi: (bi, qi, hd))
    return pl.pallas_call(
        body,
        grid=(b, w // LANES, t // tq),
        in_specs=[pl.BlockSpec(lam_params.shape, lambda bi, hd, qi: (0, 0)),
                  pl.BlockSpec((None, 1, LANES), lambda bi, hd, qi: (hd, 0, 0)),
                  qspec] + key_specs,
        out_specs=qspec,
        out_shape=jax.ShapeDtypeStruct((b, t, w), BF16),
        scratch_shapes=[pltpu.VMEM((2, tq, LANES), F32)] * 3 + key_scratch,
        compiler_params=_params(3),
    )(lam_params, gain3, q, *key_args)


def _band_kernel(q_ref, k_ref, v_ref, bias_ref, o_ref, *, rows, n_sub, valid_lo, valid_hi):
    qi = pl.program_id(2)
    lane = lax.broadcasted_iota(jnp.int32, (rows, LANES), 1)
    first = lane < HEAD_DIM
    ones = jnp.ones((BAND_WINDOW, LANES), BF16)
    kcol = lax.broadcasted_iota(jnp.int32, (1, BAND_WINDOW), 1)

    def sub(j, carry):
        r0 = pl.multiple_of(j * rows, rows)
        ws = pl.multiple_of((qi * n_sub + j) * rows, rows)
        q2 = q_ref[pl.ds(r0, rows), :]
        zero = jnp.zeros_like(q2)
        kw = k_ref[pl.ds(ws, BAND_WINDOW), :]
        vext = jnp.concatenate([v_ref[pl.ds(ws, BAND_WINDOW), :], ones], axis=1)
        krow = ws + kcol
        penalty = jnp.where((krow >= valid_lo) & (krow < valid_hi), 0.0, NEG)
        outs = []
        for head in range(2):
            qh = jnp.where(first if head == 0 else jnp.logical_not(first), q2, zero)
            s = _dot_nt(qh, kw) + bias_ref[head] + penalty
            p = jnp.exp2(s - jnp.max(s, axis=-1, keepdims=True))
            pv = _dot(p.astype(BF16), vext)
            outs.append(pv[:, :LANES] / pv[:, LANES:])
        o_ref[pl.ds(r0, rows), :] = jnp.where(first, outs[0], outs[1]).astype(o_ref.dtype)
        return carry

    lax.fori_loop(0, n_sub, sub, 0, unroll=BAND_UNROLL if n_sub % BAND_UNROLL == 0 else 1)


def _band_bias_table(rel_bias, rows):
    heads = rel_bias.shape[0]
    span = BAND_WINDOW + rows - 1
    n_far = BAND_PAST - REL_CLIP + rows
    rb = rel_bias.astype(F32) * LOG2E
    near = rb[:, 2 * REL_CLIP - 1:0:-1]
    assert n_far + near.shape[1] == span
    e = jnp.concatenate([jnp.broadcast_to(rb[:, -1:], (heads, n_far)), near,
                         jnp.zeros((heads, 1), F32)], axis=1)
    skew = jnp.broadcast_to(e[:, None, :], (heads, rows, span + 1)).reshape(heads, rows * (span + 1))
    skew = skew[:, :rows * span].reshape(heads, rows, span)
    bias = skew[:, :, rows - 1:]
    i = jnp.arange(rows, dtype=jnp.int32)[:, None]
    rel_key = jnp.arange(BAND_WINDOW, dtype=jnp.int32)[None, :] - BAND_PAST
    kc = jnp.floor_divide(rel_key, CHUNK)
    qc = i // CHUNK
    in_band = (kc <= qc) & (kc >= qc - BAND_PAST // CHUNK)
    return jnp.where(in_band[None], bias, NEG)


def _band_attn(q, k, v, bias_tab, rows, n_sub, valid_lo, valid_hi, past=None, total=None):
    b, t, w = q.shape
    tq = rows * n_sub
    body, keys, key_args, key_specs, key_scratch = _key_operands(
        functools.partial(_band_kernel, rows=rows, n_sub=n_sub, valid_lo=valid_lo, valid_hi=valid_hi),
        1, k, v, past, total)
    assert t % tq == 0 and (t - rows) + BAND_WINDOW <= keys
    qspec = pl.BlockSpec((None, tq, LANES), lambda bi, hp, qi: (bi, qi, hp))
    return pl.pallas_call(
        body,
        grid=(b, w // LANES, t // tq),
        in_specs=[qspec] + key_specs +
                 [pl.BlockSpec((2, rows, BAND_WINDOW), lambda bi, hp, qi: (hp, 0, 0))],
        out_specs=qspec,
        out_shape=jax.ShapeDtypeStruct((b, t, w), BF16),
        scratch_shapes=key_scratch,
        compiler_params=_params(3),
    )(q, *key_args, bias_tab)


def _rope_tables(pos, reps):
    half = HEAD_DIM // 2
    inv = ROPE_THETA ** (-jnp.arange(half, dtype=F32) / half)
    ang = pos.astype(F32)[:, None] * inv[None, :]
    cos, sin = jnp.cos(ang), jnp.sin(ang)
    cos = jnp.tile(cos, (reps, LANES // half))
    sin_signed = jnp.tile(jnp.concatenate([-sin, sin], axis=1), (reps, LANES // HEAD_DIM))
    return cos, sin_signed


def _cache_rows(cache):
    return cache.reshape(cache.shape[0], cache.shape[1], -1).astype(BF16)


def _trunk(x, p, past, weights, cfg):
    (norm_mix, w_in_even, w_out_even, diff_lambda, diff_norm, w_in_odd, w_out_odd, rel_bias,
     norm_ffn, w_gate, w_up, w_down, norm_ple, w_ple_gate, w_ple, norm_final) = weights
    b, t, d = x.shape
    n = b * t
    depth = norm_mix.shape[0]
    tm = cfg["tm"]
    q_off = 0 if past is None else past[0].shape[2]
    pos = q_off + jnp.arange(t, dtype=jnp.int32)
    if past is None:
        cos, sin_signed = _rope_tables(pos, 1)
    else:
        cos, sin_signed = _rope_tables(pos, tm // t)
    row = lambda a: a.reshape(1, -1)
    seq = lambda a: a.reshape(b, t, a.shape[-1])

    h = x.reshape(n, d)
    n_even, n_odd = w_in_even.shape[0], w_in_odd.shape[0]
    even_state = None
    odd_state = None
    for li in range(depth):
        if li % 2 == 0:
            e = li // 2
            qa, ka, kab, va, vab, qb, kb, kbb, vb, vbb = _pre_even(
                h, row(norm_mix[li]), w_in_even[e], cos, sin_signed, tm, e, n_even, even_state)
            even_state = (ka, va, kb, vb)
            lam_init = 0.8 - 0.6 * math.exp(-0.3 * li)
            keys = [seq(a) for a in (kab, vab, kbb, vbb)]
            if past is None:
                sb_past = df_past = total = None
                kv_len = t
            else:
                total = cfg["keys_total"]
                sb_past = (_cache_rows(past[0][e]), _cache_rows(past[1][e]))
                df_past = (_cache_rows(past[2][e]), _cache_rows(past[3][e]))
                kv_len = q_off + t
            o_a = _sb_attn(seq(qa), keys[0], keys[1], cfg["sb_tq"], q_off, sb_past, total)
            o_b = _diff_attn(seq(qb), keys[2], keys[3], diff_lambda[e], diff_norm[e],
                             cfg["diff_tq"], cfg["diff_tk"], q_off, kv_len, lam_init, df_past, total)
            o_parts = [o_a.reshape(n, -1), o_b.reshape(n, -1)]
            w_out = w_out_even[e]
        else:
            od = li // 2
            rows = cfg["band_rows"]
            bias_tab = _band_bias_table(rel_bias[od], rows)
            if past is None:
                assert t >= BAND_PAST
                q, k, kk, v, vv = _pre_odd(h, row(norm_mix[li]), w_in_odd[od], tm, od, n_odd,
                                           odd_state, batch=b)
                valid_lo, valid_hi = BAND_PAST, BAND_PAST + t
                bd_past = total = None
            else:
                q, k, kbf, v, vbf = _pre_odd(h, row(norm_mix[li]), w_in_odd[od], tm, od, n_odd,
                                             odd_state)
                kk, vv = seq(kbf), seq(vbf)
                bd_past = (_cache_rows(past[4][od]), _cache_rows(past[5][od]))
                cache_rows = bd_past[0].shape[1]
                assert cache_rows == BAND_PAST
                valid_lo, valid_hi = 0, cache_rows + t
                total = BAND_WINDOW
            odd_state = (k, v)
            o = _band_attn(seq(q), kk, vv, bias_tab, rows, cfg["band_sub"], valid_lo, valid_hi,
                           bd_past, total)
            o_parts = [o.reshape(n, -1)]
            w_out = w_out_odd[od]
        gfin = row(norm_final) if li == depth - 1 else None
        h = _post(h, o_parts, p[li].reshape(n, -1), w_out, row(norm_ffn[li]), w_gate[li], w_up[li],
                  w_down[li], row(norm_ple[li]), w_ple_gate[li], w_ple[li], gfin, tm)
    y = h.reshape(b, t, d)
    heads = lambda a, shp: a.reshape((a.shape[0], b, -1) + shp)
    sb_k, sb_v, df_k, df_v = even_state
    bd_k, bd_v = odd_state
    state = (heads(sb_k, (H_SB, HEAD_DIM)), heads(sb_v, (H_SB, HEAD_DIM)),
             heads(df_k, (H_DIFF, 2, HEAD_DIM)), heads(df_v, (H_DIFF, 2 * HEAD_DIM)),
             heads(bd_k, (H_BAND, HEAD_DIM)), heads(bd_v, (H_BAND, HEAD_DIM)))
    return y, state


def kernel(x_prompt, x_sample, cache_sb_k, cache_sb_v, cache_diff_k, cache_diff_v, cache_band_k, cache_band_v, p_prompt, p_sample, norm_mix, w_in_even, w_out_even, diff_lambda, diff_norm, w_in_odd, w_out_odd, rel_bias, norm_ffn, w_gate, w_up, w_down, norm_ple, w_ple_gate, w_ple, norm_final):
    bf = _to_bf16
    weights = (norm_mix, bf(w_in_even), bf(w_out_even), diff_lambda, diff_norm, bf(w_in_odd),
               bf(w_out_odd), rel_bias, norm_ffn, bf(w_gate), bf(w_up), bf(w_down), norm_ple,
               bf(w_ple_gate), bf(w_ple), norm_final)
    t_p = x_prompt.shape[1]
    t_s = x_sample.shape[1]
    past_len = cache_sb_k.shape[2]
    diff_tk = 512
    cfg_p = dict(tm=512, sb_tq=256, diff_tq=min(1024, t_p), diff_tk=diff_tk, band_rows=LANES,
                 band_sub=min(8, t_p // LANES))
    keys_total = -(-(past_len + t_s) // diff_tk) * diff_tk
    cfg_s = dict(tm=min(512, x_sample.shape[0] * t_s), sb_tq=t_s, diff_tq=t_s, diff_tk=diff_tk, band_rows=t_s, band_sub=1,
                 keys_total=keys_total)
    y_p, st_p = _trunk(x_prompt, p_prompt, None, weights, cfg_p)
    past = (cache_sb_k, cache_sb_v, cache_diff_k, cache_diff_v, cache_band_k, cache_band_v)
    y_s, st_s = _trunk(x_sample, p_sample, past, weights, cfg_s)
    return (y_p, y_s) + tuple(st_p) + tuple(st_s)
```

```python
import functools
import math

import jax
import jax.numpy as jnp
from jax import lax
from jax.experimental import pallas as pl
from jax.experimental.pallas import tpu as pltpu

CHUNK = 64
HEAD_DIM = 64
H_SB = 8
H_DIFF = 4
H_BAND = 16
BAND_PAST = 8 * CHUNK
REL_CLIP = 128
ROPE_THETA = 10000.0
EPS = 1e-6
NEG = -1e30
SCALE = HEAD_DIM ** -0.5
LOG2E = 1.4426950408889634
QSCALE = SCALE * LOG2E

LANES = 128
VMEM_LIMIT = 56 * 1024 * 1024

SB_KEY_BLOCK = 2 * LANES
SB_EXIT_LOG2 = -160.0
BAND_WINDOW = BAND_PAST + LANES
BAND_UNROLL = 8

F32 = jnp.float32
BF16 = jnp.bfloat16


def _rms(x, g):
    return x * lax.rsqrt(jnp.mean(x * x, axis=-1, keepdims=True) + EPS) * g


def _sigmoid(x):
    return 1.0 / (1.0 + jnp.exp(-x))


def _dot(a, b):
    return jnp.dot(a, b, preferred_element_type=F32)


def _dot_nt(a, b):
    return lax.dot_general(a, b, (((1,), (1,)), ((), ())), preferred_element_type=F32)


def _params(n_axes):
    return pltpu.CompilerParams(dimension_semantics=("arbitrary",) * n_axes,
                                vmem_limit_bytes=VMEM_LIMIT)


def _rope(x, cos, sin_signed, first_half):
    outs = []
    for j in range(x.shape[1] // LANES):
        xj = x[:, j * LANES:(j + 1) * LANES]
        partner = jnp.where(first_half, pltpu.roll(xj, LANES - HEAD_DIM // 2, 1),
                            pltpu.roll(xj, HEAD_DIM // 2, 1))
        outs.append(xj * cos + partner * sin_signed)
    return jnp.concatenate(outs, axis=1)


def _pre_even_kernel(h_ref, g_ref, w_ref, cos_ref, sin_ref, *rest, transposed):
    flip = (lambda a: a.T) if transposed else (lambda a: a)
    (qa_ref, ka_ref, kab_ref, va_ref, vab_ref, qb_ref, kb_ref, kbb_ref, vb_ref, vbb_ref) = rest[-10:]
    hn = _rms(h_ref[...], g_ref[...]).astype(BF16)
    width = qa_ref.shape[1]

    def proj(c):
        return _dot(hn, w_ref[:, c * width:(c + 1) * width])

    cos = cos_ref[...]
    sin_signed = sin_ref[...]
    lane = lax.broadcasted_iota(jnp.int32, cos.shape, 1)
    first_half = (lane % HEAD_DIM) < (HEAD_DIM // 2)

    qa_ref[...] = (proj(0) * QSCALE).astype(BF16)
    ka = proj(1)
    ka_ref[...] = flip(ka)
    kab_ref[...] = ka.astype(BF16)
    va = proj(2)
    va_ref[...] = flip(va)
    vab_ref[...] = va.astype(BF16)
    qb_ref[...] = (_rope(proj(3), cos, sin_signed, first_half) * QSCALE).astype(BF16)
    kb = _rope(proj(4), cos, sin_signed, first_half)
    kb_ref[...] = flip(kb)
    kbb_ref[...] = kb.astype(BF16)
    vb = proj(5)
    vb_ref[...] = vb
    vbb_ref[...] = vb.astype(BF16)


def _pre_odd_kernel(h_ref, g_ref, w_ref, *rest, padded):
    q_ref, k_ref, kb_ref, v_ref, vb_ref = rest[-5:]
    width = q_ref.shape[1]

    def project(keep_f32):
        hn = _rms(h_ref[...], g_ref[...]).astype(BF16)
        q_ref[...] = (_dot(hn, w_ref[:, :width]) * QSCALE).astype(BF16)
        k = _dot(hn, w_ref[:, width:2 * width])
        kb_ref[...] = k.astype(BF16)
        v = _dot(hn, w_ref[:, 2 * width:])
        vb_ref[...] = v.astype(BF16)
        keep_f32(k, v)

    def store_f32(k, v):
        k_ref[...] = k.T if padded else k
        v_ref[...] = v.T if padded else v

    if not padded:
        project(store_f32)
        return

    j = pl.program_id(1)

    @pl.when(j == 0)
    def _():
        kb_ref[...] = jnp.zeros(kb_ref.shape, BF16)
        vb_ref[...] = jnp.zeros(vb_ref.shape, BF16)

    @pl.when(j > 0)
    def _():
        project(lambda k, v: pl.when(j == pl.num_programs(1) - 1)(lambda: store_f32(k, v)))


def _stack_alias(prev, n_fixed_inputs, out_positions):
    if prev is None:
        return [], [], {}
    specs = [pl.BlockSpec(memory_space=pl.ANY)] * len(prev)
    aliases = {n_fixed_inputs + i: o for i, o in enumerate(out_positions)}
    return list(prev), specs, aliases


def _pre_even(h, g, w, cos, sin_signed, tm, slot, n_slots, prev, batch=None):
    n, d = h.shape
    width = w.shape[1] // 6
    n_pos_blocks = cos.shape[0] // tm
    tok = lambda wd: pl.BlockSpec((tm, wd), lambda i: (i, 0))
    const = lambda a: pl.BlockSpec(a.shape, lambda i: (0, 0))
    pos = pl.BlockSpec((tm, LANES), lambda i: (i % n_pos_blocks, 0))
    stk = pl.BlockSpec((None, tm, width), lambda i: (slot, i, 0))
    f32s = jax.ShapeDtypeStruct((n_slots, n, width), F32)
    bf16o = jax.ShapeDtypeStruct((n, width), BF16)
    t, s = tok(width), stk
    st, f32t = s, f32s
    if batch is not None:
        tiles = n // batch // tm
        st = pl.BlockSpec((None, None, width, tm), lambda i: (slot, i // tiles, 0, i % tiles))
        f32t = jax.ShapeDtypeStruct((n_slots, batch, width, n // batch), F32)
    prev_args, prev_specs, aliases = _stack_alias(prev, 5, (1, 3, 6, 8))
    return pl.pallas_call(
        functools.partial(_pre_even_kernel, transposed=batch is not None),
        grid=(n // tm,),
        in_specs=[tok(d), const(g), const(w), pos, pos] + prev_specs,
        out_specs=[t, st, t, st, t, t, st, t, s, t],
        out_shape=[bf16o, f32t, bf16o, f32t, bf16o, bf16o, f32t, bf16o, f32s, bf16o],
        input_output_aliases=aliases,
        compiler_params=_params(1),
    )(h, g, w, cos, sin_signed, *prev_args)


def _pre_odd(h, g, w, tm, slot, n_slots, prev, batch=None):
    n, d = h.shape
    width = w.shape[1] // 3
    padded = batch is not None
    bf16o = jax.ShapeDtypeStruct((n, width), BF16)
    if padded:
        assert tm == BAND_PAST
        tiles = n // batch // tm
        grid = (batch, tiles + 1)
        tile = lambda bi, j: bi * tiles + jnp.maximum(j - 1, 0)
        tok = lambda wd: pl.BlockSpec((tm, wd), lambda bi, j: (tile(bi, j), 0))
        const = lambda a: pl.BlockSpec(a.shape, lambda bi, j: (0, 0))
        stk = pl.BlockSpec((None, None, width, tm), lambda bi, j: (slot, bi, 0, 0))
        pad = pl.BlockSpec((None, tm, width), lambda bi, j: (bi, j, 0))
        f32s = jax.ShapeDtypeStruct((n_slots, batch, width, tm), F32)
        bf16p = jax.ShapeDtypeStruct((batch, (tiles + 1) * tm, width), BF16)
        out_specs, out_shape = [tok(width), stk, pad, stk, pad], [bf16o, f32s, bf16p, f32s, bf16p]
    else:
        grid = (n // tm,)
        tok = lambda wd: pl.BlockSpec((tm, wd), lambda i: (i, 0))
        const = lambda a: pl.BlockSpec(a.shape, lambda i: (0, 0))
        stk = pl.BlockSpec((None, tm, width), lambda i: (slot, i, 0))
        f32s = jax.ShapeDtypeStruct((n_slots, n, width), F32)
        out_specs, out_shape = [tok(width), stk, tok(width), stk, tok(width)], [bf16o, f32s, bf16o, f32s, bf16o]
    prev_args, prev_specs, aliases = _stack_alias(prev, 3, (1, 3))
    return pl.pallas_call(
        functools.partial(_pre_odd_kernel, padded=padded),
        grid=grid,
        in_specs=[tok(d), const(g), const(w)] + prev_specs,
        out_specs=out_specs,
        out_shape=out_shape,
        input_output_aliases=aliases,
        compiler_params=_params(len(grid)),
    )(h, g, w, *prev_args)


def _cast_kernel(x_ref, o_ref):
    o_ref[...] = x_ref[...].astype(o_ref.dtype)


def _to_bf16(w):
    layers, rows, cols = w.shape
    tr = rows // 4
    assert rows % 4 == 0 and tr % 16 == 0
    spec = pl.BlockSpec((None, tr, cols), lambda i, j: (i, j, 0))
    return pl.pallas_call(
        _cast_kernel,
        grid=(layers, rows // tr),
        in_specs=[spec],
        out_specs=spec,
        out_shape=jax.ShapeDtypeStruct(w.shape, BF16),
        compiler_params=_params(2),
    )(w)


def _post_kernel(*refs, n_o, ffn_chunk, final):
    h_ref = refs[0]
    o_refs = refs[1:1 + n_o]
    (p_ref, wout_ref, gffn_ref, wg_ref, wu_ref, wd_ref, gple_ref, wpg_ref, wp_ref) = refs[1 + n_o:10 + n_o]
    gfin_ref = refs[10 + n_o] if final else None
    out_ref = refs[-1]

    h = h_ref[...]
    off = 0
    for o_ref in o_refs:
        wd = o_ref.shape[1]
        h = h + _dot(o_ref[...], wout_ref[off:off + wd, :])
        off += wd

    hn = _rms(h, gffn_ref[...]).astype(BF16)
    hidden = wg_ref.shape[1]
    for c in range(hidden // ffn_chunk):
        cols = slice(c * ffn_chunk, (c + 1) * ffn_chunk)
        gt = _dot(hn, wg_ref[:, cols])
        up = _dot(hn, wu_ref[:, cols])
        act = (gt * _sigmoid(gt) * up).astype(BF16)
        h = h + _dot(act, wd_ref[cols, :])

    gate = _sigmoid(_dot(_rms(h, gple_ref[...]).astype(BF16), wpg_ref[...]))
    h = h + _dot(p_ref[...].astype(BF16), wp_ref[...]) * gate
    if final:
        h = _rms(h, gfin_ref[...])
    out_ref[...] = h


def _post(h, o_parts, p, layer, wout, wout_layer, gffn, wg, wu, wd, gple, wpg, wp, gfin, tm):
    n, d = h.shape
    final = gfin is not None
    hidden = wg.shape[2]
    ffn_chunk = hidden
    tok = lambda a: pl.BlockSpec((tm, a.shape[1]), lambda i: (i, 0))
    row = lambda a: pl.BlockSpec(a.shape, lambda i: (0, 0), pipeline_mode=pl.Buffered(1))
    mat = lambda a, li: pl.BlockSpec((None,) + a.shape[1:], lambda i: (li, 0, 0),
                                     pipeline_mode=pl.Buffered(1))
    consts = [wout, gffn, wg, wu, wd, gple, wpg, wp] + ([gfin] if final else [])
    const_specs = ([mat(wout, wout_layer), row(gffn)] + [mat(a, layer) for a in (wg, wu, wd)]
                   + [row(gple), mat(wpg, layer), mat(wp, layer)] + ([row(gfin)] if final else []))
    p_spec = pl.BlockSpec((None, tm, p.shape[2]), lambda i: (layer, i, 0))
    return pl.pallas_call(
        functools.partial(_post_kernel, n_o=len(o_parts), ffn_chunk=ffn_chunk, final=final),
        grid=(n // tm,),
        in_specs=[tok(h)] + [tok(o) for o in o_parts] + [p_spec] + const_specs,
        out_specs=tok(h),
        out_shape=jax.ShapeDtypeStruct((n, d), F32),
        compiler_params=_params(1),
    )(h, *o_parts, p, *consts)


def _sb_kernel(q_ref, k_ref, v_ref, tt_ref, o_ref, acc_ref, run_ref, *, tq, q_off, n_diag):
    qi = pl.program_id(2)
    q2 = q_ref[...]
    lane = lax.broadcasted_iota(jnp.int32, (tq, LANES), 1)
    col = lax.broadcasted_iota(jnp.int32, (tq, SB_KEY_BLOCK), 1)
    row_pos = q_off + qi * tq + lax.broadcasted_iota(jnp.int32, (tq, SB_KEY_BLOCK), 0)
    first = lane < HEAD_DIM
    blk0 = (q_off + qi * tq) // SB_KEY_BLOCK
    tt = tt_ref[...]
    zero = jnp.zeros_like(q2)
    heads = (0, 1)
    q_heads = (jnp.where(first, q2, zero), jnp.where(first, zero, q2))
    acc_ref[...] = jnp.zeros(acc_ref.shape, F32)
    run_ref[...] = jnp.zeros(run_ref.shape, F32)

    def block(kb, masked):
        start = pl.multiple_of(kb * SB_KEY_BLOCK, SB_KEY_BLOCK)
        kblk = k_ref[pl.ds(start, SB_KEY_BLOCK), :]
        vblk = v_ref[pl.ds(start, SB_KEY_BLOCK), :]
        s = [_dot_nt(q_heads[h], kblk) for h in heads]
        soft = [jnp.log(1.0 + jnp.exp2(-jnp.abs(s[h]))) * LOG2E for h in heads]
        log_beta = [jnp.minimum(s[h], 0.0) - soft[h] for h in heads]
        log_keep = [log_beta[h] - s[h] for h in heads]
        if masked:
            mask = (kb * SB_KEY_BLOCK + col) < row_pos
            log_keep = [jnp.where(mask, log_keep[h], 0.0) for h in heads]
        hi = [log_keep[h].astype(BF16) for h in heads]
        lo = [(log_keep[h] - hi[h].astype(F32)).astype(BF16) for h in heads]
        old = slice(0, LANES)
        new = slice(LANES, SB_KEY_BLOCK)
        cs_new = [_dot(jnp.concatenate([hi[h][:, new], lo[h][:, new]], axis=1), tt) for h in heads]
        cs_old = [_dot(jnp.concatenate([hi[h][:, old], lo[h][:, old]], axis=1), tt) for h in heads]
        for h in heads:
            run = run_ref[h]
            run_old = run + cs_new[h][:, LANES:]
            w = jnp.concatenate([jnp.exp2(log_beta[h][:, old] + cs_old[h][:, :LANES] + run_old),
                                 jnp.exp2(log_beta[h][:, new] + cs_new[h][:, :LANES] + run)], axis=1)
            if masked:
                w = jnp.where(mask, w, 0.0)
            acc_ref[h] += _dot(w.astype(BF16), vblk)
            run_ref[h] = run_old + cs_old[h][:, LANES:]

    for j in reversed(range(n_diag)):
        block(blk0 + j, True)

    def cond(carry):
        i, live = carry
        return jnp.logical_and(i < blk0, live > SB_EXIT_LOG2)

    def live_mass():
        return jnp.max(jnp.maximum(run_ref[0], run_ref[1]))

    def body(carry):
        i, _ = carry
        block(blk0 - 1 - i, False)
        return i + 1, live_mass()

    lax.while_loop(cond, body, (jnp.int32(0), live_mass()))
    o_ref[...] = jnp.where(first, acc_ref[0], acc_ref[1]).astype(o_ref.dtype)


def _suffix_matrix():
    r = lax.broadcasted_iota(jnp.int32, (2 * LANES, 2 * LANES), 0) % LANES
    c = lax.broadcasted_iota(jnp.int32, (2 * LANES, 2 * LANES), 1)
    return jnp.where((c >= LANES) | (r > c), 1.0, 0.0).astype(BF16)


def _keys_from_past(kernel_fn, n_before):
    def wrapped(*refs):
        past_new = refs[n_before:n_before + 4]
        k_buf, v_buf = refs[-2:]
        for past, new, buf in ((past_new[0], past_new[2], k_buf), (past_new[1], past_new[3], v_buf)):
            p, t = past.shape[0], new.shape[0]
            buf[0:p, :] = past[...]
            buf[p:p + t, :] = new[...]
            buf[p + t:, :] = jnp.zeros((buf.shape[0] - p - t, buf.shape[1]), buf.dtype)
        kernel_fn(*refs[:n_before], k_buf, v_buf, *refs[n_before + 4:-2])
    return wrapped


def _key_operands(kernel_fn, n_before, k, v, past, total):
    index = lambda bi, hp, qi: (bi, 0, hp)
    spec = lambda a: pl.BlockSpec((None, a.shape[1], LANES), index)
    if past is None:
        return kernel_fn, k.shape[1], [k, v], [spec(k), spec(v)], []
    buf = pltpu.VMEM((total, LANES), BF16)
    operands = [past[0], past[1], k, v]
    return _keys_from_past(kernel_fn, n_before), total, operands, [spec(a) for a in operands], [buf, buf]


def _sb_attn(q, k, v, tq, q_off, past=None, total=None):
    b, t, w = q.shape
    n_diag = -(-tq // SB_KEY_BLOCK)
    body, keys, key_args, key_specs, key_scratch = _key_operands(
        functools.partial(_sb_kernel, tq=tq, q_off=q_off, n_diag=n_diag), 1, k, v, past, total)
    assert (q_off % SB_KEY_BLOCK == 0) and (tq % SB_KEY_BLOCK == 0 or t == tq)
    assert q_off + (t - tq) + n_diag * SB_KEY_BLOCK <= keys
    tt = _suffix_matrix()
    qspec = pl.BlockSpec((None, tq, LANES), lambda bi, hp, qi: (bi, qi, hp))
    return pl.pallas_call(
        body,
        grid=(b, w // LANES, t // tq),
        in_specs=[qspec] + key_specs + [pl.BlockSpec(tt.shape, lambda bi, hp, qi: (0, 0))],
        out_specs=qspec,
        out_shape=jax.ShapeDtypeStruct((b, t, w), BF16),
        scratch_shapes=[pltpu.VMEM((2, tq, LANES), F32), pltpu.VMEM((2, tq, LANES), F32)] + key_scratch,
        compiler_params=_params(3),
    )(q, *key_args, tt)


def _diff_kernel(lam_ref, gain_ref, q_ref, k_ref, v_ref, o_ref, m_ref, l_ref, acc_ref,
                 *, tq, tk, q_off, kv_len, n_diag, lam_init):
    qi = pl.program_id(2)
    q2 = q_ref[...]
    lane = lax.broadcasted_iota(jnp.int32, (tq, LANES), 1)
    first = lane < HEAD_DIM
    zero = jnp.zeros_like(q2)
    q_maps = (jnp.where(first, q2, zero), jnp.where(first, zero, q2))
    q_chunk = (q_off + qi * tq + lax.broadcasted_iota(jnp.int32, (tq, tk), 0)) // CHUNK
    col = lax.broadcasted_iota(jnp.int32, (tq, tk), 1)
    blk0 = (q_off + qi * tq) // tk
    ones = jnp.ones((tk, LANES), BF16)

    m_ref[...] = jnp.full(m_ref.shape, NEG, F32)
    l_ref[...] = jnp.zeros(l_ref.shape, F32)
    acc_ref[...] = jnp.zeros(acc_ref.shape, F32)

    def block(kb, masked):
        start = pl.multiple_of(kb * tk, tk)
        kblk = k_ref[pl.ds(start, tk), :]
        vext = jnp.concatenate([v_ref[pl.ds(start, tk), :], ones], axis=1)
        if masked:
            kpos = kb * tk + col
            mask = (kpos // CHUNK) <= q_chunk
            if kv_len % tk:
                mask = mask & (kpos < kv_len)
        maps = (0, 1)
        slabs = range(tk // LANES)
        s = [_dot_nt(q_maps[mp], kblk) for mp in maps]
        if masked:
            s = [jnp.where(mask, s[mp], NEG) for mp in maps]
        m_old = [m_ref[mp] for mp in maps]
        m_new = []
        for mp in maps:
            mx = s[mp][:, :LANES]
            for j in slabs[1:]:
                mx = jnp.maximum(mx, s[mp][:, j * LANES:(j + 1) * LANES])
            m_new.append(jnp.maximum(m_old[mp], jnp.max(mx, axis=-1, keepdims=True)))
        alpha = [jnp.exp2(m_old[mp] - m_new[mp]) for mp in maps]
        p = [jnp.concatenate([jnp.exp2(s[mp][:, j * LANES:(j + 1) * LANES] - m_new[mp]).astype(BF16)
                              for j in slabs], axis=1) for mp in maps]
        pv = [_dot(p[mp], vext) for mp in maps]
        for mp in maps:
            acc_ref[mp] = acc_ref[mp] * alpha[mp] + pv[mp][:, :LANES]
            l_ref[mp] = l_ref[mp] * alpha[mp] + pv[mp][:, LANES:]
            m_ref[mp] = m_new[mp]

    for j in reversed(range(n_diag)):
        block(blk0 + j, True)

    def body(i, carry):
        block(2 * i, False)
        block(2 * i + 1, False)
        return carry

    lax.fori_loop(0, blk0 // 2, body, 0)

    @pl.when(blk0 % 2 == 1)
    def _():
        block(blk0 - 1, False)

    lp = lam_ref[...]
    lam = (jnp.exp(jnp.sum(lp[0:1] * lp[1:2], axis=-1, keepdims=True))
           - jnp.exp(jnp.sum(lp[2:3] * lp[3:4], axis=-1, keepdims=True)) + lam_init)
    o = acc_ref[0] / l_ref[0] - lam * (acc_ref[1] / l_ref[1])
    o = o * lax.rsqrt(jnp.mean(o * o, axis=-1, keepdims=True) + EPS)
    o_ref[...] = (o * gain_ref[...] * (1.0 - lam_init)).astype(o_ref.dtype)


def _diff_attn(q, k, v, lam_params, gain, tq, tk, q_off, kv_len, lam_init, past=None, total=None):
    b, t, w = q.shape
    n_diag = max(1, tq // tk)
    body, keys, key_args, key_specs, key_scratch = _key_operands(
        functools.partial(_diff_kernel, tq=tq, tk=tk, q_off=q_off, kv_len=kv_len, n_diag=n_diag,
                          lam_init=lam_init), 3, k, v, past, total)
    assert q_off % tk == 0 and (tq % tk == 0 or t == tq) and keys % tk == 0 and CHUNK % 8 == 0
    assert (q_off + t + tk - 1) // tk * tk <= keys
    gain3 = gain.reshape(gain.shape[0], 1, gain.shape[1])
    qspec = pl.BlockSpec((None, tq, LANES), lambda bi, hd, qi: (bi, qi, hd))
    return pl.pallas_call(
        body,
        grid=(b, w // LANES, t // tq),
        in_specs=[pl.BlockSpec(lam_params.shape, lambda bi, hd, qi: (0, 0)),
                  pl.BlockSpec((None, 1, LANES), lambda bi, hd, qi: (hd, 0, 0)),
                  qspec] + key_specs,
        out_specs=qspec,
        out_shape=jax.ShapeDtypeStruct((b, t, w), BF16),
        scratch_shapes=[pltpu.VMEM((2, tq, LANES), F32)] * 3 + key_scratch,
        compiler_params=_params(3),
    )(lam_params, gain3, q, *key_args)


def _band_kernel(q_ref, k_ref, v_ref, bias_ref, o_ref, *, rows, n_sub, valid_lo, valid_hi):
    qi = pl.program_id(2)
    lane = lax.broadcasted_iota(jnp.int32, (rows, LANES), 1)
    first = lane < HEAD_DIM
    ones = jnp.ones((BAND_WINDOW, LANES), BF16)
    kcol = lax.broadcasted_iota(jnp.int32, (1, BAND_WINDOW), 1)

    def sub(j, carry):
        r0 = pl.multiple_of(j * rows, rows)
        ws = pl.multiple_of((qi * n_sub + j) * rows, rows)
        q2 = q_ref[pl.ds(r0, rows), :]
        zero = jnp.zeros_like(q2)
        kw = k_ref[pl.ds(ws, BAND_WINDOW), :]
        vext = jnp.concatenate([v_ref[pl.ds(ws, BAND_WINDOW), :], ones], axis=1)
        krow = ws + kcol
        penalty = jnp.where((krow >= valid_lo) & (krow < valid_hi), 0.0, NEG)
        outs = []
        for head in range(2):
            qh = jnp.where(first if head == 0 else jnp.logical_not(first), q2, zero)
            s = _dot_nt(qh, kw) + bias_ref[head] + penalty
            p = jnp.exp2(s - jnp.max(s, axis=-1, keepdims=True))
            pv = _dot(p.astype(BF16), vext)
            outs.append(pv[:, :LANES] / pv[:, LANES:])
        o_ref[pl.ds(r0, rows), :] = jnp.where(first, outs[0], outs[1]).astype(o_ref.dtype)
        return carry

    lax.fori_loop(0, n_sub, sub, 0, unroll=BAND_UNROLL if n_sub % BAND_UNROLL == 0 else 1)


def _band_bias_table(rel_bias, rows):
    heads = rel_bias.shape[0]
    span = BAND_WINDOW + rows - 1
    n_far = BAND_PAST - REL_CLIP + rows
    rb = rel_bias.astype(F32) * LOG2E
    near = rb[:, 2 * REL_CLIP - 1:0:-1]
    assert n_far + near.shape[1] == span
    e = jnp.concatenate([jnp.broadcast_to(rb[:, -1:], (heads, n_far)), near,
                         jnp.zeros((heads, 1), F32)], axis=1)
    skew = jnp.broadcast_to(e[:, None, :], (heads, rows, span + 1)).reshape(heads, rows * (span + 1))
    skew = skew[:, :rows * span].reshape(heads, rows, span)
    bias = skew[:, :, rows - 1:]
    i = jnp.arange(rows, dtype=jnp.int32)[:, None]
    rel_key = jnp.arange(BAND_WINDOW, dtype=jnp.int32)[None, :] - BAND_PAST
    kc = jnp.floor_divide(rel_key, CHUNK)
    qc = i // CHUNK
    in_band = (kc <= qc) & (kc >= qc - BAND_PAST // CHUNK)
    return jnp.where(in_band[None], bias, NEG)


def _band_attn(q, k, v, bias_tab, rows, n_sub, valid_lo, valid_hi, past=None, total=None):
    b, t, w = q.shape
    tq = rows * n_sub
    body, keys, key_args, key_specs, key_scratch = _key_operands(
        functools.partial(_band_kernel, rows=rows, n_sub=n_sub, valid_lo=valid_lo, valid_hi=valid_hi),
        1, k, v, past, total)
    assert t % tq == 0 and (t - rows) + BAND_WINDOW <= keys
    qspec = pl.BlockSpec((None, tq, LANES), lambda bi, hp, qi: (bi, qi, hp))
    return pl.pallas_call(
        body,
        grid=(b, w // LANES, t // tq),
        in_specs=[qspec] + key_specs +
                 [pl.BlockSpec((2, rows, BAND_WINDOW), lambda bi, hp, qi: (hp, 0, 0))],
        out_specs=qspec,
        out_shape=jax.ShapeDtypeStruct((b, t, w), BF16),
        scratch_shapes=key_scratch,
        compiler_params=_params(3),
    )(q, *key_args, bias_tab)


def _rope_tables(pos, reps):
    half = HEAD_DIM // 2
    inv = ROPE_THETA ** (-jnp.arange(half, dtype=F32) / half)
    ang = pos.astype(F32)[:, None] * inv[None, :]
    cos, sin = jnp.cos(ang), jnp.sin(ang)
    cos = jnp.tile(cos, (reps, LANES // half))
    sin_signed = jnp.tile(jnp.concatenate([-sin, sin], axis=1), (reps, LANES // HEAD_DIM))
    return cos, sin_signed


def _cache_rows_kernel(x_ref, o_ref):
    o_ref[...] = x_ref[...].T.astype(o_ref.dtype)


def _cache_rows(cache, layer):
    n_layers, b, p = cache.shape[:3]
    if cache.shape[-1] % LANES == 0:
        return cache[layer].reshape(b, p, -1).astype(BF16)
    perm = (0, 1) + tuple(range(3, cache.ndim)) + (2,)
    cols = jnp.transpose(cache, perm).reshape(n_layers, b, -1, p)
    w = cols.shape[2]
    return pl.pallas_call(
        _cache_rows_kernel,
        grid=(b,),
        in_specs=[pl.BlockSpec((None, None, w, p), lambda i: (layer, i, 0, 0))],
        out_specs=pl.BlockSpec((None, p, w), lambda i: (i, 0, 0)),
        out_shape=jax.ShapeDtypeStruct((b, p, w), BF16),
        compiler_params=_params(1),
    )(cols)


def _trunk(x, p, past, weights, cfg):
    (norm_mix, w_in_even, w_out_even, diff_lambda, diff_norm, w_in_odd, w_out_odd, rel_bias,
     norm_ffn, w_gate, w_up, w_down, norm_ple, w_ple_gate, w_ple, norm_final) = weights
    b, t, d = x.shape
    n = b * t
    depth = norm_mix.shape[0]
    tm = cfg["tm"]
    q_off = 0 if past is None else past[0].shape[2]
    pos = q_off + jnp.arange(t, dtype=jnp.int32)
    if past is None:
        cos, sin_signed = _rope_tables(pos, 1)
    else:
        cos, sin_signed = _rope_tables(pos, tm // t)
    row = lambda a: a.reshape(1, -1)
    seq = lambda a: a.reshape(b, t, a.shape[-1])

    h = x.reshape(n, d)
    n_even, n_odd = w_in_even.shape[0], w_in_odd.shape[0]
    even_state = None
    odd_state = None
    for li in range(depth):
        if li % 2 == 0:
            e = li // 2
            qa, ka, kab, va, vab, qb, kb, kbb, vb, vbb = _pre_even(
                h, row(norm_mix[li]), w_in_even[e], cos, sin_signed, tm, e, n_even, even_state,
                batch=b if past is None else None)
            even_state = (ka, va, kb, vb)
            lam_init = 0.8 - 0.6 * math.exp(-0.3 * li)
            keys = [seq(a) for a in (kab, vab, kbb, vbb)]
            if past is None:
                sb_past = df_past = total = None
                kv_len = t
            else:
                total = cfg["keys_total"]
                sb_past = (_cache_rows(past[0], e), _cache_rows(past[1], e))
                df_past = (_cache_rows(past[2], e), _cache_rows(past[3], e))
                kv_len = q_off + t
            o_a = _sb_attn(seq(qa), keys[0], keys[1], cfg["sb_tq"], q_off, sb_past, total)
            o_b = _diff_attn(seq(qb), keys[2], keys[3], diff_lambda[e], diff_norm[e],
                             cfg["diff_tq"], cfg["diff_tk"], q_off, kv_len, lam_init, df_past, total)
            o_parts = [o_a.reshape(n, -1), o_b.reshape(n, -1)]
            w_out = w_out_even
        else:
            od = li // 2
            rows = cfg["band_rows"]
            bias_tab = _band_bias_table(rel_bias[od], rows)
            if past is None:
                assert t >= BAND_PAST
                q, k, kk, v, vv = _pre_odd(h, row(norm_mix[li]), w_in_odd[od], tm, od, n_odd,
                                           odd_state, batch=b)
                valid_lo, valid_hi = BAND_PAST, BAND_PAST + t
                bd_past = total = None
            else:
                q, k, kbf, v, vbf = _pre_odd(h, row(norm_mix[li]), w_in_odd[od], tm, od, n_odd,
                                             odd_state)
                kk, vv = seq(kbf), seq(vbf)
                bd_past = (_cache_rows(past[4], od), _cache_rows(past[5], od))
                cache_rows = bd_past[0].shape[1]
                assert cache_rows == BAND_PAST
                valid_lo, valid_hi = 0, cache_rows + t
                total = BAND_WINDOW
            odd_state = (k, v)
            o = _band_attn(seq(q), kk, vv, bias_tab, rows, cfg["band_sub"], valid_lo, valid_hi,
                           bd_past, total)
            o_parts = [o.reshape(n, -1)]
            w_out = w_out_odd
        gfin = row(norm_final) if li == depth - 1 else None
        h = _post(h, o_parts, p.reshape(depth, n, -1), li, w_out, li // 2, row(norm_ffn[li]), w_gate,
                  w_up, w_down, row(norm_ple[li]), w_ple_gate, w_ple, gfin, tm)
    y = h.reshape(b, t, d)
    rows = lambda a, shp: a.reshape((a.shape[0], b, -1) + shp)

    def cols(a, shp):
        a = a.reshape(a.shape[:2] + shp + a.shape[3:])
        return jnp.moveaxis(a, -1, 2)

    heads = cols if past is None else rows
    sb_k, sb_v, df_k, df_v = even_state
    bd_k, bd_v = odd_state
    state = (heads(sb_k, (H_SB, HEAD_DIM)), heads(sb_v, (H_SB, HEAD_DIM)),
             heads(df_k, (H_DIFF, 2, HEAD_DIM)), rows(df_v, (H_DIFF, 2 * HEAD_DIM)),
             heads(bd_k, (H_BAND, HEAD_DIM)), heads(bd_v, (H_BAND, HEAD_DIM)))
    return y, state


def kernel(x_prompt, x_sample, cache_sb_k, cache_sb_v, cache_diff_k, cache_diff_v, cache_band_k, cache_band_v, p_prompt, p_sample, norm_mix, w_in_even, w_out_even, diff_lambda, diff_norm, w_in_odd, w_out_odd, rel_bias, norm_ffn, w_gate, w_up, w_down, norm_ple, w_ple_gate, w_ple, norm_final):
    bf = _to_bf16
    weights = (norm_mix, bf(w_in_even), bf(w_out_even), diff_lambda, diff_norm, bf(w_in_odd),
               bf(w_out_odd), rel_bias, norm_ffn, bf(w_gate), bf(w_up), bf(w_down), norm_ple,
               bf(w_ple_gate), bf(w_ple), norm_final)
    t_p = x_prompt.shape[1]
    t_s = x_sample.shape[1]
    past_len = cache_sb_k.shape[2]
    diff_tk = 512
    cfg_p = dict(tm=512, sb_tq=256, diff_tq=min(1024, t_p), diff_tk=diff_tk, band_rows=LANES,
                 band_sub=min(8, t_p // LANES))
    keys_total = -(-(past_len + t_s) // diff_tk) * diff_tk
    cfg_s = dict(tm=min(512, x_sample.shape[0] * t_s), sb_tq=t_s, diff_tq=t_s, diff_tk=diff_tk, band_rows=t_s, band_sub=1,
                 keys_total=keys_total)
    y_p, st_p = _trunk(x_prompt, p_prompt, None, weights, cfg_p)
    past = (cache_sb_k, cache_sb_v, cache_diff_k, cache_diff_v, cache_band_k, cache_band_v)
    y_s, st_s = _trunk(x_sample, p_sample, past, weights, cfg_s)
    return (y_p, y_s) + tuple(st_p) + tuple(st_s)
```

```python
import functools
import math

import jax
import jax.numpy as jnp
from jax import lax
from jax.experimental import pallas as pl
from jax.experimental.pallas import tpu as pltpu

CHUNK = 64
HEAD_DIM = 64
H_SB = 8
H_DIFF = 4
H_BAND = 16
BAND_PAST = 8 * CHUNK
REL_CLIP = 128
ROPE_THETA = 10000.0
EPS = 1e-6
NEG = -1e30
SCALE = HEAD_DIM ** -0.5
LOG2E = 1.4426950408889634
QSCALE = SCALE * LOG2E

LANES = 128
VMEM_LIMIT = 56 * 1024 * 1024

SB_KEY_BLOCK = 2 * LANES
SB_EXIT_LOG2 = -160.0
BAND_WINDOW = BAND_PAST + LANES
BAND_UNROLL = 8

F32 = jnp.float32
BF16 = jnp.bfloat16


def _rms(x, g):
    return x * lax.rsqrt(jnp.mean(x * x, axis=-1, keepdims=True) + EPS) * g


def _sigmoid(x):
    return 1.0 / (1.0 + jnp.exp(-x))


def _dot(a, b):
    return jnp.dot(a, b, preferred_element_type=F32)


def _dot_nt(a, b):
    return lax.dot_general(a, b, (((1,), (1,)), ((), ())), preferred_element_type=F32)


def _params(n_axes):
    return pltpu.CompilerParams(dimension_semantics=("arbitrary",) * n_axes,
                                vmem_limit_bytes=VMEM_LIMIT)


def _rope(x, cos, sin_signed, first_half):
    outs = []
    for j in range(x.shape[1] // LANES):
        xj = x[:, j * LANES:(j + 1) * LANES]
        partner = jnp.where(first_half, pltpu.roll(xj, LANES - HEAD_DIM // 2, 1),
                            pltpu.roll(xj, HEAD_DIM // 2, 1))
        outs.append(xj * cos + partner * sin_signed)
    return jnp.concatenate(outs, axis=1)


def _pre_even_kernel(h_ref, g_ref, w_ref, cos_ref, sin_ref, *rest, transposed):
    flip = (lambda a: a.T) if transposed else (lambda a: a)
    (qa_ref, ka_ref, kab_ref, va_ref, vab_ref, qb_ref, kb_ref, kbb_ref, vb_ref, vbb_ref) = rest[-10:]
    hn = _rms(h_ref[...], g_ref[...]).astype(BF16)
    width = qa_ref.shape[1]

    def proj(c):
        return _dot(hn, w_ref[:, c * width:(c + 1) * width])

    cos = cos_ref[...]
    sin_signed = sin_ref[...]
    lane = lax.broadcasted_iota(jnp.int32, cos.shape, 1)
    first_half = (lane % HEAD_DIM) < (HEAD_DIM // 2)

    qa_ref[...] = (proj(0) * QSCALE).astype(BF16)
    ka = proj(1)
    ka_ref[...] = flip(ka)
    kab_ref[...] = ka.astype(BF16)
    va = proj(2)
    va_ref[...] = flip(va)
    vab_ref[...] = va.astype(BF16)
    qb_ref[...] = (_rope(proj(3), cos, sin_signed, first_half) * QSCALE).astype(BF16)
    kb = _rope(proj(4), cos, sin_signed, first_half)
    kb_ref[...] = flip(kb)
    kbb_ref[...] = kb.astype(BF16)
    vb = proj(5)
    vb_ref[...] = vb
    vbb_ref[...] = vb.astype(BF16)


def _pre_odd_kernel(h_ref, g_ref, w_ref, *rest, padded):
    q_ref, k_ref, kb_ref, v_ref, vb_ref = rest[-5:]
    width = q_ref.shape[1]

    def project(keep_f32):
        hn = _rms(h_ref[...], g_ref[...]).astype(BF16)
        q_ref[...] = (_dot(hn, w_ref[:, :width]) * QSCALE).astype(BF16)
        k = _dot(hn, w_ref[:, width:2 * width])
        kb_ref[...] = k.astype(BF16)
        v = _dot(hn, w_ref[:, 2 * width:])
        vb_ref[...] = v.astype(BF16)
        keep_f32(k, v)

    def store_f32(k, v):
        k_ref[...] = k.T if padded else k
        v_ref[...] = v.T if padded else v

    if not padded:
        project(store_f32)
        return

    j = pl.program_id(1)

    @pl.when(j == 0)
    def _():
        kb_ref[...] = jnp.zeros(kb_ref.shape, BF16)
        vb_ref[...] = jnp.zeros(vb_ref.shape, BF16)

    @pl.when(j > 0)
    def _():
        project(lambda k, v: pl.when(j == pl.num_programs(1) - 1)(lambda: store_f32(k, v)))


def _stack_alias(prev, n_fixed_inputs, out_positions):
    if prev is None:
        return [], [], {}
    specs = [pl.BlockSpec(memory_space=pl.ANY)] * len(prev)
    aliases = {n_fixed_inputs + i: o for i, o in enumerate(out_positions)}
    return list(prev), specs, aliases


def _pre_even(h, g, w, cos, sin_signed, tm, slot, n_slots, prev, batch=None):
    n, d = h.shape
    width = w.shape[1] // 6
    n_pos_blocks = cos.shape[0] // tm
    tok = lambda wd: pl.BlockSpec((tm, wd), lambda i: (i, 0))
    const = lambda a: pl.BlockSpec(a.shape, lambda i: (0, 0))
    pos = pl.BlockSpec((tm, LANES), lambda i: (i % n_pos_blocks, 0))
    stk = pl.BlockSpec((None, tm, width), lambda i: (slot, i, 0))
    f32s = jax.ShapeDtypeStruct((n_slots, n, width), F32)
    bf16o = jax.ShapeDtypeStruct((n, width), BF16)
    t, s = tok(width), stk
    st, f32t = s, f32s
    if batch is not None:
        tiles = n // batch // tm
        st = pl.BlockSpec((None, None, width, tm), lambda i: (slot, i // tiles, 0, i % tiles))
        f32t = jax.ShapeDtypeStruct((n_slots, batch, width, n // batch), F32)
    prev_args, prev_specs, aliases = _stack_alias(prev, 5, (1, 3, 6, 8))
    return pl.pallas_call(
        functools.partial(_pre_even_kernel, transposed=batch is not None),
        grid=(n // tm,),
        in_specs=[tok(d), const(g), const(w), pos, pos] + prev_specs,
        out_specs=[t, st, t, st, t, t, st, t, s, t],
        out_shape=[bf16o, f32t, bf16o, f32t, bf16o, bf16o, f32t, bf16o, f32s, bf16o],
        input_output_aliases=aliases,
        compiler_params=_params(1),
    )(h, g, w, cos, sin_signed, *prev_args)


def _pre_odd(h, g, w, tm, slot, n_slots, prev, batch=None):
    n, d = h.shape
    width = w.shape[1] // 3
    padded = batch is not None
    bf16o = jax.ShapeDtypeStruct((n, width), BF16)
    if padded:
        assert tm == BAND_PAST
        tiles = n // batch // tm
        grid = (batch, tiles + 1)
        tile = lambda bi, j: bi * tiles + jnp.maximum(j - 1, 0)
        tok = lambda wd: pl.BlockSpec((tm, wd), lambda bi, j: (tile(bi, j), 0))
        const = lambda a: pl.BlockSpec(a.shape, lambda bi, j: (0, 0))
        stk = pl.BlockSpec((None, None, width, tm), lambda bi, j: (slot, bi, 0, 0))
        pad = pl.BlockSpec((None, tm, width), lambda bi, j: (bi, j, 0))
        f32s = jax.ShapeDtypeStruct((n_slots, batch, width, tm), F32)
        bf16p = jax.ShapeDtypeStruct((batch, (tiles + 1) * tm, width), BF16)
        out_specs, out_shape = [tok(width), stk, pad, stk, pad], [bf16o, f32s, bf16p, f32s, bf16p]
    else:
        grid = (n // tm,)
        tok = lambda wd: pl.BlockSpec((tm, wd), lambda i: (i, 0))
        const = lambda a: pl.BlockSpec(a.shape, lambda i: (0, 0))
        stk = pl.BlockSpec((None, tm, width), lambda i: (slot, i, 0))
        f32s = jax.ShapeDtypeStruct((n_slots, n, width), F32)
        out_specs, out_shape = [tok(width), stk, tok(width), stk, tok(width)], [bf16o, f32s, bf16o, f32s, bf16o]
    prev_args, prev_specs, aliases = _stack_alias(prev, 3, (1, 3))
    return pl.pallas_call(
        functools.partial(_pre_odd_kernel, padded=padded),
        grid=grid,
        in_specs=[tok(d), const(g), const(w)] + prev_specs,
        out_specs=out_specs,
        out_shape=out_shape,
        input_output_aliases=aliases,
        compiler_params=_params(len(grid)),
    )(h, g, w, *prev_args)


def _cast_kernel(x_ref, o_ref):
    o_ref[...] = x_ref[...].astype(o_ref.dtype)


def _to_bf16(w):
    layers, rows, cols = w.shape
    tr = rows // 4
    assert rows % 4 == 0 and tr % 16 == 0
    spec = pl.BlockSpec((None, tr, cols), lambda i, j: (i, j, 0))
    return pl.pallas_call(
        _cast_kernel,
        grid=(layers, rows // tr),
        in_specs=[spec],
        out_specs=spec,
        out_shape=jax.ShapeDtypeStruct(w.shape, BF16),
        compiler_params=_params(2),
    )(w)


def _post_kernel(*refs, n_o, ffn_chunk, final):
    h_ref = refs[0]
    o_refs = refs[1:1 + n_o]
    (p_ref, wout_ref, gffn_ref, wg_ref, wu_ref, wd_ref, gple_ref, wpg_ref, wp_ref) = refs[1 + n_o:10 + n_o]
    gfin_ref = refs[10 + n_o] if final else None
    out_ref = refs[-1]

    h = h_ref[...]
    off = 0
    for o_ref in o_refs:
        wd = o_ref.shape[1]
        h = h + _dot(o_ref[...], wout_ref[off:off + wd, :])
        off += wd

    hn = _rms(h, gffn_ref[...]).astype(BF16)
    hidden = wg_ref.shape[1]
    for c in range(hidden // ffn_chunk):
        cols = slice(c * ffn_chunk, (c + 1) * ffn_chunk)
        gt = _dot(hn, wg_ref[:, cols])
        up = _dot(hn, wu_ref[:, cols])
        act = (gt * _sigmoid(gt) * up).astype(BF16)
        h = h + _dot(act, wd_ref[cols, :])

    gate = _sigmoid(_dot(_rms(h, gple_ref[...]).astype(BF16), wpg_ref[...]))
    h = h + _dot(p_ref[...].astype(BF16), wp_ref[...]) * gate
    if final:
        h = _rms(h, gfin_ref[...])
    out_ref[...] = h


def _post(h, o_parts, p, layer, wout, wout_layer, gffn, wg, wu, wd, gple, wpg, wp, gfin, tm):
    n, d = h.shape
    final = gfin is not None
    hidden = wg.shape[2]
    ffn_chunk = hidden
    tok = lambda a: pl.BlockSpec((tm, a.shape[1]), lambda i: (i, 0))
    row = lambda a: pl.BlockSpec(a.shape, lambda i: (0, 0), pipeline_mode=pl.Buffered(1))
    mat = lambda a, li: pl.BlockSpec((None,) + a.shape[1:], lambda i: (li, 0, 0),
                                     pipeline_mode=pl.Buffered(1))
    consts = [wout, gffn, wg, wu, wd, gple, wpg, wp] + ([gfin] if final else [])
    const_specs = ([mat(wout, wout_layer), row(gffn)] + [mat(a, layer) for a in (wg, wu, wd)]
                   + [row(gple), mat(wpg, layer), mat(wp, layer)] + ([row(gfin)] if final else []))
    p_spec = pl.BlockSpec((None, tm, p.shape[2]), lambda i: (layer, i, 0))
    return pl.pallas_call(
        functools.partial(_post_kernel, n_o=len(o_parts), ffn_chunk=ffn_chunk, final=final),
        grid=(n // tm,),
        in_specs=[tok(h)] + [tok(o) for o in o_parts] + [p_spec] + const_specs,
        out_specs=tok(h),
        out_shape=jax.ShapeDtypeStruct((n, d), F32),
        compiler_params=_params(1),
    )(h, *o_parts, p, *consts)


def _lane_tile(i):
    return slice(i * LANES, (i + 1) * LANES)


def _sb_kernel(q_ref, k_ref, v_ref, tt_ref, o_ref, acc_ref, run_ref, *, tq, q_off, n_diag):
    qi = pl.program_id(2)
    n_pairs = q_ref.shape[1] // LANES
    lane = lax.broadcasted_iota(jnp.int32, (tq, LANES), 1)
    col = lax.broadcasted_iota(jnp.int32, (tq, SB_KEY_BLOCK), 1)
    row_pos = q_off + qi * tq + lax.broadcasted_iota(jnp.int32, (tq, SB_KEY_BLOCK), 0)
    first = lane < HEAD_DIM
    blk0 = (q_off + qi * tq) // SB_KEY_BLOCK
    tt = tt_ref[...]
    heads = range(2 * n_pairs)
    q_heads = []
    for pair in range(n_pairs):
        q2 = q_ref[:, _lane_tile(pair)]
        zero = jnp.zeros_like(q2)
        q_heads += [jnp.where(first, q2, zero), jnp.where(first, zero, q2)]
    acc_ref[...] = jnp.zeros(acc_ref.shape, F32)
    run_ref[...] = jnp.zeros(run_ref.shape, F32)

    def block(kb, masked):
        start = pl.multiple_of(kb * SB_KEY_BLOCK, SB_KEY_BLOCK)
        kblk = [k_ref[pl.ds(start, SB_KEY_BLOCK), _lane_tile(pair)] for pair in range(n_pairs)]
        vblk = [v_ref[pl.ds(start, SB_KEY_BLOCK), _lane_tile(pair)] for pair in range(n_pairs)]
        s = [_dot_nt(q_heads[h], kblk[h // 2]) for h in heads]
        soft = [jnp.log(1.0 + jnp.exp2(-jnp.abs(s[h]))) * LOG2E for h in heads]
        log_beta = [jnp.minimum(s[h], 0.0) - soft[h] for h in heads]
        log_keep = [log_beta[h] - s[h] for h in heads]
        if masked:
            mask = (kb * SB_KEY_BLOCK + col) < row_pos
            log_keep = [jnp.where(mask, log_keep[h], 0.0) for h in heads]
        hi = [log_keep[h].astype(BF16) for h in heads]
        lo = [(log_keep[h] - hi[h].astype(F32)).astype(BF16) for h in heads]
        old = slice(0, LANES)
        new = slice(LANES, SB_KEY_BLOCK)
        cs_new = [_dot(jnp.concatenate([hi[h][:, new], lo[h][:, new]], axis=1), tt) for h in heads]
        cs_old = [_dot(jnp.concatenate([hi[h][:, old], lo[h][:, old]], axis=1), tt) for h in heads]
        for h in heads:
            run = run_ref[h]
            run_old = run + cs_new[h][:, LANES:]
            w = jnp.concatenate([jnp.exp2(log_beta[h][:, old] + cs_old[h][:, :LANES] + run_old),
                                 jnp.exp2(log_beta[h][:, new] + cs_new[h][:, :LANES] + run)], axis=1)
            if masked:
                w = jnp.where(mask, w, 0.0)
            acc_ref[h] += _dot(w.astype(BF16), vblk[h // 2])
            run_ref[h] = run_old + cs_old[h][:, LANES:]

    for j in reversed(range(n_diag)):
        block(blk0 + j, True)

    def cond(carry):
        i, live = carry
        return jnp.logical_and(i < blk0, live > SB_EXIT_LOG2)

    def live_mass():
        return jnp.max(functools.reduce(jnp.maximum, [run_ref[h] for h in heads]))

    def body(carry):
        i, _ = carry
        block(blk0 - 1 - i, False)
        return i + 1, live_mass()

    lax.while_loop(cond, body, (jnp.int32(0), live_mass()))
    for pair in range(n_pairs):
        o_ref[:, _lane_tile(pair)] = jnp.where(first, acc_ref[2 * pair],
                                               acc_ref[2 * pair + 1]).astype(o_ref.dtype)


def _suffix_matrix():
    r = lax.broadcasted_iota(jnp.int32, (2 * LANES, 2 * LANES), 0) % LANES
    c = lax.broadcasted_iota(jnp.int32, (2 * LANES, 2 * LANES), 1)
    return jnp.where((c >= LANES) | (r > c), 1.0, 0.0).astype(BF16)


def _keys_from_past(kernel_fn, n_before):
    def wrapped(*refs):
        past_new = refs[n_before:n_before + 4]
        k_buf, v_buf = refs[-2:]
        for past, new, buf in ((past_new[0], past_new[2], k_buf), (past_new[1], past_new[3], v_buf)):
            p, t = past.shape[0], new.shape[0]
            buf[0:p, :] = past[...]
            buf[p:p + t, :] = new[...]
            buf[p + t:, :] = jnp.zeros((buf.shape[0] - p - t, buf.shape[1]), buf.dtype)
        kernel_fn(*refs[:n_before], k_buf, v_buf, *refs[n_before + 4:-2])
    return wrapped


def _key_operands(kernel_fn, n_before, k, v, past, total, tiles=1):
    index = lambda bi, hp, qi: (bi, 0, hp)
    spec = lambda a: pl.BlockSpec((None, a.shape[1], tiles * LANES), index)
    if past is None:
        return kernel_fn, k.shape[1], [k, v], [spec(k), spec(v)], []
    buf = pltpu.VMEM((total, tiles * LANES), BF16)
    operands = [past[0], past[1], k, v]
    return _keys_from_past(kernel_fn, n_before), total, operands, [spec(a) for a in operands], [buf, buf]


def _sb_attn(q, k, v, tq, q_off, past=None, total=None, tiles=1):
    b, t, w = q.shape
    n_diag = -(-tq // SB_KEY_BLOCK)
    body, keys, key_args, key_specs, key_scratch = _key_operands(
        functools.partial(_sb_kernel, tq=tq, q_off=q_off, n_diag=n_diag), 1, k, v, past, total, tiles)
    assert (q_off % SB_KEY_BLOCK == 0) and (tq % SB_KEY_BLOCK == 0 or t == tq)
    assert q_off + (t - tq) + n_diag * SB_KEY_BLOCK <= keys and w % (tiles * LANES) == 0
    tt = _suffix_matrix()
    qspec = pl.BlockSpec((None, tq, tiles * LANES), lambda bi, hp, qi: (bi, qi, hp))
    state = pltpu.VMEM((2 * tiles, tq, LANES), F32)
    return pl.pallas_call(
        body,
        grid=(b, w // (tiles * LANES), t // tq),
        in_specs=[qspec] + key_specs + [pl.BlockSpec(tt.shape, lambda bi, hp, qi: (0, 0))],
        out_specs=qspec,
        out_shape=jax.ShapeDtypeStruct((b, t, w), BF16),
        scratch_shapes=[state, state] + key_scratch,
        compiler_params=_params(3),
    )(q, *key_args, tt)


def _diff_kernel(lam_ref, gain_ref, q_ref, k_ref, v_ref, o_ref, m_ref, l_ref, acc_ref,
                 *, tq, tk, q_off, kv_len, n_diag, lam_init):
    qi = pl.program_id(2)
    n_heads = q_ref.shape[1] // LANES
    lane = lax.broadcasted_iota(jnp.int32, (tq, LANES), 1)
    first = lane < HEAD_DIM
    q_maps = []
    for hd in range(n_heads):
        q2 = q_ref[:, _lane_tile(hd)]
        zero = jnp.zeros_like(q2)
        q_maps += [jnp.where(first, q2, zero), jnp.where(first, zero, q2)]
    q_chunk = (q_off + qi * tq + lax.broadcasted_iota(jnp.int32, (tq, tk), 0)) // CHUNK
    col = lax.broadcasted_iota(jnp.int32, (tq, tk), 1)
    blk0 = (q_off + qi * tq) // tk
    ones = jnp.ones((tk, LANES), BF16)

    m_ref[...] = jnp.full(m_ref.shape, NEG, F32)
    l_ref[...] = jnp.zeros(l_ref.shape, F32)
    acc_ref[...] = jnp.zeros(acc_ref.shape, F32)

    def block(kb, masked):
        start = pl.multiple_of(kb * tk, tk)
        kblk = [k_ref[pl.ds(start, tk), _lane_tile(hd)] for hd in range(n_heads)]
        vext = [jnp.concatenate([v_ref[pl.ds(start, tk), _lane_tile(hd)], ones], axis=1)
                for hd in range(n_heads)]
        if masked:
            kpos = kb * tk + col
            mask = (kpos // CHUNK) <= q_chunk
            if kv_len % tk:
                mask = mask & (kpos < kv_len)
        maps = range(2 * n_heads)
        slabs = range(tk // LANES)
        s = [_dot_nt(q_maps[mp], kblk[mp // 2]) for mp in maps]
        if masked:
            s = [jnp.where(mask, s[mp], NEG) for mp in maps]
        m_old = [m_ref[mp] for mp in maps]
        m_new = []
        for mp in maps:
            mx = s[mp][:, :LANES]
            for j in slabs[1:]:
                mx = jnp.maximum(mx, s[mp][:, j * LANES:(j + 1) * LANES])
            m_new.append(jnp.maximum(m_old[mp], jnp.max(mx, axis=-1, keepdims=True)))
        alpha = [jnp.exp2(m_old[mp] - m_new[mp]) for mp in maps]
        p = [jnp.concatenate([jnp.exp2(s[mp][:, j * LANES:(j + 1) * LANES] - m_new[mp]).astype(BF16)
                              for j in slabs], axis=1) for mp in maps]
        pv = [_dot(p[mp], vext[mp // 2]) for mp in maps]
        for mp in maps:
            acc_ref[mp] = acc_ref[mp] * alpha[mp] + pv[mp][:, :LANES]
            l_ref[mp] = l_ref[mp] * alpha[mp] + pv[mp][:, LANES:]
            m_ref[mp] = m_new[mp]

    for j in reversed(range(n_diag)):
        block(blk0 + j, True)

    def body(i, carry):
        block(2 * i, False)
        block(2 * i + 1, False)
        return carry

    lax.fori_loop(0, blk0 // 2, body, 0)

    @pl.when(blk0 % 2 == 1)
    def _():
        block(blk0 - 1, False)

    lp = lam_ref[...]
    lam = (jnp.exp(jnp.sum(lp[0:1] * lp[1:2], axis=-1, keepdims=True))
           - jnp.exp(jnp.sum(lp[2:3] * lp[3:4], axis=-1, keepdims=True)) + lam_init)
    for hd in range(n_heads):
        o = acc_ref[2 * hd] / l_ref[2 * hd] - lam * (acc_ref[2 * hd + 1] / l_ref[2 * hd + 1])
        o = o * lax.rsqrt(jnp.mean(o * o, axis=-1, keepdims=True) + EPS)
        o_ref[:, _lane_tile(hd)] = (o * gain_ref[hd] * (1.0 - lam_init)).astype(o_ref.dtype)


def _diff_attn(q, k, v, lam_params, gain, tq, tk, q_off, kv_len, lam_init, past=None, total=None,
               tiles=1):
    b, t, w = q.shape
    n_diag = max(1, tq // tk)
    body, keys, key_args, key_specs, key_scratch = _key_operands(
        functools.partial(_diff_kernel, tq=tq, tk=tk, q_off=q_off, kv_len=kv_len, n_diag=n_diag,
                          lam_init=lam_init), 3, k, v, past, total, tiles)
    assert q_off % tk == 0 and (tq % tk == 0 or t == tq) and keys % tk == 0 and CHUNK % 8 == 0
    assert (q_off + t + tk - 1) // tk * tk <= keys and w % (tiles * LANES) == 0
    gain3 = gain.reshape(gain.shape[0], 1, gain.shape[1])
    qspec = pl.BlockSpec((None, tq, tiles * LANES), lambda bi, hd, qi: (bi, qi, hd))
    return pl.pallas_call(
        body,
        grid=(b, w // (tiles * LANES), t // tq),
        in_specs=[pl.BlockSpec(lam_params.shape, lambda bi, hd, qi: (0, 0)),
                  pl.BlockSpec((tiles, 1, LANES), lambda bi, hd, qi: (hd, 0, 0)),
                  qspec] + key_specs,
        out_specs=qspec,
        out_shape=jax.ShapeDtypeStruct((b, t, w), BF16),
        scratch_shapes=[pltpu.VMEM((2 * tiles, tq, LANES), F32)] * 3 + key_scratch,
        compiler_params=_params(3),
    )(lam_params, gain3, q, *key_args)


def _band_kernel(q_ref, k_ref, v_ref, bias_ref, o_ref, *, rows, n_sub, valid_lo, valid_hi):
    qi = pl.program_id(2)
    n_pairs = q_ref.shape[1] // LANES
    lane = lax.broadcasted_iota(jnp.int32, (rows, LANES), 1)
    first = lane < HEAD_DIM
    ones = jnp.ones((BAND_WINDOW, LANES), BF16)
    kcol = lax.broadcasted_iota(jnp.int32, (1, BAND_WINDOW), 1)

    def sub(j, carry):
        r0 = pl.multiple_of(j * rows, rows)
        ws = pl.multiple_of((qi * n_sub + j) * rows, rows)
        krow = ws + kcol
        penalty = jnp.where((krow >= valid_lo) & (krow < valid_hi), 0.0, NEG)
        for pair in range(n_pairs):
            q2 = q_ref[pl.ds(r0, rows), _lane_tile(pair)]
            zero = jnp.zeros_like(q2)
            kw = k_ref[pl.ds(ws, BAND_WINDOW), _lane_tile(pair)]
            vext = jnp.concatenate([v_ref[pl.ds(ws, BAND_WINDOW), _lane_tile(pair)], ones], axis=1)
            outs = []
            for head in range(2):
                qh = jnp.where(first if head == 0 else jnp.logical_not(first), q2, zero)
                s = _dot_nt(qh, kw) + bias_ref[2 * pair + head] + penalty
                p = jnp.exp2(s - jnp.max(s, axis=-1, keepdims=True))
                pv = _dot(p.astype(BF16), vext)
                outs.append(pv[:, :LANES] / pv[:, LANES:])
            o_ref[pl.ds(r0, rows), _lane_tile(pair)] = jnp.where(first, outs[0], outs[1]).astype(o_ref.dtype)
        return carry

    lax.fori_loop(0, n_sub, sub, 0, unroll=BAND_UNROLL if n_sub % BAND_UNROLL == 0 else 1)


def _band_bias_table(rel_bias, rows):
    heads = rel_bias.shape[0]
    span = BAND_WINDOW + rows - 1
    n_far = BAND_PAST - REL_CLIP + rows
    rb = rel_bias.astype(F32) * LOG2E
    near = rb[:, 2 * REL_CLIP - 1:0:-1]
    assert n_far + near.shape[1] == span
    e = jnp.concatenate([jnp.broadcast_to(rb[:, -1:], (heads, n_far)), near,
                         jnp.zeros((heads, 1), F32)], axis=1)
    skew = jnp.broadcast_to(e[:, None, :], (heads, rows, span + 1)).reshape(heads, rows * (span + 1))
    skew = skew[:, :rows * span].reshape(heads, rows, span)
    bias = skew[:, :, rows - 1:]
    i = jnp.arange(rows, dtype=jnp.int32)[:, None]
    rel_key = jnp.arange(BAND_WINDOW, dtype=jnp.int32)[None, :] - BAND_PAST
    kc = jnp.floor_divide(rel_key, CHUNK)
    qc = i // CHUNK
    in_band = (kc <= qc) & (kc >= qc - BAND_PAST // CHUNK)
    return jnp.where(in_band[None], bias, NEG)


def _band_attn(q, k, v, bias_tab, rows, n_sub, valid_lo, valid_hi, past=None, total=None, tiles=1):
    b, t, w = q.shape
    tq = rows * n_sub
    body, keys, key_args, key_specs, key_scratch = _key_operands(
        functools.partial(_band_kernel, rows=rows, n_sub=n_sub, valid_lo=valid_lo, valid_hi=valid_hi),
        1, k, v, past, total, tiles)
    assert t % tq == 0 and (t - rows) + BAND_WINDOW <= keys and w % (tiles * LANES) == 0
    qspec = pl.BlockSpec((None, tq, tiles * LANES), lambda bi, hp, qi: (bi, qi, hp))
    return pl.pallas_call(
        body,
        grid=(b, w // (tiles * LANES), t // tq),
        in_specs=[qspec] + key_specs +
                 [pl.BlockSpec((2 * tiles, rows, BAND_WINDOW), lambda bi, hp, qi: (hp, 0, 0))],
        out_specs=qspec,
        out_shape=jax.ShapeDtypeStruct((b, t, w), BF16),
        scratch_shapes=key_scratch,
        compiler_params=_params(3),
    )(q, *key_args, bias_tab)


def _rope_tables(pos, reps):
    half = HEAD_DIM // 2
    inv = ROPE_THETA ** (-jnp.arange(half, dtype=F32) / half)
    ang = pos.astype(F32)[:, None] * inv[None, :]
    cos, sin = jnp.cos(ang), jnp.sin(ang)
    cos = jnp.tile(cos, (reps, LANES // half))
    sin_signed = jnp.tile(jnp.concatenate([-sin, sin], axis=1), (reps, LANES // HEAD_DIM))
    return cos, sin_signed


def _cache_rows_kernel(x_ref, o_ref):
    o_ref[...] = x_ref[...].T.astype(o_ref.dtype)


def _cache_rows(cache, layer):
    n_layers, b, p = cache.shape[:3]
    if cache.shape[-1] % LANES == 0:
        return cache[layer].reshape(b, p, -1).astype(BF16)
    perm = (0, 1) + tuple(range(3, cache.ndim)) + (2,)
    cols = jnp.transpose(cache, perm).reshape(n_layers, b, -1, p)
    w = cols.shape[2]
    return pl.pallas_call(
        _cache_rows_kernel,
        grid=(b,),
        in_specs=[pl.BlockSpec((None, None, w, p), lambda i: (layer, i, 0, 0))],
        out_specs=pl.BlockSpec((None, p, w), lambda i: (i, 0, 0)),
        out_shape=jax.ShapeDtypeStruct((b, p, w), BF16),
        compiler_params=_params(1),
    )(cols)


def _trunk(x, p, past, weights, cfg):
    (norm_mix, w_in_even, w_out_even, diff_lambda, diff_norm, w_in_odd, w_out_odd, rel_bias,
     norm_ffn, w_gate, w_up, w_down, norm_ple, w_ple_gate, w_ple, norm_final) = weights
    b, t, d = x.shape
    n = b * t
    depth = norm_mix.shape[0]
    tm = cfg["tm"]
    q_off = 0 if past is None else past[0].shape[2]
    pos = q_off + jnp.arange(t, dtype=jnp.int32)
    if past is None:
        cos, sin_signed = _rope_tables(pos, 1)
    else:
        cos, sin_signed = _rope_tables(pos, tm // t)
    row = lambda a: a.reshape(1, -1)
    seq = lambda a: a.reshape(b, t, a.shape[-1])
    tiles_of = lambda a: a.shape[-1] // LANES if cfg["whole_width"] else 1

    h = x.reshape(n, d)
    n_even, n_odd = w_in_even.shape[0], w_in_odd.shape[0]
    even_state = None
    odd_state = None
    for li in range(depth):
        if li % 2 == 0:
            e = li // 2
            qa, ka, kab, va, vab, qb, kb, kbb, vb, vbb = _pre_even(
                h, row(norm_mix[li]), w_in_even[e], cos, sin_signed, tm, e, n_even, even_state,
                batch=b if past is None else None)
            even_state = (ka, va, kb, vb)
            lam_init = 0.8 - 0.6 * math.exp(-0.3 * li)
            keys = [seq(a) for a in (kab, vab, kbb, vbb)]
            if past is None:
                sb_past = df_past = total = None
                kv_len = t
            else:
                total = cfg["keys_total"]
                sb_past = (_cache_rows(past[0], e), _cache_rows(past[1], e))
                df_past = (_cache_rows(past[2], e), _cache_rows(past[3], e))
                kv_len = q_off + t
            o_a = _sb_attn(seq(qa), keys[0], keys[1], cfg["sb_tq"], q_off, sb_past, total, tiles_of(qa))
            o_b = _diff_attn(seq(qb), keys[2], keys[3], diff_lambda[e], diff_norm[e],
                             cfg["diff_tq"], cfg["diff_tk"], q_off, kv_len, lam_init, df_past, total,
                             tiles_of(qb))
            o_parts = [o_a.reshape(n, -1), o_b.reshape(n, -1)]
            w_out = w_out_even
        else:
            od = li // 2
            rows = cfg["band_rows"]
            bias_tab = _band_bias_table(rel_bias[od], rows)
            if past is None:
                assert t >= BAND_PAST
                q, k, kk, v, vv = _pre_odd(h, row(norm_mix[li]), w_in_odd[od], tm, od, n_odd,
                                           odd_state, batch=b)
                valid_lo, valid_hi = BAND_PAST, BAND_PAST + t
                bd_past = total = None
            else:
                q, k, kbf, v, vbf = _pre_odd(h, row(norm_mix[li]), w_in_odd[od], tm, od, n_odd,
                                             odd_state)
                kk, vv = seq(kbf), seq(vbf)
                bd_past = (_cache_rows(past[4], od), _cache_rows(past[5], od))
                cache_rows = bd_past[0].shape[1]
                assert cache_rows == BAND_PAST
                valid_lo, valid_hi = 0, cache_rows + t
                total = BAND_WINDOW
            odd_state = (k, v)
            o = _band_attn(seq(q), kk, vv, bias_tab, rows, cfg["band_sub"], valid_lo, valid_hi,
                           bd_past, total, tiles_of(q))
            o_parts = [o.reshape(n, -1)]
            w_out = w_out_odd
        gfin = row(norm_final) if li == depth - 1 else None
        h = _post(h, o_parts, p.reshape(depth, n, -1), li, w_out, li // 2, row(norm_ffn[li]), w_gate,
                  w_up, w_down, row(norm_ple[li]), w_ple_gate, w_ple, gfin, tm)
    y = h.reshape(b, t, d)
    rows = lambda a, shp: a.reshape((a.shape[0], b, -1) + shp)

    def cols(a, shp):
        a = a.reshape(a.shape[:2] + shp + a.shape[3:])
        return jnp.moveaxis(a, -1, 2)

    heads = cols if past is None else rows
    sb_k, sb_v, df_k, df_v = even_state
    bd_k, bd_v = odd_state
    state = (heads(sb_k, (H_SB, HEAD_DIM)), heads(sb_v, (H_SB, HEAD_DIM)),
             heads(df_k, (H_DIFF, 2, HEAD_DIM)), rows(df_v, (H_DIFF, 2 * HEAD_DIM)),
             heads(bd_k, (H_BAND, HEAD_DIM)), heads(bd_v, (H_BAND, HEAD_DIM)))
    return y, state


def kernel(x_prompt, x_sample, cache_sb_k, cache_sb_v, cache_diff_k, cache_diff_v, cache_band_k, cache_band_v, p_prompt, p_sample, norm_mix, w_in_even, w_out_even, diff_lambda, diff_norm, w_in_odd, w_out_odd, rel_bias, norm_ffn, w_gate, w_up, w_down, norm_ple, w_ple_gate, w_ple, norm_final):
    bf = _to_bf16
    weights = (norm_mix, bf(w_in_even), bf(w_out_even), diff_lambda, diff_norm, bf(w_in_odd),
               bf(w_out_odd), rel_bias, norm_ffn, bf(w_gate), bf(w_up), bf(w_down), norm_ple,
               bf(w_ple_gate), bf(w_ple), norm_final)
    t_p = x_prompt.shape[1]
    t_s = x_sample.shape[1]
    past_len = cache_sb_k.shape[2]
    diff_tk = 512
    cfg_p = dict(tm=512, sb_tq=256, diff_tq=min(1024, t_p), diff_tk=diff_tk, band_rows=LANES,
                 band_sub=min(8, t_p // LANES), whole_width=False)
    keys_total = -(-(past_len + t_s) // diff_tk) * diff_tk
    cfg_s = dict(tm=min(512, x_sample.shape[0] * t_s), sb_tq=t_s, diff_tq=t_s, diff_tk=diff_tk, band_rows=t_s, band_sub=1,
                 keys_total=keys_total, whole_width=True)
    y_p, st_p = _trunk(x_prompt, p_prompt, None, weights, cfg_p)
    past = (cache_sb_k, cache_sb_v, cache_diff_k, cache_diff_v, cache_band_k, cache_band_v)
    y_s, st_s = _trunk(x_sample, p_sample, past, weights, cfg_s)
    return (y_p, y_s) + tuple(st_p) + tuple(st_s)
```

```python
import functools
import math

import jax
import jax.numpy as jnp
from jax import lax
from jax.experimental import pallas as pl
from jax.experimental.pallas import tpu as pltpu

CHUNK = 64
HEAD_DIM = 64
H_SB = 8
H_DIFF = 4
H_BAND = 16
BAND_PAST = 8 * CHUNK
REL_CLIP = 128
ROPE_THETA = 10000.0
EPS = 1e-6
NEG = -1e30
SCALE = HEAD_DIM ** -0.5
LOG2E = 1.4426950408889634
QSCALE = SCALE * LOG2E

LANES = 128
VMEM_LIMIT = 56 * 1024 * 1024

SB_KEY_BLOCK = 2 * LANES
SB_EXIT_LOG2 = -160.0
BAND_WINDOW = BAND_PAST + LANES
BAND_UNROLL = 8

F32 = jnp.float32
BF16 = jnp.bfloat16


def _rms(x, g):
    return x * lax.rsqrt(jnp.mean(x * x, axis=-1, keepdims=True) + EPS) * g


def _sigmoid(x):
    return 1.0 / (1.0 + jnp.exp(-x))


def _dot(a, b):
    return jnp.dot(a, b, preferred_element_type=F32)


def _dot_nt(a, b):
    return lax.dot_general(a, b, (((1,), (1,)), ((), ())), preferred_element_type=F32)


def _params(n_axes):
    return pltpu.CompilerParams(dimension_semantics=("arbitrary",) * n_axes,
                                vmem_limit_bytes=VMEM_LIMIT)


def _rope(x, cos, sin_signed, first_half):
    outs = []
    for j in range(x.shape[1] // LANES):
        xj = x[:, j * LANES:(j + 1) * LANES]
        partner = jnp.where(first_half, pltpu.roll(xj, LANES - HEAD_DIM // 2, 1),
                            pltpu.roll(xj, HEAD_DIM // 2, 1))
        outs.append(xj * cos + partner * sin_signed)
    return jnp.concatenate(outs, axis=1)


def _pre_even_kernel(h_ref, g_ref, w_ref, cos_ref, sin_ref, *rest, transposed):
    flip = (lambda a: a.T) if transposed else (lambda a: a)
    (qa_ref, ka_ref, kab_ref, va_ref, vab_ref, qb_ref, kb_ref, kbb_ref, vb_ref, vbb_ref) = rest[-10:]
    hn = _rms(h_ref[...], g_ref[...]).astype(BF16)
    width = qa_ref.shape[1]

    def proj(c):
        return _dot(hn, w_ref[:, c * width:(c + 1) * width])

    cos = cos_ref[...]
    sin_signed = sin_ref[...]
    lane = lax.broadcasted_iota(jnp.int32, cos.shape, 1)
    first_half = (lane % HEAD_DIM) < (HEAD_DIM // 2)

    qa_ref[...] = (proj(0) * QSCALE).astype(BF16)
    ka = proj(1)
    ka_ref[...] = flip(ka)
    kab_ref[...] = ka.astype(BF16)
    va = proj(2)
    va_ref[...] = flip(va)
    vab_ref[...] = va.astype(BF16)
    qb_ref[...] = (_rope(proj(3), cos, sin_signed, first_half) * QSCALE).astype(BF16)
    kb = _rope(proj(4), cos, sin_signed, first_half)
    kb_ref[...] = flip(kb)
    kbb_ref[...] = kb.astype(BF16)
    vb = proj(5)
    vb_ref[...] = vb
    vbb_ref[...] = vb.astype(BF16)


def _pre_odd_kernel(h_ref, g_ref, w_ref, *rest, padded):
    q_ref, k_ref, kb_ref, v_ref, vb_ref = rest[-5:]
    width = q_ref.shape[1]

    def project(keep_f32):
        hn = _rms(h_ref[...], g_ref[...]).astype(BF16)
        q_ref[...] = (_dot(hn, w_ref[:, :width]) * QSCALE).astype(BF16)
        k = _dot(hn, w_ref[:, width:2 * width])
        kb_ref[...] = k.astype(BF16)
        v = _dot(hn, w_ref[:, 2 * width:])
        vb_ref[...] = v.astype(BF16)
        keep_f32(k, v)

    def store_f32(k, v):
        k_ref[...] = k.T if padded else k
        v_ref[...] = v.T if padded else v

    if not padded:
        project(store_f32)
        return

    j = pl.program_id(1)

    @pl.when(j == 0)
    def _():
        kb_ref[...] = jnp.zeros(kb_ref.shape, BF16)
        vb_ref[...] = jnp.zeros(vb_ref.shape, BF16)

    @pl.when(j > 0)
    def _():
        project(lambda k, v: pl.when(j == pl.num_programs(1) - 1)(lambda: store_f32(k, v)))


def _stack_alias(prev, n_fixed_inputs, out_positions):
    if prev is None:
        return [], [], {}
    specs = [pl.BlockSpec(memory_space=pl.ANY)] * len(prev)
    aliases = {n_fixed_inputs + i: o for i, o in enumerate(out_positions)}
    return list(prev), specs, aliases


def _pre_even(h, g, w, cos, sin_signed, tm, slot, n_slots, prev, batch=None):
    n, d = h.shape
    width = w.shape[1] // 6
    n_pos_blocks = cos.shape[0] // tm
    tok = lambda wd: pl.BlockSpec((tm, wd), lambda i: (i, 0))
    const = lambda a: pl.BlockSpec(a.shape, lambda i: (0, 0))
    pos = pl.BlockSpec((tm, LANES), lambda i: (i % n_pos_blocks, 0))
    stk = pl.BlockSpec((None, tm, width), lambda i: (slot, i, 0))
    f32s = jax.ShapeDtypeStruct((n_slots, n, width), F32)
    bf16o = jax.ShapeDtypeStruct((n, width), BF16)
    t, s = tok(width), stk
    st, f32t = s, f32s
    if batch is not None:
        tiles = n // batch // tm
        st = pl.BlockSpec((None, None, width, tm), lambda i: (slot, i // tiles, 0, i % tiles))
        f32t = jax.ShapeDtypeStruct((n_slots, batch, width, n // batch), F32)
    prev_args, prev_specs, aliases = _stack_alias(prev, 5, (1, 3, 6, 8))
    return pl.pallas_call(
        functools.partial(_pre_even_kernel, transposed=batch is not None),
        grid=(n // tm,),
        in_specs=[tok(d), const(g), const(w), pos, pos] + prev_specs,
        out_specs=[t, st, t, st, t, t, st, t, s, t],
        out_shape=[bf16o, f32t, bf16o, f32t, bf16o, bf16o, f32t, bf16o, f32s, bf16o],
        input_output_aliases=aliases,
        compiler_params=_params(1),
    )(h, g, w, cos, sin_signed, *prev_args)


def _pre_odd(h, g, w, tm, slot, n_slots, prev, batch=None):
    n, d = h.shape
    width = w.shape[1] // 3
    padded = batch is not None
    bf16o = jax.ShapeDtypeStruct((n, width), BF16)
    if padded:
        assert tm == BAND_PAST
        tiles = n // batch // tm
        grid = (batch, tiles + 1)
        tile = lambda bi, j: bi * tiles + jnp.maximum(j - 1, 0)
        tok = lambda wd: pl.BlockSpec((tm, wd), lambda bi, j: (tile(bi, j), 0))
        const = lambda a: pl.BlockSpec(a.shape, lambda bi, j: (0, 0))
        stk = pl.BlockSpec((None, None, width, tm), lambda bi, j: (slot, bi, 0, 0))
        pad = pl.BlockSpec((None, tm, width), lambda bi, j: (bi, j, 0))
        f32s = jax.ShapeDtypeStruct((n_slots, batch, width, tm), F32)
        bf16p = jax.ShapeDtypeStruct((batch, (tiles + 1) * tm, width), BF16)
        out_specs, out_shape = [tok(width), stk, pad, stk, pad], [bf16o, f32s, bf16p, f32s, bf16p]
    else:
        grid = (n // tm,)
        tok = lambda wd: pl.BlockSpec((tm, wd), lambda i: (i, 0))
        const = lambda a: pl.BlockSpec(a.shape, lambda i: (0, 0))
        stk = pl.BlockSpec((None, tm, width), lambda i: (slot, i, 0))
        f32s = jax.ShapeDtypeStruct((n_slots, n, width), F32)
        out_specs, out_shape = [tok(width), stk, tok(width), stk, tok(width)], [bf16o, f32s, bf16o, f32s, bf16o]
    prev_args, prev_specs, aliases = _stack_alias(prev, 3, (1, 3))
    return pl.pallas_call(
        functools.partial(_pre_odd_kernel, padded=padded),
        grid=grid,
        in_specs=[tok(d), const(g), const(w)] + prev_specs,
        out_specs=out_specs,
        out_shape=out_shape,
        input_output_aliases=aliases,
        compiler_params=_params(len(grid)),
    )(h, g, w, *prev_args)


def _cast_kernel(x_ref, o_ref):
    o_ref[...] = x_ref[...].astype(o_ref.dtype)


def _to_bf16(w):
    layers, rows, cols = w.shape
    tr = rows // 4
    assert rows % 4 == 0 and tr % 16 == 0
    spec = pl.BlockSpec((None, tr, cols), lambda i, j: (i, j, 0))
    return pl.pallas_call(
        _cast_kernel,
        grid=(layers, rows // tr),
        in_specs=[spec],
        out_specs=spec,
        out_shape=jax.ShapeDtypeStruct(w.shape, BF16),
        compiler_params=_params(2),
    )(w)


def _post_kernel(*refs, n_o, ffn_chunk, final):
    h_ref = refs[0]
    o_refs = refs[1:1 + n_o]
    (p_ref, wout_ref, gffn_ref, wg_ref, wu_ref, wd_ref, gple_ref, wpg_ref, wp_ref) = refs[1 + n_o:10 + n_o]
    gfin_ref = refs[10 + n_o] if final else None
    out_ref = refs[-1]

    h = h_ref[...]
    off = 0
    for o_ref in o_refs:
        wd = o_ref.shape[1]
        h = h + _dot(o_ref[...], wout_ref[off:off + wd, :])
        off += wd

    hn = _rms(h, gffn_ref[...]).astype(BF16)
    hidden = wg_ref.shape[1]
    for c in range(hidden // ffn_chunk):
        cols = slice(c * ffn_chunk, (c + 1) * ffn_chunk)
        gt = _dot(hn, wg_ref[:, cols])
        up = _dot(hn, wu_ref[:, cols])
        act = (gt * _sigmoid(gt) * up).astype(BF16)
        h = h + _dot(act, wd_ref[cols, :])

    gate = _sigmoid(_dot(_rms(h, gple_ref[...]).astype(BF16), wpg_ref[...]))
    h = h + _dot(p_ref[...].astype(BF16), wp_ref[...]) * gate
    if final:
        h = _rms(h, gfin_ref[...])
    out_ref[...] = h


def _post(h, o_parts, p, layer, wout, wout_layer, gffn, wg, wu, wd, gple, wpg, wp, gfin, tm):
    n, d = h.shape
    final = gfin is not None
    hidden = wg.shape[2]
    ffn_chunk = hidden
    tok = lambda a: pl.BlockSpec((tm, a.shape[1]), lambda i: (i, 0))
    row = lambda a: pl.BlockSpec(a.shape, lambda i: (0, 0), pipeline_mode=pl.Buffered(1))
    mat = lambda a, li: pl.BlockSpec((None,) + a.shape[1:], lambda i: (li, 0, 0),
                                     pipeline_mode=pl.Buffered(1))
    consts = [wout, gffn, wg, wu, wd, gple, wpg, wp] + ([gfin] if final else [])
    const_specs = ([mat(wout, wout_layer), row(gffn)] + [mat(a, layer) for a in (wg, wu, wd)]
                   + [row(gple), mat(wpg, layer), mat(wp, layer)] + ([row(gfin)] if final else []))
    p_spec = pl.BlockSpec((None, tm, p.shape[2]), lambda i: (layer, i, 0))
    return pl.pallas_call(
        functools.partial(_post_kernel, n_o=len(o_parts), ffn_chunk=ffn_chunk, final=final),
        grid=(n // tm,),
        in_specs=[tok(h)] + [tok(o) for o in o_parts] + [p_spec] + const_specs,
        out_specs=tok(h),
        out_shape=jax.ShapeDtypeStruct((n, d), F32),
        compiler_params=_params(1),
    )(h, *o_parts, p, *consts)


def _lane_tile(i):
    return slice(i * LANES, (i + 1) * LANES)


def _sb_kernel(q_ref, k_ref, v_ref, tt_ref, o_ref, acc_ref, run_ref, *, tq, q_off):
    qi = pl.program_id(2)
    n_pairs = q_ref.shape[1] // LANES
    lane = lax.broadcasted_iota(jnp.int32, (tq, LANES), 1)
    row_pos = q_off + qi * tq + lax.broadcasted_iota(jnp.int32, (tq, LANES), 0)
    first = lane < HEAD_DIM
    blk0 = (q_off + qi * tq) // SB_KEY_BLOCK
    tt = tt_ref[...]
    heads = range(2 * n_pairs)
    q_heads = []
    for pair in range(n_pairs):
        q2 = q_ref[:, _lane_tile(pair)]
        zero = jnp.zeros_like(q2)
        q_heads += [jnp.where(first, q2, zero), jnp.where(first, zero, q2)]
    acc_ref[...] = jnp.zeros(acc_ref.shape, F32)
    run_ref[...] = jnp.zeros(run_ref.shape, F32)

    def span(first_block, n_blocks, own_block):
        n_halves = 2 * n_blocks
        width = n_halves * LANES
        start = first_block * SB_KEY_BLOCK
        if not isinstance(start, int):
            start = pl.multiple_of(start, SB_KEY_BLOCK)
        kblk = [k_ref[pl.ds(start, width), _lane_tile(pair)] for pair in range(n_pairs)]
        vblk = [v_ref[pl.ds(start, width), _lane_tile(pair)] for pair in range(n_pairs)]
        s = [_dot_nt(q_heads[h], kblk[h // 2]) for h in heads]
        soft = [jnp.log(1.0 + jnp.exp2(-jnp.abs(s[h]))) * LOG2E for h in heads]
        log_beta = [jnp.minimum(s[h], 0.0) - soft[h] for h in heads]
        log_keep = [log_beta[h] - s[h] for h in heads]
        masks = {}
        if own_block:
            for i in (n_halves - 2, n_halves - 1):
                masks[i] = (first_block * SB_KEY_BLOCK + i * LANES + lane) < row_pos
        cs = []
        for h in heads:
            per_half = []
            for i in range(n_halves):
                keep = log_keep[h][:, _lane_tile(i)]
                if i in masks:
                    keep = jnp.where(masks[i], keep, 0.0)
                hi = keep.astype(BF16)
                lo = (keep - hi.astype(F32)).astype(BF16)
                per_half.append(_dot(jnp.concatenate([hi, lo], axis=1), tt))
            cs.append(per_half)
        for h in heads:
            run = run_ref[h]
            w = [None] * n_halves
            for i in reversed(range(n_halves)):
                w_i = jnp.exp2(log_beta[h][:, _lane_tile(i)] + cs[h][i][:, :LANES] + run)
                if i in masks:
                    w_i = jnp.where(masks[i], w_i, 0.0)
                w[i] = w_i.astype(BF16)
                run = run + cs[h][i][:, LANES:]
            acc_ref[h] += _dot(jnp.concatenate(w, axis=1), vblk[h // 2])
            run_ref[h] = run

    @pl.when(blk0 == 0)
    def _():
        span(0, 1, True)

    @pl.when(blk0 > 0)
    def _():
        span(blk0 - 1, 2, True)

    def cond(carry):
        i, live = carry
        return jnp.logical_and(i < blk0 - 1, live > SB_EXIT_LOG2)

    def live_mass():
        return jnp.max(functools.reduce(jnp.maximum, [run_ref[h] for h in heads]))

    def body(carry):
        i, _ = carry
        span(blk0 - 2 - i, 1, False)
        return i + 1, live_mass()

    lax.while_loop(cond, body, (jnp.int32(0), live_mass()))
    for pair in range(n_pairs):
        o_ref[:, _lane_tile(pair)] = jnp.where(first, acc_ref[2 * pair],
                                               acc_ref[2 * pair + 1]).astype(o_ref.dtype)


def _suffix_matrix():
    r = lax.broadcasted_iota(jnp.int32, (2 * LANES, 2 * LANES), 0) % LANES
    c = lax.broadcasted_iota(jnp.int32, (2 * LANES, 2 * LANES), 1)
    return jnp.where((c >= LANES) | (r > c), 1.0, 0.0).astype(BF16)


def _keys_from_past(kernel_fn, n_before):
    def wrapped(*refs):
        past_new = refs[n_before:n_before + 4]
        k_buf, v_buf = refs[-2:]
        for past, new, buf in ((past_new[0], past_new[2], k_buf), (past_new[1], past_new[3], v_buf)):
            p, t = past.shape[0], new.shape[0]
            buf[0:p, :] = past[...]
            buf[p:p + t, :] = new[...]
            buf[p + t:, :] = jnp.zeros((buf.shape[0] - p - t, buf.shape[1]), buf.dtype)
        kernel_fn(*refs[:n_before], k_buf, v_buf, *refs[n_before + 4:-2])
    return wrapped


def _key_operands(kernel_fn, n_before, k, v, past, total, tiles=1):
    index = lambda bi, hp, qi: (bi, 0, hp)
    spec = lambda a: pl.BlockSpec((None, a.shape[1], tiles * LANES), index)
    if past is None:
        return kernel_fn, k.shape[1], [k, v], [spec(k), spec(v)], []
    buf = pltpu.VMEM((total, tiles * LANES), BF16)
    operands = [past[0], past[1], k, v]
    return _keys_from_past(kernel_fn, n_before), total, operands, [spec(a) for a in operands], [buf, buf]


def _sb_attn(q, k, v, tq, q_off, past=None, total=None, tiles=1):
    b, t, w = q.shape
    body, keys, key_args, key_specs, key_scratch = _key_operands(
        functools.partial(_sb_kernel, tq=tq, q_off=q_off), 1, k, v, past, total, tiles)
    assert q_off % SB_KEY_BLOCK == 0 and (tq == SB_KEY_BLOCK or (t == tq and tq < SB_KEY_BLOCK))
    assert -(-(q_off + t) // SB_KEY_BLOCK) * SB_KEY_BLOCK <= keys and w % (tiles * LANES) == 0
    tt = _suffix_matrix()
    qspec = pl.BlockSpec((None, tq, tiles * LANES), lambda bi, hp, qi: (bi, qi, hp))
    state = pltpu.VMEM((2 * tiles, tq, LANES), F32)
    return pl.pallas_call(
        body,
        grid=(b, w // (tiles * LANES), t // tq),
        in_specs=[qspec] + key_specs + [pl.BlockSpec(tt.shape, lambda bi, hp, qi: (0, 0))],
        out_specs=qspec,
        out_shape=jax.ShapeDtypeStruct((b, t, w), BF16),
        scratch_shapes=[state, state] + key_scratch,
        compiler_params=_params(3),
    )(q, *key_args, tt)


def _diff_kernel(lam_ref, gain_ref, q_ref, k_ref, v_ref, o_ref, m_ref, l_ref, acc_ref,
                 *, tq, tk, q_off, kv_len, n_diag, lam_init):
    qi = pl.program_id(2)
    n_heads = q_ref.shape[1] // LANES
    lane = lax.broadcasted_iota(jnp.int32, (tq, LANES), 1)
    first = lane < HEAD_DIM
    q_maps = []
    for hd in range(n_heads):
        q2 = q_ref[:, _lane_tile(hd)]
        zero = jnp.zeros_like(q2)
        q_maps += [jnp.where(first, q2, zero), jnp.where(first, zero, q2)]
    diag_rows = min(tq, tk)
    row_chunk = lax.broadcasted_iota(jnp.int32, (diag_rows, tk), 0) // CHUNK
    col = lax.broadcasted_iota(jnp.int32, (diag_rows, tk), 1)
    diag_mask = (col // CHUNK) <= row_chunk
    blk0 = (q_off + qi * tq) // tk
    ones = jnp.ones((tk, LANES), BF16)

    m_ref[...] = jnp.full(m_ref.shape, NEG, F32)
    l_ref[...] = jnp.zeros(l_ref.shape, F32)
    acc_ref[...] = jnp.zeros(acc_ref.shape, F32)

    def block(kb, masked, rows=slice(0, tq)):
        start = pl.multiple_of(kb * tk, tk)
        kblk = [k_ref[pl.ds(start, tk), _lane_tile(hd)] for hd in range(n_heads)]
        vext = [jnp.concatenate([v_ref[pl.ds(start, tk), _lane_tile(hd)], ones], axis=1)
                for hd in range(n_heads)]
        if masked:
            mask = diag_mask
            if kv_len % tk:
                mask = mask & (kb * tk + col < kv_len)
        maps = range(2 * n_heads)
        slabs = range(tk // LANES)
        s = [_dot_nt(q_maps[mp][rows], kblk[mp // 2]) for mp in maps]
        if masked:
            s = [jnp.where(mask, s[mp], NEG) for mp in maps]
        m_old = [m_ref[mp, rows, :] for mp in maps]
        m_new = []
        for mp in maps:
            mx = s[mp][:, :LANES]
            for j in slabs[1:]:
                mx = jnp.maximum(mx, s[mp][:, j * LANES:(j + 1) * LANES])
            m_new.append(jnp.maximum(m_old[mp], jnp.max(mx, axis=-1, keepdims=True)))
        alpha = [jnp.exp2(m_old[mp] - m_new[mp]) for mp in maps]
        p = [jnp.concatenate([jnp.exp2(s[mp][:, j * LANES:(j + 1) * LANES] - m_new[mp]).astype(BF16)
                              for j in slabs], axis=1) for mp in maps]
        pv = [_dot(p[mp], vext[mp // 2]) for mp in maps]
        for mp in maps:
            acc_ref[mp, rows, :] = acc_ref[mp, rows, :] * alpha[mp] + pv[mp][:, :LANES]
            l_ref[mp, rows, :] = l_ref[mp, rows, :] * alpha[mp] + pv[mp][:, LANES:]
            m_ref[mp, rows, :] = m_new[mp]

    for r in range(n_diag):
        rows = slice(r * diag_rows, (r + 1) * diag_rows)
        block(blk0 + r, True, rows)
        for j in range(r):
            block(blk0 + j, False, rows)

    def body(i, carry):
        block(2 * i, False)
        block(2 * i + 1, False)
        return carry

    lax.fori_loop(0, blk0 // 2, body, 0)

    @pl.when(blk0 % 2 == 1)
    def _():
        block(blk0 - 1, False)

    lp = lam_ref[...]
    lam = (jnp.exp(jnp.sum(lp[0:1] * lp[1:2], axis=-1, keepdims=True))
           - jnp.exp(jnp.sum(lp[2:3] * lp[3:4], axis=-1, keepdims=True)) + lam_init)
    for hd in range(n_heads):
        o = acc_ref[2 * hd] / l_ref[2 * hd] - lam * (acc_ref[2 * hd + 1] / l_ref[2 * hd + 1])
        o = o * lax.rsqrt(jnp.mean(o * o, axis=-1, keepdims=True) + EPS)
        o_ref[:, _lane_tile(hd)] = (o * gain_ref[hd] * (1.0 - lam_init)).astype(o_ref.dtype)


def _diff_attn(q, k, v, lam_params, gain, tq, tk, q_off, kv_len, lam_init, past=None, total=None,
               tiles=1):
    b, t, w = q.shape
    n_diag = max(1, tq // tk)
    body, keys, key_args, key_specs, key_scratch = _key_operands(
        functools.partial(_diff_kernel, tq=tq, tk=tk, q_off=q_off, kv_len=kv_len, n_diag=n_diag,
                          lam_init=lam_init), 3, k, v, past, total, tiles)
    assert q_off % tk == 0 and (tq % tk == 0 or t == tq) and keys % tk == 0 and CHUNK % 8 == 0
    assert (q_off + t + tk - 1) // tk * tk <= keys and w % (tiles * LANES) == 0
    gain3 = gain.reshape(gain.shape[0], 1, gain.shape[1])
    qspec = pl.BlockSpec((None, tq, tiles * LANES), lambda bi, hd, qi: (bi, qi, hd))
    return pl.pallas_call(
        body,
        grid=(b, w // (tiles * LANES), t // tq),
        in_specs=[pl.BlockSpec(lam_params.shape, lambda bi, hd, qi: (0, 0)),
                  pl.BlockSpec((tiles, 1, LANES), lambda bi, hd, qi: (hd, 0, 0)),
                  qspec] + key_specs,
        out_specs=qspec,
        out_shape=jax.ShapeDtypeStruct((b, t, w), BF16),
        scratch_shapes=[pltpu.VMEM((2 * tiles, tq, LANES), F32)] * 3 + key_scratch,
        compiler_params=_params(3),
    )(lam_params, gain3, q, *key_args)


def _band_kernel(q_ref, k_ref, v_ref, bias_ref, o_ref, *, rows, n_sub, valid_lo, valid_hi):
    qi = pl.program_id(2)
    n_pairs = q_ref.shape[1] // LANES
    lane = lax.broadcasted_iota(jnp.int32, (rows, LANES), 1)
    first = lane < HEAD_DIM
    ones = jnp.ones((BAND_WINDOW, LANES), BF16)
    kcol = lax.broadcasted_iota(jnp.int32, (1, BAND_WINDOW), 1)

    def sub(j, carry):
        r0 = pl.multiple_of(j * rows, rows)
        ws = pl.multiple_of((qi * n_sub + j) * rows, rows)
        krow = ws + kcol
        penalty = jnp.where((krow >= valid_lo) & (krow < valid_hi), 0.0, NEG)
        for pair in range(n_pairs):
            q2 = q_ref[pl.ds(r0, rows), _lane_tile(pair)]
            zero = jnp.zeros_like(q2)
            kw = k_ref[pl.ds(ws, BAND_WINDOW), _lane_tile(pair)]
            vext = jnp.concatenate([v_ref[pl.ds(ws, BAND_WINDOW), _lane_tile(pair)], ones], axis=1)
            outs = []
            for head in range(2):
                qh = jnp.where(first if head == 0 else jnp.logical_not(first), q2, zero)
                s = _dot_nt(qh, kw) + bias_ref[2 * pair + head] + penalty
                p = jnp.exp2(s - jnp.max(s, axis=-1, keepdims=True))
                pv = _dot(p.astype(BF16), vext)
                outs.append(pv[:, :LANES] / pv[:, LANES:])
            o_ref[pl.ds(r0, rows), _lane_tile(pair)] = jnp.where(first, outs[0], outs[1]).astype(o_ref.dtype)
        return carry

    lax.fori_loop(0, n_sub, sub, 0, unroll=BAND_UNROLL if n_sub % BAND_UNROLL == 0 else 1)


def _band_bias_table(rel_bias, rows):
    heads = rel_bias.shape[0]
    span = BAND_WINDOW + rows - 1
    n_far = BAND_PAST - REL_CLIP + rows
    rb = rel_bias.astype(F32) * LOG2E
    near = rb[:, 2 * REL_CLIP - 1:0:-1]
    assert n_far + near.shape[1] == span
    e = jnp.concatenate([jnp.broadcast_to(rb[:, -1:], (heads, n_far)), near,
                         jnp.zeros((heads, 1), F32)], axis=1)
    skew = jnp.broadcast_to(e[:, None, :], (heads, rows, span + 1)).reshape(heads, rows * (span + 1))
    skew = skew[:, :rows * span].reshape(heads, rows, span)
    bias = skew[:, :, rows - 1:]
    i = jnp.arange(rows, dtype=jnp.int32)[:, None]
    rel_key = jnp.arange(BAND_WINDOW, dtype=jnp.int32)[None, :] - BAND_PAST
    kc = jnp.floor_divide(rel_key, CHUNK)
    qc = i // CHUNK
    in_band = (kc <= qc) & (kc >= qc - BAND_PAST // CHUNK)
    return jnp.where(in_band[None], bias, NEG)


def _band_attn(q, k, v, bias_tab, rows, n_sub, valid_lo, valid_hi, past=None, total=None, tiles=1):
    b, t, w = q.shape
    tq = rows * n_sub
    body, keys, key_args, key_specs, key_scratch = _key_operands(
        functools.partial(_band_kernel, rows=rows, n_sub=n_sub, valid_lo=valid_lo, valid_hi=valid_hi),
        1, k, v, past, total, tiles)
    assert t % tq == 0 and (t - rows) + BAND_WINDOW <= keys and w % (tiles * LANES) == 0
    qspec = pl.BlockSpec((None, tq, tiles * LANES), lambda bi, hp, qi: (bi, qi, hp))
    return pl.pallas_call(
        body,
        grid=(b, w // (tiles * LANES), t // tq),
        in_specs=[qspec] + key_specs +
                 [pl.BlockSpec((2 * tiles, rows, BAND_WINDOW), lambda bi, hp, qi: (hp, 0, 0))],
        out_specs=qspec,
        out_shape=jax.ShapeDtypeStruct((b, t, w), BF16),
        scratch_shapes=key_scratch,
        compiler_params=_params(3),
    )(q, *key_args, bias_tab)


def _rope_tables(pos, reps):
    half = HEAD_DIM // 2
    inv = ROPE_THETA ** (-jnp.arange(half, dtype=F32) / half)
    ang = pos.astype(F32)[:, None] * inv[None, :]
    cos, sin = jnp.cos(ang), jnp.sin(ang)
    cos = jnp.tile(cos, (reps, LANES // half))
    sin_signed = jnp.tile(jnp.concatenate([-sin, sin], axis=1), (reps, LANES // HEAD_DIM))
    return cos, sin_signed


def _cache_rows_kernel(x_ref, o_ref):
    o_ref[...] = x_ref[...].T.astype(o_ref.dtype)


def _cache_rows(cache, layer):
    n_layers, b, p = cache.shape[:3]
    if cache.shape[-1] % LANES == 0:
        return cache[layer].reshape(b, p, -1).astype(BF16)
    perm = (0, 1) + tuple(range(3, cache.ndim)) + (2,)
    cols = jnp.transpose(cache, perm).reshape(n_layers, b, -1, p)
    w = cols.shape[2]
    return pl.pallas_call(
        _cache_rows_kernel,
        grid=(b,),
        in_specs=[pl.BlockSpec((None, None, w, p), lambda i: (layer, i, 0, 0))],
        out_specs=pl.BlockSpec((None, p, w), lambda i: (i, 0, 0)),
        out_shape=jax.ShapeDtypeStruct((b, p, w), BF16),
        compiler_params=_params(1),
    )(cols)


def _trunk(x, p, past, weights, cfg):
    (norm_mix, w_in_even, w_out_even, diff_lambda, diff_norm, w_in_odd, w_out_odd, rel_bias,
     norm_ffn, w_gate, w_up, w_down, norm_ple, w_ple_gate, w_ple, norm_final) = weights
    b, t, d = x.shape
    n = b * t
    depth = norm_mix.shape[0]
    tm = cfg["tm"]
    q_off = 0 if past is None else past[0].shape[2]
    pos = q_off + jnp.arange(t, dtype=jnp.int32)
    if past is None:
        cos, sin_signed = _rope_tables(pos, 1)
    else:
        cos, sin_signed = _rope_tables(pos, tm // t)
    row = lambda a: a.reshape(1, -1)
    seq = lambda a: a.reshape(b, t, a.shape[-1])
    tiles_of = lambda a: a.shape[-1] // LANES if cfg["whole_width"] else 1

    h = x.reshape(n, d)
    n_even, n_odd = w_in_even.shape[0], w_in_odd.shape[0]
    even_state = None
    odd_state = None
    for li in range(depth):
        if li % 2 == 0:
            e = li // 2
            qa, ka, kab, va, vab, qb, kb, kbb, vb, vbb = _pre_even(
                h, row(norm_mix[li]), w_in_even[e], cos, sin_signed, tm, e, n_even, even_state,
                batch=b if past is None else None)
            even_state = (ka, va, kb, vb)
            lam_init = 0.8 - 0.6 * math.exp(-0.3 * li)
            keys = [seq(a) for a in (kab, vab, kbb, vbb)]
            if past is None:
                sb_past = df_past = total = None
                kv_len = t
            else:
                total = cfg["keys_total"]
                sb_past = (_cache_rows(past[0], e), _cache_rows(past[1], e))
                df_past = (_cache_rows(past[2], e), _cache_rows(past[3], e))
                kv_len = q_off + t
            o_a = _sb_attn(seq(qa), keys[0], keys[1], cfg["sb_tq"], q_off, sb_past, total, tiles_of(qa))
            o_b = _diff_attn(seq(qb), keys[2], keys[3], diff_lambda[e], diff_norm[e],
                             cfg["diff_tq"], cfg["diff_tk"], q_off, kv_len, lam_init, df_past, total,
                             tiles_of(qb))
            o_parts = [o_a.reshape(n, -1), o_b.reshape(n, -1)]
            w_out = w_out_even
        else:
            od = li // 2
            rows = cfg["band_rows"]
            bias_tab = _band_bias_table(rel_bias[od], rows)
            if past is None:
                assert t >= BAND_PAST
                q, k, kk, v, vv = _pre_odd(h, row(norm_mix[li]), w_in_odd[od], tm, od, n_odd,
                                           odd_state, batch=b)
                valid_lo, valid_hi = BAND_PAST, BAND_PAST + t
                bd_past = total = None
            else:
                q, k, kbf, v, vbf = _pre_odd(h, row(norm_mix[li]), w_in_odd[od], tm, od, n_odd,
                                             odd_state)
                kk, vv = seq(kbf), seq(vbf)
                bd_past = (_cache_rows(past[4], od), _cache_rows(past[5], od))
                cache_rows = bd_past[0].shape[1]
                assert cache_rows == BAND_PAST
                valid_lo, valid_hi = 0, cache_rows + t
                total = BAND_WINDOW
            odd_state = (k, v)
            o = _band_attn(seq(q), kk, vv, bias_tab, rows, cfg["band_sub"], valid_lo, valid_hi,
                           bd_past, total, tiles_of(q))
            o_parts = [o.reshape(n, -1)]
            w_out = w_out_odd
        gfin = row(norm_final) if li == depth - 1 else None
        h = _post(h, o_parts, p.reshape(depth, n, -1), li, w_out, li // 2, row(norm_ffn[li]), w_gate,
                  w_up, w_down, row(norm_ple[li]), w_ple_gate, w_ple, gfin, tm)
    y = h.reshape(b, t, d)
    rows = lambda a, shp: a.reshape((a.shape[0], b, -1) + shp)

    def cols(a, shp):
        a = a.reshape(a.shape[:2] + shp + a.shape[3:])
        return jnp.moveaxis(a, -1, 2)

    heads = cols if past is None else rows
    sb_k, sb_v, df_k, df_v = even_state
    bd_k, bd_v = odd_state
    state = (heads(sb_k, (H_SB, HEAD_DIM)), heads(sb_v, (H_SB, HEAD_DIM)),
             heads(df_k, (H_DIFF, 2, HEAD_DIM)), rows(df_v, (H_DIFF, 2 * HEAD_DIM)),
             heads(bd_k, (H_BAND, HEAD_DIM)), heads(bd_v, (H_BAND, HEAD_DIM)))
    return y, state


def kernel(x_prompt, x_sample, cache_sb_k, cache_sb_v, cache_diff_k, cache_diff_v, cache_band_k, cache_band_v, p_prompt, p_sample, norm_mix, w_in_even, w_out_even, diff_lambda, diff_norm, w_in_odd, w_out_odd, rel_bias, norm_ffn, w_gate, w_up, w_down, norm_ple, w_ple_gate, w_ple, norm_final):
    bf = _to_bf16
    weights = (norm_mix, bf(w_in_even), bf(w_out_even), diff_lambda, diff_norm, bf(w_in_odd),
               bf(w_out_odd), rel_bias, norm_ffn, bf(w_gate), bf(w_up), bf(w_down), norm_ple,
               bf(w_ple_gate), bf(w_ple), norm_final)
    t_p = x_prompt.shape[1]
    t_s = x_sample.shape[1]
    past_len = cache_sb_k.shape[2]
    diff_tk = 512
    cfg_p = dict(tm=512, sb_tq=256, diff_tq=min(1024, t_p), diff_tk=diff_tk, band_rows=LANES,
                 band_sub=min(8, t_p // LANES), whole_width=False)
    keys_total = -(-(past_len + t_s) // diff_tk) * diff_tk
    cfg_s = dict(tm=min(512, x_sample.shape[0] * t_s), sb_tq=t_s, diff_tq=t_s, diff_tk=diff_tk, band_rows=t_s, band_sub=1,
                 keys_total=keys_total, whole_width=True)
    y_p, st_p = _trunk(x_prompt, p_prompt, None, weights, cfg_p)
    past = (cache_sb_k, cache_sb_v, cache_diff_k, cache_diff_v, cache_band_k, cache_band_v)
    y_s, st_s = _trunk(x_sample, p_sample, past, weights, cfg_s)
    return (y_p, y_s) + tuple(st_p) + tuple(st_s)
```

```python
import functools
import math

import jax
import jax.numpy as jnp
from jax import lax
from jax.experimental import pallas as pl
from jax.experimental.pallas import tpu as pltpu

CHUNK = 64
HEAD_DIM = 64
H_SB = 8
H_DIFF = 4
H_BAND = 16
BAND_PAST = 8 * CHUNK
REL_CLIP = 128
ROPE_THETA = 10000.0
EPS = 1e-6
NEG = -1e30
SCALE = HEAD_DIM ** -0.5
LOG2E = 1.4426950408889634
QSCALE = SCALE * LOG2E

LANES = 128
VMEM_LIMIT = 56 * 1024 * 1024

SB_KEY_BLOCK = 2 * LANES
SB_EXIT_LOG2 = -160.0
BAND_WINDOW = BAND_PAST + LANES
BAND_UNROLL = 8

F32 = jnp.float32
BF16 = jnp.bfloat16


def _rms(x, g):
    return x * lax.rsqrt(jnp.mean(x * x, axis=-1, keepdims=True) + EPS) * g


def _sigmoid(x):
    return 1.0 / (1.0 + jnp.exp(-x))


def _dot(a, b):
    return jnp.dot(a, b, preferred_element_type=F32)


def _dot_nt(a, b):
    return lax.dot_general(a, b, (((1,), (1,)), ((), ())), preferred_element_type=F32)


def _params(n_axes):
    return pltpu.CompilerParams(dimension_semantics=("arbitrary",) * n_axes,
                                vmem_limit_bytes=VMEM_LIMIT)


def _rope(x, cos, sin_signed, first_half):
    outs = []
    for j in range(x.shape[1] // LANES):
        xj = x[:, j * LANES:(j + 1) * LANES]
        partner = jnp.where(first_half, pltpu.roll(xj, LANES - HEAD_DIM // 2, 1),
                            pltpu.roll(xj, HEAD_DIM // 2, 1))
        outs.append(xj * cos + partner * sin_signed)
    return jnp.concatenate(outs, axis=1)


def _pre_even_kernel(h_ref, g_ref, w_ref, cos_ref, sin_ref, *rest, transposed):
    flip = (lambda a: a.T) if transposed else (lambda a: a)
    (qa_ref, ka_ref, kab_ref, va_ref, vab_ref, qb_ref, kb_ref, kbb_ref, vb_ref, vbb_ref) = rest[-10:]
    hn = _rms(h_ref[...], g_ref[...]).astype(BF16)
    width = qa_ref.shape[1]

    def proj(c):
        return _dot(hn, w_ref[:, c * width:(c + 1) * width])

    cos = cos_ref[...]
    sin_signed = sin_ref[...]
    lane = lax.broadcasted_iota(jnp.int32, cos.shape, 1)
    first_half = (lane % HEAD_DIM) < (HEAD_DIM // 2)

    qa_ref[...] = (proj(0) * QSCALE).astype(BF16)
    ka = proj(1)
    ka_ref[...] = flip(ka)
    kab_ref[...] = ka.astype(BF16)
    va = proj(2)
    va_ref[...] = flip(va)
    vab_ref[...] = va.astype(BF16)
    qb_ref[...] = (_rope(proj(3), cos, sin_signed, first_half) * QSCALE).astype(BF16)
    kb = _rope(proj(4), cos, sin_signed, first_half)
    kb_ref[...] = flip(kb)
    kbb_ref[...] = kb.astype(BF16)
    vb = proj(5)
    for hd in range(H_DIFF):
        vb_ref[pl.ds(hd, vb.shape[0], stride=H_DIFF), :] = vb[:, _lane_tile(hd)]
    vbb_ref[...] = vb.astype(BF16)


def _pre_odd_kernel(h_ref, g_ref, w_ref, *rest, padded):
    q_ref, k_ref, kb_ref, v_ref, vb_ref = rest[-5:]
    width = q_ref.shape[1]

    def project(keep_f32):
        hn = _rms(h_ref[...], g_ref[...]).astype(BF16)
        q_ref[...] = (_dot(hn, w_ref[:, :width]) * QSCALE).astype(BF16)
        k = _dot(hn, w_ref[:, width:2 * width])
        kb_ref[...] = k.astype(BF16)
        v = _dot(hn, w_ref[:, 2 * width:])
        vb_ref[...] = v.astype(BF16)
        keep_f32(k, v)

    def store_f32(k, v):
        k_ref[...] = k.T if padded else k
        v_ref[...] = v.T if padded else v

    if not padded:
        project(store_f32)
        return

    j = pl.program_id(1)

    @pl.when(j == 0)
    def _():
        kb_ref[...] = jnp.zeros(kb_ref.shape, BF16)
        vb_ref[...] = jnp.zeros(vb_ref.shape, BF16)

    @pl.when(j > 0)
    def _():
        project(lambda k, v: pl.when(j == pl.num_programs(1) - 1)(lambda: store_f32(k, v)))


def _stack_alias(prev, n_fixed_inputs, out_positions):
    if prev is None:
        return [], [], {}
    specs = [pl.BlockSpec(memory_space=pl.ANY)] * len(prev)
    aliases = {n_fixed_inputs + i: o for i, o in enumerate(out_positions)}
    return list(prev), specs, aliases


def _pre_even(h, g, w, cos, sin_signed, tm, slot, n_slots, prev, batch=None):
    n, d = h.shape
    width = w.shape[1] // 6
    n_pos_blocks = cos.shape[0] // tm
    tok = lambda wd: pl.BlockSpec((tm, wd), lambda i: (i, 0))
    const = lambda a: pl.BlockSpec(a.shape, lambda i: (0, 0))
    pos = pl.BlockSpec((tm, LANES), lambda i: (i % n_pos_blocks, 0))
    stk = pl.BlockSpec((None, tm, width), lambda i: (slot, i, 0))
    f32s = jax.ShapeDtypeStruct((n_slots, n, width), F32)
    bf16o = jax.ShapeDtypeStruct((n, width), BF16)
    t, s = tok(width), stk
    st, f32t = s, f32s
    assert width == H_DIFF * LANES
    sv = pl.BlockSpec((None, tm * H_DIFF, LANES), lambda i: (slot, i, 0))
    f32v = jax.ShapeDtypeStruct((n_slots, n * H_DIFF, LANES), F32)
    if batch is not None:
        tiles = n // batch // tm
        st = pl.BlockSpec((None, None, width, tm), lambda i: (slot, i // tiles, 0, i % tiles))
        f32t = jax.ShapeDtypeStruct((n_slots, batch, width, n // batch), F32)
    prev_args, prev_specs, aliases = _stack_alias(prev, 5, (1, 3, 6, 8))
    return pl.pallas_call(
        functools.partial(_pre_even_kernel, transposed=batch is not None),
        grid=(n // tm,),
        in_specs=[tok(d), const(g), const(w), pos, pos] + prev_specs,
        out_specs=[t, st, t, st, t, t, st, t, sv, t],
        out_shape=[bf16o, f32t, bf16o, f32t, bf16o, bf16o, f32t, bf16o, f32v, bf16o],
        input_output_aliases=aliases,
        compiler_params=_params(1),
    )(h, g, w, cos, sin_signed, *prev_args)


def _pre_odd(h, g, w, tm, slot, n_slots, prev, batch=None):
    n, d = h.shape
    width = w.shape[1] // 3
    padded = batch is not None
    bf16o = jax.ShapeDtypeStruct((n, width), BF16)
    if padded:
        assert tm == BAND_PAST
        tiles = n // batch // tm
        grid = (batch, tiles + 1)
        tile = lambda bi, j: bi * tiles + jnp.maximum(j - 1, 0)
        tok = lambda wd: pl.BlockSpec((tm, wd), lambda bi, j: (tile(bi, j), 0))
        const = lambda a: pl.BlockSpec(a.shape, lambda bi, j: (0, 0))
        stk = pl.BlockSpec((None, None, width, tm), lambda bi, j: (slot, bi, 0, 0))
        pad = pl.BlockSpec((None, tm, width), lambda bi, j: (bi, j, 0))
        f32s = jax.ShapeDtypeStruct((n_slots, batch, width, tm), F32)
        bf16p = jax.ShapeDtypeStruct((batch, (tiles + 1) * tm, width), BF16)
        out_specs, out_shape = [tok(width), stk, pad, stk, pad], [bf16o, f32s, bf16p, f32s, bf16p]
    else:
        grid = (n // tm,)
        tok = lambda wd: pl.BlockSpec((tm, wd), lambda i: (i, 0))
        const = lambda a: pl.BlockSpec(a.shape, lambda i: (0, 0))
        stk = pl.BlockSpec((None, tm, width), lambda i: (slot, i, 0))
        f32s = jax.ShapeDtypeStruct((n_slots, n, width), F32)
        out_specs, out_shape = [tok(width), stk, tok(width), stk, tok(width)], [bf16o, f32s, bf16o, f32s, bf16o]
    prev_args, prev_specs, aliases = _stack_alias(prev, 3, (1, 3))
    return pl.pallas_call(
        functools.partial(_pre_odd_kernel, padded=padded),
        grid=grid,
        in_specs=[tok(d), const(g), const(w)] + prev_specs,
        out_specs=out_specs,
        out_shape=out_shape,
        input_output_aliases=aliases,
        compiler_params=_params(len(grid)),
    )(h, g, w, *prev_args)


def _cast_kernel(x_ref, o_ref):
    o_ref[...] = x_ref[...].astype(o_ref.dtype)


def _to_bf16(w):
    layers, rows, cols = w.shape
    tr = rows // 4
    assert rows % 4 == 0 and tr % 16 == 0
    spec = pl.BlockSpec((None, tr, cols), lambda i, j: (i, j, 0))
    return pl.pallas_call(
        _cast_kernel,
        grid=(layers, rows // tr),
        in_specs=[spec],
        out_specs=spec,
        out_shape=jax.ShapeDtypeStruct(w.shape, BF16),
        compiler_params=_params(2),
    )(w)


def _post_kernel(*refs, n_o, ffn_chunk, final):
    h_ref = refs[0]
    o_refs = refs[1:1 + n_o]
    (p_ref, wout_ref, gffn_ref, wg_ref, wu_ref, wd_ref, gple_ref, wpg_ref, wp_ref) = refs[1 + n_o:10 + n_o]
    gfin_ref = refs[10 + n_o] if final else None
    out_ref = refs[-1]

    h = h_ref[...]
    off = 0
    for o_ref in o_refs:
        wd = o_ref.shape[1]
        h = h + _dot(o_ref[...], wout_ref[off:off + wd, :])
        off += wd

    hn = _rms(h, gffn_ref[...]).astype(BF16)
    hidden = wg_ref.shape[1]
    for c in range(hidden // ffn_chunk):
        cols = slice(c * ffn_chunk, (c + 1) * ffn_chunk)
        gt = _dot(hn, wg_ref[:, cols])
        up = _dot(hn, wu_ref[:, cols])
        act = (gt * _sigmoid(gt) * up).astype(BF16)
        h = h + _dot(act, wd_ref[cols, :])

    gate = _sigmoid(_dot(_rms(h, gple_ref[...]).astype(BF16), wpg_ref[...]))
    h = h + _dot(p_ref[...].astype(BF16), wp_ref[...]) * gate
    if final:
        h = _rms(h, gfin_ref[...])
    out_ref[...] = h


def _post(h, o_parts, p, layer, wout, wout_layer, gffn, wg, wu, wd, gple, wpg, wp, gfin, tm):
    n, d = h.shape
    final = gfin is not None
    hidden = wg.shape[2]
    ffn_chunk = hidden
    tok = lambda a: pl.BlockSpec((tm, a.shape[1]), lambda i: (i, 0))
    row = lambda a: pl.BlockSpec(a.shape, lambda i: (0, 0), pipeline_mode=pl.Buffered(1))
    mat = lambda a, li: pl.BlockSpec((None,) + a.shape[1:], lambda i: (li, 0, 0),
                                     pipeline_mode=pl.Buffered(1))
    consts = [wout, gffn, wg, wu, wd, gple, wpg, wp] + ([gfin] if final else [])
    const_specs = ([mat(wout, wout_layer), row(gffn)] + [mat(a, layer) for a in (wg, wu, wd)]
                   + [row(gple), mat(wpg, layer), mat(wp, layer)] + ([row(gfin)] if final else []))
    p_spec = pl.BlockSpec((None, tm, p.shape[2]), lambda i: (layer, i, 0))
    return pl.pallas_call(
        functools.partial(_post_kernel, n_o=len(o_parts), ffn_chunk=ffn_chunk, final=final),
        grid=(n // tm,),
        in_specs=[tok(h)] + [tok(o) for o in o_parts] + [p_spec] + const_specs,
        out_specs=tok(h),
        out_shape=jax.ShapeDtypeStruct((n, d), F32),
        compiler_params=_params(1),
    )(h, *o_parts, p, *consts)


def _lane_tile(i):
    return slice(i * LANES, (i + 1) * LANES)


def _sb_kernel(q_ref, k_ref, v_ref, tt_ref, o_ref, acc_ref, run_ref, *, tq, q_off):
    qi = pl.program_id(2)
    n_pairs = q_ref.shape[1] // LANES
    lane = lax.broadcasted_iota(jnp.int32, (tq, LANES), 1)
    row_pos = q_off + qi * tq + lax.broadcasted_iota(jnp.int32, (tq, LANES), 0)
    first = lane < HEAD_DIM
    blk0 = (q_off + qi * tq) // SB_KEY_BLOCK
    tt = tt_ref[...]
    heads = range(2 * n_pairs)
    q_heads = []
    for pair in range(n_pairs):
        q2 = q_ref[:, _lane_tile(pair)]
        zero = jnp.zeros_like(q2)
        q_heads += [jnp.where(first, q2, zero), jnp.where(first, zero, q2)]
    acc_ref[...] = jnp.zeros(acc_ref.shape, F32)
    run_ref[...] = jnp.zeros(run_ref.shape, F32)

    def span(first_block, n_blocks, own_block):
        n_halves = 2 * n_blocks
        width = n_halves * LANES
        start = first_block * SB_KEY_BLOCK
        if not isinstance(start, int):
            start = pl.multiple_of(start, SB_KEY_BLOCK)
        kblk = [k_ref[pl.ds(start, width), _lane_tile(pair)] for pair in range(n_pairs)]
        vblk = [v_ref[pl.ds(start, width), _lane_tile(pair)] for pair in range(n_pairs)]
        s = [_dot_nt(q_heads[h], kblk[h // 2]) for h in heads]
        soft = [jnp.log(1.0 + jnp.exp2(-jnp.abs(s[h]))) * LOG2E for h in heads]
        log_beta = [jnp.minimum(s[h], 0.0) - soft[h] for h in heads]
        log_keep = [log_beta[h] - s[h] for h in heads]
        masks = {}
        if own_block:
            for i in (n_halves - 2, n_halves - 1):
                masks[i] = (first_block * SB_KEY_BLOCK + i * LANES + lane) < row_pos
        cs = []
        for h in heads:
            per_half = []
            for i in range(n_halves):
                keep = log_keep[h][:, _lane_tile(i)]
                if i in masks:
                    keep = jnp.where(masks[i], keep, 0.0)
                hi = keep.astype(BF16)
                lo = (keep - hi.astype(F32)).astype(BF16)
                per_half.append(_dot(jnp.concatenate([hi, lo], axis=1), tt))
            cs.append(per_half)
        for h in heads:
            run = run_ref[h]
            w = [None] * n_halves
            for i in reversed(range(n_halves)):
                w_i = jnp.exp2(log_beta[h][:, _lane_tile(i)] + cs[h][i][:, :LANES] + run)
                if i in masks:
                    w_i = jnp.where(masks[i], w_i, 0.0)
                w[i] = w_i.astype(BF16)
                run = run + cs[h][i][:, LANES:]
            acc_ref[h] += _dot(jnp.concatenate(w, axis=1), vblk[h // 2])
            run_ref[h] = run

    @pl.when(blk0 == 0)
    def _():
        span(0, 1, True)

    @pl.when(blk0 > 0)
    def _():
        span(blk0 - 1, 2, True)

    def cond(carry):
        i, live = carry
        return jnp.logical_and(i < blk0 - 1, live > SB_EXIT_LOG2)

    def live_mass():
        return jnp.max(functools.reduce(jnp.maximum, [run_ref[h] for h in heads]))

    def body(carry):
        i, _ = carry
        span(blk0 - 2 - i, 1, False)
        return i + 1, live_mass()

    lax.while_loop(cond, body, (jnp.int32(0), live_mass()))
    for pair in range(n_pairs):
        o_ref[:, _lane_tile(pair)] = jnp.where(first, acc_ref[2 * pair],
                                               acc_ref[2 * pair + 1]).astype(o_ref.dtype)


def _suffix_matrix():
    r = lax.broadcasted_iota(jnp.int32, (2 * LANES, 2 * LANES), 0) % LANES
    c = lax.broadcasted_iota(jnp.int32, (2 * LANES, 2 * LANES), 1)
    return jnp.where((c >= LANES) | (r > c), 1.0, 0.0).astype(BF16)


def _keys_from_past(kernel_fn, n_before):
    def wrapped(*refs):
        past_new = refs[n_before:n_before + 4]
        k_buf, v_buf = refs[-2:]
        for past, new, buf in ((past_new[0], past_new[2], k_buf), (past_new[1], past_new[3], v_buf)):
            p, t = past.shape[0], new.shape[0]
            buf[0:p, :] = past[...]
            buf[p:p + t, :] = new[...]
            buf[p + t:, :] = jnp.zeros((buf.shape[0] - p - t, buf.shape[1]), buf.dtype)
        kernel_fn(*refs[:n_before], k_buf, v_buf, *refs[n_before + 4:-2])
    return wrapped


def _key_operands(kernel_fn, n_before, k, v, past, total, tiles=1):
    index = lambda bi, hp, qi: (bi, 0, hp)
    spec = lambda a: pl.BlockSpec((None, a.shape[1], tiles * LANES), index)
    if past is None:
        return kernel_fn, k.shape[1], [k, v], [spec(k), spec(v)], []
    buf = pltpu.VMEM((total, tiles * LANES), BF16)
    operands = [past[0], past[1], k, v]
    return _keys_from_past(kernel_fn, n_before), total, operands, [spec(a) for a in operands], [buf, buf]


def _sb_attn(q, k, v, tq, q_off, past=None, total=None, tiles=1):
    b, t, w = q.shape
    body, keys, key_args, key_specs, key_scratch = _key_operands(
        functools.partial(_sb_kernel, tq=tq, q_off=q_off), 1, k, v, past, total, tiles)
    assert q_off % SB_KEY_BLOCK == 0 and (tq == SB_KEY_BLOCK or (t == tq and tq < SB_KEY_BLOCK))
    assert -(-(q_off + t) // SB_KEY_BLOCK) * SB_KEY_BLOCK <= keys and w % (tiles * LANES) == 0
    tt = _suffix_matrix()
    qspec = pl.BlockSpec((None, tq, tiles * LANES), lambda bi, hp, qi: (bi, qi, hp))
    state = pltpu.VMEM((2 * tiles, tq, LANES), F32)
    return pl.pallas_call(
        body,
        grid=(b, w // (tiles * LANES), t // tq),
        in_specs=[qspec] + key_specs + [pl.BlockSpec(tt.shape, lambda bi, hp, qi: (0, 0))],
        out_specs=qspec,
        out_shape=jax.ShapeDtypeStruct((b, t, w), BF16),
        scratch_shapes=[state, state] + key_scratch,
        compiler_params=_params(3),
    )(q, *key_args, tt)


def _diff_kernel(lam_ref, gain_ref, q_ref, k_ref, v_ref, o_ref, m_ref, l_ref, acc_ref,
                 *, tq, tk, q_off, kv_len, n_diag, lam_init):
    qi = pl.program_id(2)
    n_heads = q_ref.shape[1] // LANES
    lane = lax.broadcasted_iota(jnp.int32, (tq, LANES), 1)
    first = lane < HEAD_DIM
    q_maps = []
    for hd in range(n_heads):
        q2 = q_ref[:, _lane_tile(hd)]
        zero = jnp.zeros_like(q2)
        q_maps += [jnp.where(first, q2, zero), jnp.where(first, zero, q2)]
    diag_rows = min(tq, tk)
    row_chunk = lax.broadcasted_iota(jnp.int32, (diag_rows, tk), 0) // CHUNK
    col = lax.broadcasted_iota(jnp.int32, (diag_rows, tk), 1)
    diag_mask = (col // CHUNK) <= row_chunk
    blk0 = (q_off + qi * tq) // tk
    ones = jnp.ones((tk, LANES), BF16)

    m_ref[...] = jnp.full(m_ref.shape, NEG, F32)
    l_ref[...] = jnp.zeros(l_ref.shape, F32)
    acc_ref[...] = jnp.zeros(acc_ref.shape, F32)

    def block(kb, masked, rows=slice(0, tq)):
        start = pl.multiple_of(kb * tk, tk)
        kblk = [k_ref[pl.ds(start, tk), _lane_tile(hd)] for hd in range(n_heads)]
        vext = [jnp.concatenate([v_ref[pl.ds(start, tk), _lane_tile(hd)], ones], axis=1)
                for hd in range(n_heads)]
        if masked:
            mask = diag_mask
            if kv_len % tk:
                mask = mask & (kb * tk + col < kv_len)
        maps = range(2 * n_heads)
        slabs = range(tk // LANES)
        s = [_dot_nt(q_maps[mp][rows], kblk[mp // 2]) for mp in maps]
        if masked:
            s = [jnp.where(mask, s[mp], NEG) for mp in maps]
        m_old = [m_ref[mp, rows, :] for mp in maps]
        m_new = []
        for mp in maps:
            mx = s[mp][:, :LANES]
            for j in slabs[1:]:
                mx = jnp.maximum(mx, s[mp][:, j * LANES:(j + 1) * LANES])
            m_new.append(jnp.maximum(m_old[mp], jnp.max(mx, axis=-1, keepdims=True)))
        alpha = [jnp.exp2(m_old[mp] - m_new[mp]) for mp in maps]
        p = [jnp.concatenate([jnp.exp2(s[mp][:, j * LANES:(j + 1) * LANES] - m_new[mp]).astype(BF16)
                              for j in slabs], axis=1) for mp in maps]
        pv = [_dot(p[mp], vext[mp // 2]) for mp in maps]
        for mp in maps:
            acc_ref[mp, rows, :] = acc_ref[mp, rows, :] * alpha[mp] + pv[mp][:, :LANES]
            l_ref[mp, rows, :] = l_ref[mp, rows, :] * alpha[mp] + pv[mp][:, LANES:]
            m_ref[mp, rows, :] = m_new[mp]

    for r in range(n_diag):
        rows = slice(r * diag_rows, (r + 1) * diag_rows)
        block(blk0 + r, True, rows)
        for j in range(r):
            block(blk0 + j, False, rows)

    def body(i, carry):
        block(2 * i, False)
        block(2 * i + 1, False)
        return carry

    lax.fori_loop(0, blk0 // 2, body, 0)

    @pl.when(blk0 % 2 == 1)
    def _():
        block(blk0 - 1, False)

    lp = lam_ref[...]
    lam = (jnp.exp(jnp.sum(lp[0:1] * lp[1:2], axis=-1, keepdims=True))
           - jnp.exp(jnp.sum(lp[2:3] * lp[3:4], axis=-1, keepdims=True)) + lam_init)
    for hd in range(n_heads):
        o = acc_ref[2 * hd] / l_ref[2 * hd] - lam * (acc_ref[2 * hd + 1] / l_ref[2 * hd + 1])
        o = o * lax.rsqrt(jnp.mean(o * o, axis=-1, keepdims=True) + EPS)
        o_ref[:, _lane_tile(hd)] = (o * gain_ref[hd] * (1.0 - lam_init)).astype(o_ref.dtype)


def _diff_attn(q, k, v, lam_params, gain, tq, tk, q_off, kv_len, lam_init, past=None, total=None,
               tiles=1):
    b, t, w = q.shape
    n_diag = max(1, tq // tk)
    body, keys, key_args, key_specs, key_scratch = _key_operands(
        functools.partial(_diff_kernel, tq=tq, tk=tk, q_off=q_off, kv_len=kv_len, n_diag=n_diag,
                          lam_init=lam_init), 3, k, v, past, total, tiles)
    assert q_off % tk == 0 and (tq % tk == 0 or t == tq) and keys % tk == 0 and CHUNK % 8 == 0
    assert (q_off + t + tk - 1) // tk * tk <= keys and w % (tiles * LANES) == 0
    gain3 = gain.reshape(gain.shape[0], 1, gain.shape[1])
    qspec = pl.BlockSpec((None, tq, tiles * LANES), lambda bi, hd, qi: (bi, qi, hd))
    return pl.pallas_call(
        body,
        grid=(b, w // (tiles * LANES), t // tq),
        in_specs=[pl.BlockSpec(lam_params.shape, lambda bi, hd, qi: (0, 0)),
                  pl.BlockSpec((tiles, 1, LANES), lambda bi, hd, qi: (hd, 0, 0)),
                  qspec] + key_specs,
        out_specs=qspec,
        out_shape=jax.ShapeDtypeStruct((b, t, w), BF16),
        scratch_shapes=[pltpu.VMEM((2 * tiles, tq, LANES), F32)] * 3 + key_scratch,
        compiler_params=_params(3),
    )(lam_params, gain3, q, *key_args)


def _band_kernel(q_ref, k_ref, v_ref, bias_ref, o_ref, *, rows, n_sub, valid_lo, valid_hi):
    qi = pl.program_id(2)
    n_pairs = q_ref.shape[1] // LANES
    lane = lax.broadcasted_iota(jnp.int32, (rows, LANES), 1)
    first = lane < HEAD_DIM
    ones = jnp.ones((BAND_WINDOW, LANES), BF16)
    kcol = lax.broadcasted_iota(jnp.int32, (1, BAND_WINDOW), 1)

    def sub(j, carry):
        r0 = pl.multiple_of(j * rows, rows)
        ws = pl.multiple_of((qi * n_sub + j) * rows, rows)
        krow = ws + kcol
        penalty = jnp.where((krow >= valid_lo) & (krow < valid_hi), 0.0, NEG)
        for pair in range(n_pairs):
            q2 = q_ref[pl.ds(r0, rows), _lane_tile(pair)]
            zero = jnp.zeros_like(q2)
            kw = k_ref[pl.ds(ws, BAND_WINDOW), _lane_tile(pair)]
            vext = jnp.concatenate([v_ref[pl.ds(ws, BAND_WINDOW), _lane_tile(pair)], ones], axis=1)
            outs = []
            for head in range(2):
                qh = jnp.where(first if head == 0 else jnp.logical_not(first), q2, zero)
                s = _dot_nt(qh, kw) + bias_ref[2 * pair + head] + penalty
                p = jnp.exp2(s - jnp.max(s, axis=-1, keepdims=True))
                pv = _dot(p.astype(BF16), vext)
                outs.append(pv[:, :LANES] / pv[:, LANES:])
            o_ref[pl.ds(r0, rows), _lane_tile(pair)] = jnp.where(first, outs[0], outs[1]).astype(o_ref.dtype)
        return carry

    lax.fori_loop(0, n_sub, sub, 0, unroll=BAND_UNROLL if n_sub % BAND_UNROLL == 0 else 1)


def _band_bias_table(rel_bias, rows):
    heads = rel_bias.shape[0]
    span = BAND_WINDOW + rows - 1
    n_far = BAND_PAST - REL_CLIP + rows
    rb = rel_bias.astype(F32) * LOG2E
    near = rb[:, 2 * REL_CLIP - 1:0:-1]
    assert n_far + near.shape[1] == span
    e = jnp.concatenate([jnp.broadcast_to(rb[:, -1:], (heads, n_far)), near,
                         jnp.zeros((heads, 1), F32)], axis=1)
    skew = jnp.broadcast_to(e[:, None, :], (heads, rows, span + 1)).reshape(heads, rows * (span + 1))
    skew = skew[:, :rows * span].reshape(heads, rows, span)
    bias = skew[:, :, rows - 1:]
    i = jnp.arange(rows, dtype=jnp.int32)[:, None]
    rel_key = jnp.arange(BAND_WINDOW, dtype=jnp.int32)[None, :] - BAND_PAST
    kc = jnp.floor_divide(rel_key, CHUNK)
    qc = i // CHUNK
    in_band = (kc <= qc) & (kc >= qc - BAND_PAST // CHUNK)
    return jnp.where(in_band[None], bias, NEG)


def _band_attn(q, k, v, bias_tab, rows, n_sub, valid_lo, valid_hi, past=None, total=None, tiles=1):
    b, t, w = q.shape
    tq = rows * n_sub
    body, keys, key_args, key_specs, key_scratch = _key_operands(
        functools.partial(_band_kernel, rows=rows, n_sub=n_sub, valid_lo=valid_lo, valid_hi=valid_hi),
        1, k, v, past, total, tiles)
    assert t % tq == 0 and (t - rows) + BAND_WINDOW <= keys and w % (tiles * LANES) == 0
    qspec = pl.BlockSpec((None, tq, tiles * LANES), lambda bi, hp, qi: (bi, qi, hp))
    return pl.pallas_call(
        body,
        grid=(b, w // (tiles * LANES), t // tq),
        in_specs=[qspec] + key_specs +
                 [pl.BlockSpec((2 * tiles, rows, BAND_WINDOW), lambda bi, hp, qi: (hp, 0, 0))],
        out_specs=qspec,
        out_shape=jax.ShapeDtypeStruct((b, t, w), BF16),
        scratch_shapes=key_scratch,
        compiler_params=_params(3),
    )(q, *key_args, bias_tab)


def _rope_tables(pos, reps):
    half = HEAD_DIM // 2
    inv = ROPE_THETA ** (-jnp.arange(half, dtype=F32) / half)
    ang = pos.astype(F32)[:, None] * inv[None, :]
    cos, sin = jnp.cos(ang), jnp.sin(ang)
    cos = jnp.tile(cos, (reps, LANES // half))
    sin_signed = jnp.tile(jnp.concatenate([-sin, sin], axis=1), (reps, LANES // HEAD_DIM))
    return cos, sin_signed


def _cache_rows_kernel(x_ref, o_ref):
    o_ref[...] = x_ref[...].T.astype(o_ref.dtype)


def _cache_rows_by_head_kernel(x_ref, o_ref, *, heads):
    for hd in range(heads):
        o_ref[:, _lane_tile(hd)] = x_ref[pl.ds(hd, o_ref.shape[0], stride=heads), :].astype(o_ref.dtype)


def _cache_rows(cache, layer):
    n_layers, b, p = cache.shape[:3]
    if cache.shape[-1] == LANES:
        heads = cache.shape[3]
        return pl.pallas_call(
            functools.partial(_cache_rows_by_head_kernel, heads=heads),
            grid=(b,),
            in_specs=[pl.BlockSpec((None, None, p * heads, LANES), lambda i: (layer, i, 0, 0))],
            out_specs=pl.BlockSpec((None, p, heads * LANES), lambda i: (i, 0, 0)),
            out_shape=jax.ShapeDtypeStruct((b, p, heads * LANES), BF16),
            compiler_params=_params(1),
        )(cache.reshape(n_layers, b, p * heads, LANES))
    perm = (0, 1) + tuple(range(3, cache.ndim)) + (2,)
    cols = jnp.transpose(cache, perm).reshape(n_layers, b, -1, p)
    w = cols.shape[2]
    return pl.pallas_call(
        _cache_rows_kernel,
        grid=(b,),
        in_specs=[pl.BlockSpec((None, None, w, p), lambda i: (layer, i, 0, 0))],
        out_specs=pl.BlockSpec((None, p, w), lambda i: (i, 0, 0)),
        out_shape=jax.ShapeDtypeStruct((b, p, w), BF16),
        compiler_params=_params(1),
    )(cols)


def _trunk(x, p, past, weights, cfg):
    (norm_mix, w_in_even, w_out_even, diff_lambda, diff_norm, w_in_odd, w_out_odd, rel_bias,
     norm_ffn, w_gate, w_up, w_down, norm_ple, w_ple_gate, w_ple, norm_final) = weights
    b, t, d = x.shape
    n = b * t
    depth = norm_mix.shape[0]
    tm = cfg["tm"]
    q_off = 0 if past is None else past[0].shape[2]
    pos = q_off + jnp.arange(t, dtype=jnp.int32)
    if past is None:
        cos, sin_signed = _rope_tables(pos, 1)
    else:
        cos, sin_signed = _rope_tables(pos, tm // t)
    row = lambda a: a.reshape(1, -1)
    seq = lambda a: a.reshape(b, t, a.shape[-1])
    tiles_of = lambda a: a.shape[-1] // LANES if cfg["whole_width"] else 1

    h = x.reshape(n, d)
    n_even, n_odd = w_in_even.shape[0], w_in_odd.shape[0]
    even_state = None
    odd_state = None
    for li in range(depth):
        if li % 2 == 0:
            e = li // 2
            qa, ka, kab, va, vab, qb, kb, kbb, vb, vbb = _pre_even(
                h, row(norm_mix[li]), w_in_even[e], cos, sin_signed, tm, e, n_even, even_state,
                batch=b if past is None else None)
            even_state = (ka, va, kb, vb)
            lam_init = 0.8 - 0.6 * math.exp(-0.3 * li)
            keys = [seq(a) for a in (kab, vab, kbb, vbb)]
            if past is None:
                sb_past = df_past = total = None
                kv_len = t
            else:
                total = cfg["keys_total"]
                sb_past = (_cache_rows(past[0], e), _cache_rows(past[1], e))
                df_past = (_cache_rows(past[2], e), _cache_rows(past[3], e))
                kv_len = q_off + t
            o_a = _sb_attn(seq(qa), keys[0], keys[1], cfg["sb_tq"], q_off, sb_past, total, tiles_of(qa))
            o_b = _diff_attn(seq(qb), keys[2], keys[3], diff_lambda[e], diff_norm[e],
                             cfg["diff_tq"], cfg["diff_tk"], q_off, kv_len, lam_init, df_past, total,
                             tiles_of(qb))
            o_parts = [o_a.reshape(n, -1), o_b.reshape(n, -1)]
            w_out = w_out_even
        else:
            od = li // 2
            rows = cfg["band_rows"]
            bias_tab = _band_bias_table(rel_bias[od], rows)
            if past is None:
                assert t >= BAND_PAST
                q, k, kk, v, vv = _pre_odd(h, row(norm_mix[li]), w_in_odd[od], tm, od, n_odd,
                                           odd_state, batch=b)
                valid_lo, valid_hi = BAND_PAST, BAND_PAST + t
                bd_past = total = None
            else:
                q, k, kbf, v, vbf = _pre_odd(h, row(norm_mix[li]), w_in_odd[od], tm, od, n_odd,
                                             odd_state)
                kk, vv = seq(kbf), seq(vbf)
                bd_past = (_cache_rows(past[4], od), _cache_rows(past[5], od))
                cache_rows = bd_past[0].shape[1]
                assert cache_rows == BAND_PAST
                valid_lo, valid_hi = 0, cache_rows + t
                total = BAND_WINDOW
            odd_state = (k, v)
            o = _band_attn(seq(q), kk, vv, bias_tab, rows, cfg["band_sub"], valid_lo, valid_hi,
                           bd_past, total, tiles_of(q))
            o_parts = [o.reshape(n, -1)]
            w_out = w_out_odd
        gfin = row(norm_final) if li == depth - 1 else None
        h = _post(h, o_parts, p.reshape(depth, n, -1), li, w_out, li // 2, row(norm_ffn[li]), w_gate,
                  w_up, w_down, row(norm_ple[li]), w_ple_gate, w_ple, gfin, tm)
    y = h.reshape(b, t, d)
    rows = lambda a, shp: a.reshape((a.shape[0], b, -1) + shp)

    def cols(a, shp):
        a = a.reshape(a.shape[:2] + shp + a.shape[3:])
        return jnp.moveaxis(a, -1, 2)

    heads = cols if past is None else rows
    sb_k, sb_v, df_k, df_v = even_state
    bd_k, bd_v = odd_state
    state = (heads(sb_k, (H_SB, HEAD_DIM)), heads(sb_v, (H_SB, HEAD_DIM)),
             heads(df_k, (H_DIFF, 2, HEAD_DIM)), rows(df_v, (H_DIFF, 2 * HEAD_DIM)),
             heads(bd_k, (H_BAND, HEAD_DIM)), heads(bd_v, (H_BAND, HEAD_DIM)))
    return y, state


def kernel(x_prompt, x_sample, cache_sb_k, cache_sb_v, cache_diff_k, cache_diff_v, cache_band_k, cache_band_v, p_prompt, p_sample, norm_mix, w_in_even, w_out_even, diff_lambda, diff_norm, w_in_odd, w_out_odd, rel_bias, norm_ffn, w_gate, w_up, w_down, norm_ple, w_ple_gate, w_ple, norm_final):
    bf = _to_bf16
    weights = (norm_mix, bf(w_in_even), bf(w_out_even), diff_lambda, diff_norm, bf(w_in_odd),
               bf(w_out_odd), rel_bias, norm_ffn, bf(w_gate), bf(w_up), bf(w_down), norm_ple,
               bf(w_ple_gate), bf(w_ple), norm_final)
    t_p = x_prompt.shape[1]
    t_s = x_sample.shape[1]
    past_len = cache_sb_k.shape[2]
    diff_tk = 512
    cfg_p = dict(tm=512, sb_tq=256, diff_tq=min(1024, t_p), diff_tk=diff_tk, band_rows=LANES,
                 band_sub=min(8, t_p // LANES), whole_width=False)
    keys_total = -(-(past_len + t_s) // diff_tk) * diff_tk
    cfg_s = dict(tm=min(512, x_sample.shape[0] * t_s), sb_tq=t_s, diff_tq=t_s, diff_tk=diff_tk, band_rows=t_s, band_sub=1,
                 keys_total=keys_total, whole_width=True)
    y_p, st_p = _trunk(x_prompt, p_prompt, None, weights, cfg_p)
    past = (cache_sb_k, cache_sb_v, cache_diff_k, cache_diff_v, cache_band_k, cache_band_v)
    y_s, st_s = _trunk(x_sample, p_sample, past, weights, cfg_s)
    return (y_p, y_s) + tuple(st_p) + tuple(st_s)
```

```python
import functools
import math

import jax
import jax.numpy as jnp
from jax import lax
from jax.experimental import pallas as pl
from jax.experimental.pallas import tpu as pltpu

CHUNK = 64
HEAD_DIM = 64
H_SB = 8
H_DIFF = 4
H_BAND = 16
BAND_PAST = 8 * CHUNK
REL_CLIP = 128
ROPE_THETA = 10000.0
EPS = 1e-6
NEG = -1e30
SCALE = HEAD_DIM ** -0.5
LOG2E = 1.4426950408889634
QSCALE = SCALE * LOG2E

LANES = 128
VMEM_LIMIT = 56 * 1024 * 1024

SB_KEY_BLOCK = 2 * LANES
SB_EXIT_LOG2 = -160.0
BAND_WINDOW = BAND_PAST + LANES
BAND_UNROLL = 8

F32 = jnp.float32
BF16 = jnp.bfloat16


def _rms(x, g):
    return x * lax.rsqrt(jnp.mean(x * x, axis=-1, keepdims=True) + EPS) * g


def _sigmoid(x):
    return 1.0 / (1.0 + jnp.exp(-x))


def _dot(a, b):
    return jnp.dot(a, b, preferred_element_type=F32)


def _dot_nt(a, b):
    return lax.dot_general(a, b, (((1,), (1,)), ((), ())), preferred_element_type=F32)


def _params(n_axes):
    return pltpu.CompilerParams(dimension_semantics=("arbitrary",) * n_axes,
                                vmem_limit_bytes=VMEM_LIMIT)


def _rope(x, cos, sin_signed, first_half):
    outs = []
    for j in range(x.shape[1] // LANES):
        xj = x[:, j * LANES:(j + 1) * LANES]
        partner = jnp.where(first_half, pltpu.roll(xj, LANES - HEAD_DIM // 2, 1),
                            pltpu.roll(xj, HEAD_DIM // 2, 1))
        outs.append(xj * cos + partner * sin_signed)
    return jnp.concatenate(outs, axis=1)


def _pre_even_kernel(h_ref, g_ref, w_ref, cos_ref, sin_ref, *rest, transposed):
    flip = (lambda a: a.T) if transposed else (lambda a: a)
    (qa_ref, ka_ref, kab_ref, va_ref, vab_ref, qb_ref, kb_ref, kbb_ref, vb_ref, vbb_ref) = rest[-10:]
    hn = _rms(h_ref[...], g_ref[...]).astype(BF16)
    width = qa_ref.shape[1]

    def proj(c):
        return _dot(hn, w_ref[:, c * width:(c + 1) * width])

    cos = cos_ref[...]
    sin_signed = sin_ref[...]
    lane = lax.broadcasted_iota(jnp.int32, cos.shape, 1)
    first_half = (lane % HEAD_DIM) < (HEAD_DIM // 2)

    qa_ref[...] = (proj(0) * QSCALE).astype(BF16)
    ka = proj(1)
    ka_ref[...] = flip(ka)
    kab_ref[...] = ka.astype(BF16)
    va = proj(2)
    va_ref[...] = flip(va)
    vab_ref[...] = va.astype(BF16)
    qb_ref[...] = (_rope(proj(3), cos, sin_signed, first_half) * QSCALE).astype(BF16)
    kb = _rope(proj(4), cos, sin_signed, first_half)
    kb_ref[...] = flip(kb)
    kbb_ref[...] = kb.astype(BF16)
    vb = proj(5)
    for hd in range(H_DIFF):
        vb_ref[pl.ds(hd, vb.shape[0], stride=H_DIFF), :] = vb[:, _lane_tile(hd)]
    vbb_ref[...] = vb.astype(BF16)


def _pre_odd_kernel(h_ref, g_ref, w_ref, *rest, padded):
    q_ref, k_ref, kb_ref, v_ref, vb_ref = rest[-5:]
    width = q_ref.shape[1]

    def project(keep_f32):
        hn = _rms(h_ref[...], g_ref[...]).astype(BF16)
        q_ref[...] = (_dot(hn, w_ref[:, :width]) * QSCALE).astype(BF16)
        k = _dot(hn, w_ref[:, width:2 * width])
        kb_ref[...] = k.astype(BF16)
        v = _dot(hn, w_ref[:, 2 * width:])
        vb_ref[...] = v.astype(BF16)
        keep_f32(k, v)

    def store_f32(k, v):
        k_ref[...] = k.T if padded else k
        v_ref[...] = v.T if padded else v

    if not padded:
        project(store_f32)
        return

    j = pl.program_id(1)

    @pl.when(j == 0)
    def _():
        kb_ref[...] = jnp.zeros(kb_ref.shape, BF16)
        vb_ref[...] = jnp.zeros(vb_ref.shape, BF16)

    @pl.when(j > 0)
    def _():
        project(lambda k, v: pl.when(j == pl.num_programs(1) - 1)(lambda: store_f32(k, v)))


def _stack_alias(prev, n_fixed_inputs, out_positions):
    if prev is None:
        return [], [], {}
    specs = [pl.BlockSpec(memory_space=pl.ANY)] * len(prev)
    aliases = {n_fixed_inputs + i: o for i, o in enumerate(out_positions)}
    return list(prev), specs, aliases


def _pre_even(h, g, w, cos, sin_signed, tm, slot, n_slots, prev, batch=None):
    n, d = h.shape
    width = w.shape[1] // 6
    n_pos_blocks = cos.shape[0] // tm
    tok = lambda wd: pl.BlockSpec((tm, wd), lambda i: (i, 0))
    const = lambda a: pl.BlockSpec(a.shape, lambda i: (0, 0))
    pos = pl.BlockSpec((tm, LANES), lambda i: (i % n_pos_blocks, 0))
    stk = pl.BlockSpec((None, tm, width), lambda i: (slot, i, 0))
    f32s = jax.ShapeDtypeStruct((n_slots, n, width), F32)
    bf16o = jax.ShapeDtypeStruct((n, width), BF16)
    t, s = tok(width), stk
    st, f32t = s, f32s
    assert width == H_DIFF * LANES
    sv = pl.BlockSpec((None, tm * H_DIFF, LANES), lambda i: (slot, i, 0))
    f32v = jax.ShapeDtypeStruct((n_slots, n * H_DIFF, LANES), F32)
    if batch is not None:
        tiles = n // batch // tm
        st = pl.BlockSpec((None, None, width, tm), lambda i: (slot, i // tiles, 0, i % tiles))
        f32t = jax.ShapeDtypeStruct((n_slots, batch, width, n // batch), F32)
    prev_args, prev_specs, aliases = _stack_alias(prev, 5, (1, 3, 6, 8))
    return pl.pallas_call(
        functools.partial(_pre_even_kernel, transposed=batch is not None),
        grid=(n // tm,),
        in_specs=[tok(d), const(g), const(w), pos, pos] + prev_specs,
        out_specs=[t, st, t, st, t, t, st, t, sv, t],
        out_shape=[bf16o, f32t, bf16o, f32t, bf16o, bf16o, f32t, bf16o, f32v, bf16o],
        input_output_aliases=aliases,
        compiler_params=_params(1),
    )(h, g, w, cos, sin_signed, *prev_args)


def _pre_odd(h, g, w, tm, slot, n_slots, prev, batch=None):
    n, d = h.shape
    width = w.shape[1] // 3
    padded = batch is not None
    bf16o = jax.ShapeDtypeStruct((n, width), BF16)
    if padded:
        assert tm == BAND_PAST
        tiles = n // batch // tm
        grid = (batch, tiles + 1)
        tile = lambda bi, j: bi * tiles + jnp.maximum(j - 1, 0)
        tok = lambda wd: pl.BlockSpec((tm, wd), lambda bi, j: (tile(bi, j), 0))
        const = lambda a: pl.BlockSpec(a.shape, lambda bi, j: (0, 0))
        stk = pl.BlockSpec((None, None, width, tm), lambda bi, j: (slot, bi, 0, 0))
        pad = pl.BlockSpec((None, tm, width), lambda bi, j: (bi, j, 0))
        f32s = jax.ShapeDtypeStruct((n_slots, batch, width, tm), F32)
        bf16p = jax.ShapeDtypeStruct((batch, (tiles + 1) * tm, width), BF16)
        out_specs, out_shape = [tok(width), stk, pad, stk, pad], [bf16o, f32s, bf16p, f32s, bf16p]
    else:
        grid = (n // tm,)
        tok = lambda wd: pl.BlockSpec((tm, wd), lambda i: (i, 0))
        const = lambda a: pl.BlockSpec(a.shape, lambda i: (0, 0))
        stk = pl.BlockSpec((None, tm, width), lambda i: (slot, i, 0))
        f32s = jax.ShapeDtypeStruct((n_slots, n, width), F32)
        out_specs, out_shape = [tok(width), stk, tok(width), stk, tok(width)], [bf16o, f32s, bf16o, f32s, bf16o]
    prev_args, prev_specs, aliases = _stack_alias(prev, 3, (1, 3))
    return pl.pallas_call(
        functools.partial(_pre_odd_kernel, padded=padded),
        grid=grid,
        in_specs=[tok(d), const(g), const(w)] + prev_specs,
        out_specs=out_specs,
        out_shape=out_shape,
        input_output_aliases=aliases,
        compiler_params=_params(len(grid)),
    )(h, g, w, *prev_args)


def _cast_kernel(x_ref, o_ref):
    o_ref[...] = x_ref[...].astype(o_ref.dtype)


def _to_bf16(w):
    layers, rows, cols = w.shape
    tr = rows // 4
    assert rows % 4 == 0 and tr % 16 == 0
    spec = pl.BlockSpec((None, tr, cols), lambda i, j: (i, j, 0))
    return pl.pallas_call(
        _cast_kernel,
        grid=(layers, rows // tr),
        in_specs=[spec],
        out_specs=spec,
        out_shape=jax.ShapeDtypeStruct(w.shape, BF16),
        compiler_params=_params(2),
    )(w)


def _post_kernel(*refs, n_o, ffn_chunk, final):
    h_ref = refs[0]
    o_refs = refs[1:1 + n_o]
    (p_ref, wout_ref, gffn_ref, wg_ref, wu_ref, wd_ref, gple_ref, wpg_ref, wp_ref) = refs[1 + n_o:10 + n_o]
    gfin_ref = refs[10 + n_o] if final else None
    out_ref = refs[-1]

    h = h_ref[...]
    off = 0
    for o_ref in o_refs:
        wd = o_ref.shape[1]
        h = h + _dot(o_ref[...], wout_ref[off:off + wd, :])
        off += wd

    hn = _rms(h, gffn_ref[...]).astype(BF16)
    hidden = wg_ref.shape[1]
    for c in range(hidden // ffn_chunk):
        cols = slice(c * ffn_chunk, (c + 1) * ffn_chunk)
        gt = _dot(hn, wg_ref[:, cols])
        up = _dot(hn, wu_ref[:, cols])
        act = (gt * _sigmoid(gt) * up).astype(BF16)
        h = h + _dot(act, wd_ref[cols, :])

    gate = _sigmoid(_dot(_rms(h, gple_ref[...]).astype(BF16), wpg_ref[...]))
    h = h + _dot(p_ref[...].astype(BF16), wp_ref[...]) * gate
    if final:
        h = _rms(h, gfin_ref[...])
    out_ref[...] = h


def _post(h, o_parts, p, layer, wout, wout_layer, gffn, wg, wu, wd, gple, wpg, wp, gfin, tm):
    n, d = h.shape
    final = gfin is not None
    hidden = wg.shape[2]
    ffn_chunk = hidden
    tok = lambda a: pl.BlockSpec((tm, a.shape[1]), lambda i: (i, 0))
    row = lambda a: pl.BlockSpec(a.shape, lambda i: (0, 0), pipeline_mode=pl.Buffered(1))
    mat = lambda a, li: pl.BlockSpec((None,) + a.shape[1:], lambda i: (li, 0, 0),
                                     pipeline_mode=pl.Buffered(1))
    consts = [wout, gffn, wg, wu, wd, gple, wpg, wp] + ([gfin] if final else [])
    const_specs = ([mat(wout, wout_layer), row(gffn)] + [mat(a, layer) for a in (wg, wu, wd)]
                   + [row(gple), mat(wpg, layer), mat(wp, layer)] + ([row(gfin)] if final else []))
    p_spec = pl.BlockSpec((None, tm, p.shape[2]), lambda i: (layer, i, 0))
    return pl.pallas_call(
        functools.partial(_post_kernel, n_o=len(o_parts), ffn_chunk=ffn_chunk, final=final),
        grid=(n // tm,),
        in_specs=[tok(h)] + [tok(o) for o in o_parts] + [p_spec] + const_specs,
        out_specs=tok(h),
        out_shape=jax.ShapeDtypeStruct((n, d), F32),
        compiler_params=_params(1),
    )(h, *o_parts, p, *consts)


def _lane_tile(i):
    return slice(i * LANES, (i + 1) * LANES)


def _sb_kernel(q_ref, k_ref, v_ref, tt_ref, o_ref, acc_ref, run_ref, *, tq, q_off):
    qi = pl.program_id(2)
    n_pairs = q_ref.shape[1] // LANES
    lane = lax.broadcasted_iota(jnp.int32, (tq, LANES), 1)
    row_pos = q_off + qi * tq + lax.broadcasted_iota(jnp.int32, (tq, LANES), 0)
    first = lane < HEAD_DIM
    blk0 = (q_off + qi * tq) // SB_KEY_BLOCK
    tt = tt_ref[...]
    heads = range(2 * n_pairs)
    q_heads = []
    for pair in range(n_pairs):
        q2 = q_ref[:, _lane_tile(pair)]
        zero = jnp.zeros_like(q2)
        q_heads += [jnp.where(first, q2, zero), jnp.where(first, zero, q2)]
    acc_ref[...] = jnp.zeros(acc_ref.shape, F32)
    run_ref[...] = jnp.zeros(run_ref.shape, F32)

    def span(first_block, n_blocks, own_block):
        n_halves = 2 * n_blocks
        width = n_halves * LANES
        start = first_block * SB_KEY_BLOCK
        if not isinstance(start, int):
            start = pl.multiple_of(start, SB_KEY_BLOCK)
        kblk = [k_ref[pl.ds(start, width), _lane_tile(pair)] for pair in range(n_pairs)]
        vblk = [v_ref[pl.ds(start, width), _lane_tile(pair)] for pair in range(n_pairs)]
        s = [_dot_nt(q_heads[h], kblk[h // 2]) for h in heads]
        soft = [jnp.log(1.0 + jnp.exp2(-jnp.abs(s[h]))) * LOG2E for h in heads]
        log_beta = [jnp.minimum(s[h], 0.0) - soft[h] for h in heads]
        log_keep = [log_beta[h] - s[h] for h in heads]
        masks = {}
        if own_block:
            for i in (n_halves - 2, n_halves - 1):
                masks[i] = (first_block * SB_KEY_BLOCK + i * LANES + lane) < row_pos
        cs = []
        for h in heads:
            per_half = []
            for i in range(n_halves):
                keep = log_keep[h][:, _lane_tile(i)]
                if i in masks:
                    keep = jnp.where(masks[i], keep, 0.0)
                hi = keep.astype(BF16)
                lo = (keep - hi.astype(F32)).astype(BF16)
                per_half.append(_dot(jnp.concatenate([hi, lo], axis=1), tt))
            cs.append(per_half)
        for h in heads:
            run = run_ref[h]
            w = [None] * n_halves
            for i in reversed(range(n_halves)):
                w_i = jnp.exp2(log_beta[h][:, _lane_tile(i)] + cs[h][i][:, :LANES] + run)
                if i in masks:
                    w_i = jnp.where(masks[i], w_i, 0.0)
                w[i] = w_i.astype(BF16)
                run = run + cs[h][i][:, LANES:]
            acc_ref[h] += _dot(jnp.concatenate(w, axis=1), vblk[h // 2])
            run_ref[h] = run

    @pl.when(blk0 == 0)
    def _():
        span(0, 1, True)

    @pl.when(blk0 > 0)
    def _():
        span(blk0 - 1, 2, True)

    def cond(carry):
        i, live = carry
        return jnp.logical_and(i < blk0 - 1, live > SB_EXIT_LOG2)

    def live_mass():
        return jnp.max(functools.reduce(jnp.maximum, [run_ref[h] for h in heads]))

    def body(carry):
        i, _ = carry
        span(blk0 - 2 - i, 1, False)
        return i + 1, live_mass()

    lax.while_loop(cond, body, (jnp.int32(0), live_mass()))
    for pair in range(n_pairs):
        o_ref[:, _lane_tile(pair)] = jnp.where(first, acc_ref[2 * pair],
                                               acc_ref[2 * pair + 1]).astype(o_ref.dtype)


def _suffix_matrix():
    r = lax.broadcasted_iota(jnp.int32, (2 * LANES, 2 * LANES), 0) % LANES
    c = lax.broadcasted_iota(jnp.int32, (2 * LANES, 2 * LANES), 1)
    return jnp.where((c >= LANES) | (r > c), 1.0, 0.0).astype(BF16)


def _keys_from_past(kernel_fn, n_before, past_rows):
    def wrapped(*refs):
        past_new = refs[n_before:n_before + 4]
        k_buf, v_buf = refs[-2:]
        p = past_rows
        for past, new, buf in ((past_new[0], past_new[2], k_buf), (past_new[1], past_new[3], v_buf)):
            t = new.shape[0]
            if past.shape[1] == p:
                buf[0:p, :] = past[...].T.astype(buf.dtype)
            else:
                heads = past.shape[0] // p
                for hd in range(heads):
                    buf[0:p, _lane_tile(hd)] = past[pl.ds(hd, p, stride=heads), :].astype(buf.dtype)
            buf[p:p + t, :] = new[...]
            buf[p + t:, :] = jnp.zeros((buf.shape[0] - p - t, buf.shape[1]), buf.dtype)
        kernel_fn(*refs[:n_before], k_buf, v_buf, *refs[n_before + 4:-2])
    return wrapped


def _cache_view(cache, layer):
    n_layers, b, p = cache.shape[:3]
    if cache.shape[-1] == LANES:
        return cache.reshape(n_layers, b, -1, LANES), layer, p
    perm = (0, 1) + tuple(range(3, cache.ndim)) + (2,)
    return jnp.transpose(cache, perm).reshape(n_layers, b, -1, p), layer, p


def _key_operands(kernel_fn, n_before, k, v, past, total, tiles=1):
    index = lambda bi, hp, qi: (bi, 0, hp)
    spec = lambda a: pl.BlockSpec((None, a.shape[1], tiles * LANES), index)
    if past is None:
        return kernel_fn, k.shape[1], [k, v], [spec(k), spec(v)], []
    assert tiles * LANES == k.shape[2]
    past_specs = [pl.BlockSpec((None, None) + a.shape[2:], lambda bi, hp, qi, layer=layer: (layer, bi, 0, 0))
                  for a, layer, _ in past]
    buf = pltpu.VMEM((total, tiles * LANES), BF16)
    operands = [past[0][0], past[1][0], k, v]
    return (_keys_from_past(kernel_fn, n_before, past[0][2]), total, operands,
            past_specs + [spec(k), spec(v)], [buf, buf])


def _sb_attn(q, k, v, tq, q_off, past=None, total=None, tiles=1):
    b, t, w = q.shape
    body, keys, key_args, key_specs, key_scratch = _key_operands(
        functools.partial(_sb_kernel, tq=tq, q_off=q_off), 1, k, v, past, total, tiles)
    assert q_off % SB_KEY_BLOCK == 0 and (tq == SB_KEY_BLOCK or (t == tq and tq < SB_KEY_BLOCK))
    assert -(-(q_off + t) // SB_KEY_BLOCK) * SB_KEY_BLOCK <= keys and w % (tiles * LANES) == 0
    tt = _suffix_matrix()
    qspec = pl.BlockSpec((None, tq, tiles * LANES), lambda bi, hp, qi: (bi, qi, hp))
    state = pltpu.VMEM((2 * tiles, tq, LANES), F32)
    return pl.pallas_call(
        body,
        grid=(b, w // (tiles * LANES), t // tq),
        in_specs=[qspec] + key_specs + [pl.BlockSpec(tt.shape, lambda bi, hp, qi: (0, 0))],
        out_specs=qspec,
        out_shape=jax.ShapeDtypeStruct((b, t, w), BF16),
        scratch_shapes=[state, state] + key_scratch,
        compiler_params=_params(3),
    )(q, *key_args, tt)


def _diff_kernel(lam_ref, gain_ref, q_ref, k_ref, v_ref, o_ref, m_ref, l_ref, acc_ref,
                 *, tq, tk, q_off, kv_len, n_diag, lam_init):
    qi = pl.program_id(2)
    n_heads = q_ref.shape[1] // LANES
    lane = lax.broadcasted_iota(jnp.int32, (tq, LANES), 1)
    first = lane < HEAD_DIM
    q_maps = []
    for hd in range(n_heads):
        q2 = q_ref[:, _lane_tile(hd)]
        zero = jnp.zeros_like(q2)
        q_maps += [jnp.where(first, q2, zero), jnp.where(first, zero, q2)]
    diag_rows = min(tq, tk)
    row_chunk = lax.broadcasted_iota(jnp.int32, (diag_rows, tk), 0) // CHUNK
    col = lax.broadcasted_iota(jnp.int32, (diag_rows, tk), 1)
    diag_mask = (col // CHUNK) <= row_chunk
    blk0 = (q_off + qi * tq) // tk
    ones = jnp.ones((tk, LANES), BF16)

    m_ref[...] = jnp.full(m_ref.shape, NEG, F32)
    l_ref[...] = jnp.zeros(l_ref.shape, F32)
    acc_ref[...] = jnp.zeros(acc_ref.shape, F32)

    def block(kb, masked, rows=slice(0, tq)):
        start = pl.multiple_of(kb * tk, tk)
        kblk = [k_ref[pl.ds(start, tk), _lane_tile(hd)] for hd in range(n_heads)]
        vext = [jnp.concatenate([v_ref[pl.ds(start, tk), _lane_tile(hd)], ones], axis=1)
                for hd in range(n_heads)]
        if masked:
            mask = diag_mask
            if kv_len % tk:
                mask = mask & (kb * tk + col < kv_len)
        maps = range(2 * n_heads)
        slabs = range(tk // LANES)
        s = [_dot_nt(q_maps[mp][rows], kblk[mp // 2]) for mp in maps]
        if masked:
            s = [jnp.where(mask, s[mp], NEG) for mp in maps]
        m_old = [m_ref[mp, rows, :] for mp in maps]
        m_new = []
        for mp in maps:
            mx = s[mp][:, :LANES]
            for j in slabs[1:]:
                mx = jnp.maximum(mx, s[mp][:, j * LANES:(j + 1) * LANES])
            m_new.append(jnp.maximum(m_old[mp], jnp.max(mx, axis=-1, keepdims=True)))
        alpha = [jnp.exp2(m_old[mp] - m_new[mp]) for mp in maps]
        p = [jnp.concatenate([jnp.exp2(s[mp][:, j * LANES:(j + 1) * LANES] - m_new[mp]).astype(BF16)
                              for j in slabs], axis=1) for mp in maps]
        pv = [_dot(p[mp], vext[mp // 2]) for mp in maps]
        for mp in maps:
            acc_ref[mp, rows, :] = acc_ref[mp, rows, :] * alpha[mp] + pv[mp][:, :LANES]
            l_ref[mp, rows, :] = l_ref[mp, rows, :] * alpha[mp] + pv[mp][:, LANES:]
            m_ref[mp, rows, :] = m_new[mp]

    for r in range(n_diag):
        rows = slice(r * diag_rows, (r + 1) * diag_rows)
        block(blk0 + r, True, rows)
        for j in range(r):
            block(blk0 + j, False, rows)

    def body(i, carry):
        block(2 * i, False)
        block(2 * i + 1, False)
        return carry

    lax.fori_loop(0, blk0 // 2, body, 0)

    @pl.when(blk0 % 2 == 1)
    def _():
        block(blk0 - 1, False)

    lp = lam_ref[...]
    lam = (jnp.exp(jnp.sum(lp[0:1] * lp[1:2], axis=-1, keepdims=True))
           - jnp.exp(jnp.sum(lp[2:3] * lp[3:4], axis=-1, keepdims=True)) + lam_init)
    for hd in range(n_heads):
        o = acc_ref[2 * hd] / l_ref[2 * hd] - lam * (acc_ref[2 * hd + 1] / l_ref[2 * hd + 1])
        o = o * lax.rsqrt(jnp.mean(o * o, axis=-1, keepdims=True) + EPS)
        o_ref[:, _lane_tile(hd)] = (o * gain_ref[hd] * (1.0 - lam_init)).astype(o_ref.dtype)


def _diff_attn(q, k, v, lam_params, gain, tq, tk, q_off, kv_len, lam_init, past=None, total=None,
               tiles=1):
    b, t, w = q.shape
    n_diag = max(1, tq // tk)
    body, keys, key_args, key_specs, key_scratch = _key_operands(
        functools.partial(_diff_kernel, tq=tq, tk=tk, q_off=q_off, kv_len=kv_len, n_diag=n_diag,
                          lam_init=lam_init), 3, k, v, past, total, tiles)
    assert q_off % tk == 0 and (tq % tk == 0 or t == tq) and keys % tk == 0 and CHUNK % 8 == 0
    assert (q_off + t + tk - 1) // tk * tk <= keys and w % (tiles * LANES) == 0
    gain3 = gain.reshape(gain.shape[0], 1, gain.shape[1])
    qspec = pl.BlockSpec((None, tq, tiles * LANES), lambda bi, hd, qi: (bi, qi, hd))
    return pl.pallas_call(
        body,
        grid=(b, w // (tiles * LANES), t // tq),
        in_specs=[pl.BlockSpec(lam_params.shape, lambda bi, hd, qi: (0, 0)),
                  pl.BlockSpec((tiles, 1, LANES), lambda bi, hd, qi: (hd, 0, 0)),
                  qspec] + key_specs,
        out_specs=qspec,
        out_shape=jax.ShapeDtypeStruct((b, t, w), BF16),
        scratch_shapes=[pltpu.VMEM((2 * tiles, tq, LANES), F32)] * 3 + key_scratch,
        compiler_params=_params(3),
    )(lam_params, gain3, q, *key_args)


def _band_kernel(q_ref, k_ref, v_ref, bias_ref, o_ref, *, rows, n_sub, valid_lo, valid_hi):
    qi = pl.program_id(2)
    n_pairs = q_ref.shape[1] // LANES
    lane = lax.broadcasted_iota(jnp.int32, (rows, LANES), 1)
    first = lane < HEAD_DIM
    ones = jnp.ones((BAND_WINDOW, LANES), BF16)
    kcol = lax.broadcasted_iota(jnp.int32, (1, BAND_WINDOW), 1)

    def sub(j, carry):
        r0 = pl.multiple_of(j * rows, rows)
        ws = pl.multiple_of((qi * n_sub + j) * rows, rows)
        krow = ws + kcol
        penalty = jnp.where((krow >= valid_lo) & (krow < valid_hi), 0.0, NEG)
        for pair in range(n_pairs):
            q2 = q_ref[pl.ds(r0, rows), _lane_tile(pair)]
            zero = jnp.zeros_like(q2)
            kw = k_ref[pl.ds(ws, BAND_WINDOW), _lane_tile(pair)]
            vext = jnp.concatenate([v_ref[pl.ds(ws, BAND_WINDOW), _lane_tile(pair)], ones], axis=1)
            outs = []
            for head in range(2):
                qh = jnp.where(first if head == 0 else jnp.logical_not(first), q2, zero)
                s = _dot_nt(qh, kw) + bias_ref[2 * pair + head] + penalty
                p = jnp.exp2(s - jnp.max(s, axis=-1, keepdims=True))
                pv = _dot(p.astype(BF16), vext)
                outs.append(pv[:, :LANES] / pv[:, LANES:])
            o_ref[pl.ds(r0, rows), _lane_tile(pair)] = jnp.where(first, outs[0], outs[1]).astype(o_ref.dtype)
        return carry

    lax.fori_loop(0, n_sub, sub, 0, unroll=BAND_UNROLL if n_sub % BAND_UNROLL == 0 else 1)


def _band_bias_table(rel_bias, rows):
    heads = rel_bias.shape[0]
    span = BAND_WINDOW + rows - 1
    n_far = BAND_PAST - REL_CLIP + rows
    rb = rel_bias.astype(F32) * LOG2E
    near = rb[:, 2 * REL_CLIP - 1:0:-1]
    assert n_far + near.shape[1] == span
    e = jnp.concatenate([jnp.broadcast_to(rb[:, -1:], (heads, n_far)), near,
                         jnp.zeros((heads, 1), F32)], axis=1)
    skew = jnp.broadcast_to(e[:, None, :], (heads, rows, span + 1)).reshape(heads, rows * (span + 1))
    skew = skew[:, :rows * span].reshape(heads, rows, span)
    bias = skew[:, :, rows - 1:]
    i = jnp.arange(rows, dtype=jnp.int32)[:, None]
    rel_key = jnp.arange(BAND_WINDOW, dtype=jnp.int32)[None, :] - BAND_PAST
    kc = jnp.floor_divide(rel_key, CHUNK)
    qc = i // CHUNK
    in_band = (kc <= qc) & (kc >= qc - BAND_PAST // CHUNK)
    return jnp.where(in_band[None], bias, NEG)


def _band_attn(q, k, v, bias_tab, rows, n_sub, valid_lo, valid_hi, past=None, total=None, tiles=1):
    b, t, w = q.shape
    tq = rows * n_sub
    body, keys, key_args, key_specs, key_scratch = _key_operands(
        functools.partial(_band_kernel, rows=rows, n_sub=n_sub, valid_lo=valid_lo, valid_hi=valid_hi),
        1, k, v, past, total, tiles)
    assert t % tq == 0 and (t - rows) + BAND_WINDOW <= keys and w % (tiles * LANES) == 0
    qspec = pl.BlockSpec((None, tq, tiles * LANES), lambda bi, hp, qi: (bi, qi, hp))
    return pl.pallas_call(
        body,
        grid=(b, w // (tiles * LANES), t // tq),
        in_specs=[qspec] + key_specs +
                 [pl.BlockSpec((2 * tiles, rows, BAND_WINDOW), lambda bi, hp, qi: (hp, 0, 0))],
        out_specs=qspec,
        out_shape=jax.ShapeDtypeStruct((b, t, w), BF16),
        scratch_shapes=key_scratch,
        compiler_params=_params(3),
    )(q, *key_args, bias_tab)


def _rope_tables(pos, reps):
    half = HEAD_DIM // 2
    inv = ROPE_THETA ** (-jnp.arange(half, dtype=F32) / half)
    ang = pos.astype(F32)[:, None] * inv[None, :]
    cos, sin = jnp.cos(ang), jnp.sin(ang)
    cos = jnp.tile(cos, (reps, LANES // half))
    sin_signed = jnp.tile(jnp.concatenate([-sin, sin], axis=1), (reps, LANES // HEAD_DIM))
    return cos, sin_signed


def _trunk(x, p, past, weights, cfg):
    (norm_mix, w_in_even, w_out_even, diff_lambda, diff_norm, w_in_odd, w_out_odd, rel_bias,
     norm_ffn, w_gate, w_up, w_down, norm_ple, w_ple_gate, w_ple, norm_final) = weights
    b, t, d = x.shape
    n = b * t
    depth = norm_mix.shape[0]
    tm = cfg["tm"]
    q_off = 0 if past is None else past[0].shape[2]
    pos = q_off + jnp.arange(t, dtype=jnp.int32)
    if past is None:
        cos, sin_signed = _rope_tables(pos, 1)
    else:
        cos, sin_signed = _rope_tables(pos, tm // t)
    row = lambda a: a.reshape(1, -1)
    seq = lambda a: a.reshape(b, t, a.shape[-1])
    tiles_of = lambda a: a.shape[-1] // LANES if cfg["whole_width"] else 1

    h = x.reshape(n, d)
    n_even, n_odd = w_in_even.shape[0], w_in_odd.shape[0]
    even_state = None
    odd_state = None
    for li in range(depth):
        if li % 2 == 0:
            e = li // 2
            qa, ka, kab, va, vab, qb, kb, kbb, vb, vbb = _pre_even(
                h, row(norm_mix[li]), w_in_even[e], cos, sin_signed, tm, e, n_even, even_state,
                batch=b if past is None else None)
            even_state = (ka, va, kb, vb)
            lam_init = 0.8 - 0.6 * math.exp(-0.3 * li)
            keys = [seq(a) for a in (kab, vab, kbb, vbb)]
            if past is None:
                sb_past = df_past = total = None
                kv_len = t
            else:
                total = cfg["keys_total"]
                sb_past = (_cache_view(past[0], e), _cache_view(past[1], e))
                df_past = (_cache_view(past[2], e), _cache_view(past[3], e))
                kv_len = q_off + t
            o_a = _sb_attn(seq(qa), keys[0], keys[1], cfg["sb_tq"], q_off, sb_past, total, tiles_of(qa))
            o_b = _diff_attn(seq(qb), keys[2], keys[3], diff_lambda[e], diff_norm[e],
                             cfg["diff_tq"], cfg["diff_tk"], q_off, kv_len, lam_init, df_past, total,
                             tiles_of(qb))
            o_parts = [o_a.reshape(n, -1), o_b.reshape(n, -1)]
            w_out = w_out_even
        else:
            od = li // 2
            rows = cfg["band_rows"]
            bias_tab = _band_bias_table(rel_bias[od], rows)
            if past is None:
                assert t >= BAND_PAST
                q, k, kk, v, vv = _pre_odd(h, row(norm_mix[li]), w_in_odd[od], tm, od, n_odd,
                                           odd_state, batch=b)
                valid_lo, valid_hi = BAND_PAST, BAND_PAST + t
                bd_past = total = None
            else:
                q, k, kbf, v, vbf = _pre_odd(h, row(norm_mix[li]), w_in_odd[od], tm, od, n_odd,
                                             odd_state)
                kk, vv = seq(kbf), seq(vbf)
                bd_past = (_cache_view(past[4], od), _cache_view(past[5], od))
                cache_rows = bd_past[0][2]
                assert cache_rows == BAND_PAST
                valid_lo, valid_hi = 0, cache_rows + t
                total = BAND_WINDOW
            odd_state = (k, v)
            o = _band_attn(seq(q), kk, vv, bias_tab, rows, cfg["band_sub"], valid_lo, valid_hi,
                           bd_past, total, tiles_of(q))
            o_parts = [o.reshape(n, -1)]
            w_out = w_out_odd
        gfin = row(norm_final) if li == depth - 1 else None
        h = _post(h, o_parts, p.reshape(depth, n, -1), li, w_out, li // 2, row(norm_ffn[li]), w_gate,
                  w_up, w_down, row(norm_ple[li]), w_ple_gate, w_ple, gfin, tm)
    y = h.reshape(b, t, d)
    rows = lambda a, shp: a.reshape((a.shape[0], b, -1) + shp)

    def cols(a, shp):
        a = a.reshape(a.shape[:2] + shp + a.shape[3:])
        return jnp.moveaxis(a, -1, 2)

    heads = cols if past is None else rows
    sb_k, sb_v, df_k, df_v = even_state
    bd_k, bd_v = odd_state
    state = (heads(sb_k, (H_SB, HEAD_DIM)), heads(sb_v, (H_SB, HEAD_DIM)),
             heads(df_k, (H_DIFF, 2, HEAD_DIM)), rows(df_v, (H_DIFF, 2 * HEAD_DIM)),
             heads(bd_k, (H_BAND, HEAD_DIM)), heads(bd_v, (H_BAND, HEAD_DIM)))
    return y, state


def kernel(x_prompt, x_sample, cache_sb_k, cache_sb_v, cache_diff_k, cache_diff_v, cache_band_k, cache_band_v, p_prompt, p_sample, norm_mix, w_in_even, w_out_even, diff_lambda, diff_norm, w_in_odd, w_out_odd, rel_bias, norm_ffn, w_gate, w_up, w_down, norm_ple, w_ple_gate, w_ple, norm_final):
    bf = _to_bf16
    weights = (norm_mix, bf(w_in_even), bf(w_out_even), diff_lambda, diff_norm, bf(w_in_odd),
               bf(w_out_odd), rel_bias, norm_ffn, bf(w_gate), bf(w_up), bf(w_down), norm_ple,
               bf(w_ple_gate), bf(w_ple), norm_final)
    t_p = x_prompt.shape[1]
    t_s = x_sample.shape[1]
    past_len = cache_sb_k.shape[2]
    diff_tk = 512
    cfg_p = dict(tm=512, sb_tq=256, diff_tq=min(1024, t_p), diff_tk=diff_tk, band_rows=LANES,
                 band_sub=min(8, t_p // LANES), whole_width=False)
    keys_total = -(-(past_len + t_s) // diff_tk) * diff_tk
    cfg_s = dict(tm=min(512, x_sample.shape[0] * t_s), sb_tq=t_s, diff_tq=t_s, diff_tk=diff_tk, band_rows=t_s, band_sub=1,
                 keys_total=keys_total, whole_width=True)
    y_p, st_p = _trunk(x_prompt, p_prompt, None, weights, cfg_p)
    past = (cache_sb_k, cache_sb_v, cache_diff_k, cache_diff_v, cache_band_k, cache_band_v)
    y_s, st_s = _trunk(x_sample, p_sample, past, weights, cfg_s)
    return (y_p, y_s) + tuple(st_p) + tuple(st_s)
```

```python
import functools
import math

import jax
import jax.numpy as jnp
from jax import lax
from jax.experimental import pallas as pl
from jax.experimental.pallas import tpu as pltpu

CHUNK = 64
HEAD_DIM = 64
H_SB = 8
H_DIFF = 4
H_BAND = 16
BAND_PAST = 8 * CHUNK
REL_CLIP = 128
ROPE_THETA = 10000.0
EPS = 1e-6
NEG = -1e30
SCALE = HEAD_DIM ** -0.5
LOG2E = 1.4426950408889634
QSCALE = SCALE * LOG2E

LANES = 128
VMEM_LIMIT = 56 * 1024 * 1024

SB_KEY_BLOCK = 2 * LANES
SB_EXIT_LOG2 = -160.0
BAND_WINDOW = BAND_PAST + LANES
BAND_UNROLL = 8

F32 = jnp.float32
BF16 = jnp.bfloat16


def _rms(x, g):
    return x * lax.rsqrt(jnp.mean(x * x, axis=-1, keepdims=True) + EPS) * g


def _sigmoid(x):
    return 1.0 / (1.0 + jnp.exp(-x))


def _dot(a, b):
    return jnp.dot(a, b, preferred_element_type=F32)


def _dot_nt(a, b):
    return lax.dot_general(a, b, (((1,), (1,)), ((), ())), preferred_element_type=F32)


def _params(n_axes):
    return pltpu.CompilerParams(dimension_semantics=("arbitrary",) * n_axes,
                                vmem_limit_bytes=VMEM_LIMIT)


def _rope(x, cos, sin_signed, first_half):
    outs = []
    for j in range(x.shape[1] // LANES):
        xj = x[:, j * LANES:(j + 1) * LANES]
        partner = jnp.where(first_half, pltpu.roll(xj, LANES - HEAD_DIM // 2, 1),
                            pltpu.roll(xj, HEAD_DIM // 2, 1))
        outs.append(xj * cos + partner * sin_signed)
    return jnp.concatenate(outs, axis=1)


def _pre_even_kernel(h_ref, g_ref, w_ref, cos_ref, sin_ref, *rest, transposed):
    flip = (lambda a: a.T) if transposed else (lambda a: a)
    (qa_ref, ka_ref, kab_ref, va_ref, vab_ref, qb_ref, kb_ref, kbb_ref, vb_ref, vbb_ref) = rest[-10:]
    hn = _rms(h_ref[...], g_ref[...]).astype(BF16)
    width = qa_ref.shape[1]

    def proj(c):
        return _dot(hn, w_ref[:, c * width:(c + 1) * width])

    cos = cos_ref[...]
    sin_signed = sin_ref[...]
    lane = lax.broadcasted_iota(jnp.int32, cos.shape, 1)
    first_half = (lane % HEAD_DIM) < (HEAD_DIM // 2)

    qa_ref[...] = (proj(0) * QSCALE).astype(BF16)
    ka = proj(1)
    ka_ref[...] = flip(ka)
    kab_ref[...] = ka.astype(BF16)
    va = proj(2)
    va_ref[...] = flip(va)
    vab_ref[...] = va.astype(BF16)
    qb_ref[...] = (_rope(proj(3), cos, sin_signed, first_half) * QSCALE).astype(BF16)
    kb = _rope(proj(4), cos, sin_signed, first_half)
    kb_ref[...] = flip(kb)
    kbb_ref[...] = kb.astype(BF16)
    vb = proj(5)
    for hd in range(H_DIFF):
        vb_ref[pl.ds(hd, vb.shape[0], stride=H_DIFF), :] = vb[:, _lane_tile(hd)]
    vbb_ref[...] = vb.astype(BF16)


def _pre_odd_kernel(h_ref, g_ref, w_ref, *rest, padded):
    q_ref, k_ref, kb_ref, v_ref, vb_ref = rest[-5:]
    width = q_ref.shape[1]

    def project(keep_f32):
        hn = _rms(h_ref[...], g_ref[...]).astype(BF16)
        q_ref[...] = (_dot(hn, w_ref[:, :width]) * QSCALE).astype(BF16)
        k = _dot(hn, w_ref[:, width:2 * width])
        kb_ref[...] = k.astype(BF16)
        v = _dot(hn, w_ref[:, 2 * width:])
        vb_ref[...] = v.astype(BF16)
        keep_f32(k, v)

    def store_f32(k, v):
        k_ref[...] = k.T if padded else k
        v_ref[...] = v.T if padded else v

    if not padded:
        project(store_f32)
        return

    j = pl.program_id(1)

    @pl.when(j == 0)
    def _():
        kb_ref[...] = jnp.zeros(kb_ref.shape, BF16)
        vb_ref[...] = jnp.zeros(vb_ref.shape, BF16)

    @pl.when(j > 0)
    def _():
        project(lambda k, v: pl.when(j == pl.num_programs(1) - 1)(lambda: store_f32(k, v)))


def _stack_alias(prev, n_fixed_inputs, out_positions):
    if prev is None:
        return [], [], {}
    specs = [pl.BlockSpec(memory_space=pl.ANY)] * len(prev)
    aliases = {n_fixed_inputs + i: o for i, o in enumerate(out_positions)}
    return list(prev), specs, aliases


def _pre_even(h, g, w, cos, sin_signed, tm, slot, n_slots, prev, batch=None):
    n, d = h.shape
    width = w.shape[1] // 6
    n_pos_blocks = cos.shape[0] // tm
    tok = lambda wd: pl.BlockSpec((tm, wd), lambda i: (i, 0))
    const = lambda a: pl.BlockSpec(a.shape, lambda i: (0, 0))
    pos = pl.BlockSpec((tm, LANES), lambda i: (i % n_pos_blocks, 0))
    stk = pl.BlockSpec((None, tm, width), lambda i: (slot, i, 0))
    f32s = jax.ShapeDtypeStruct((n_slots, n, width), F32)
    bf16o = jax.ShapeDtypeStruct((n, width), BF16)
    t, s = tok(width), stk
    st, f32t = s, f32s
    assert width == H_DIFF * LANES
    sv = pl.BlockSpec((None, tm * H_DIFF, LANES), lambda i: (slot, i, 0))
    f32v = jax.ShapeDtypeStruct((n_slots, n * H_DIFF, LANES), F32)
    if batch is not None:
        tiles = n // batch // tm
        st = pl.BlockSpec((None, None, width, tm), lambda i: (slot, i // tiles, 0, i % tiles))
        f32t = jax.ShapeDtypeStruct((n_slots, batch, width, n // batch), F32)
    prev_args, prev_specs, aliases = _stack_alias(prev, 5, (1, 3, 6, 8))
    return pl.pallas_call(
        functools.partial(_pre_even_kernel, transposed=batch is not None),
        grid=(n // tm,),
        in_specs=[tok(d), const(g), const(w), pos, pos] + prev_specs,
        out_specs=[t, st, t, st, t, t, st, t, sv, t],
        out_shape=[bf16o, f32t, bf16o, f32t, bf16o, bf16o, f32t, bf16o, f32v, bf16o],
        input_output_aliases=aliases,
        compiler_params=_params(1),
    )(h, g, w, cos, sin_signed, *prev_args)


def _pre_odd(h, g, w, tm, slot, n_slots, prev, batch=None):
    n, d = h.shape
    width = w.shape[1] // 3
    padded = batch is not None
    bf16o = jax.ShapeDtypeStruct((n, width), BF16)
    if padded:
        assert tm == BAND_PAST
        tiles = n // batch // tm
        grid = (batch, tiles + 1)
        tile = lambda bi, j: bi * tiles + jnp.maximum(j - 1, 0)
        tok = lambda wd: pl.BlockSpec((tm, wd), lambda bi, j: (tile(bi, j), 0))
        const = lambda a: pl.BlockSpec(a.shape, lambda bi, j: (0, 0))
        stk = pl.BlockSpec((None, None, width, tm), lambda bi, j: (slot, bi, 0, 0))
        pad = pl.BlockSpec((None, tm, width), lambda bi, j: (bi, j, 0))
        f32s = jax.ShapeDtypeStruct((n_slots, batch, width, tm), F32)
        bf16p = jax.ShapeDtypeStruct((batch, (tiles + 1) * tm, width), BF16)
        out_specs, out_shape = [tok(width), stk, pad, stk, pad], [bf16o, f32s, bf16p, f32s, bf16p]
    else:
        grid = (n // tm,)
        tok = lambda wd: pl.BlockSpec((tm, wd), lambda i: (i, 0))
        const = lambda a: pl.BlockSpec(a.shape, lambda i: (0, 0))
        stk = pl.BlockSpec((None, tm, width), lambda i: (slot, i, 0))
        f32s = jax.ShapeDtypeStruct((n_slots, n, width), F32)
        out_specs, out_shape = [tok(width), stk, tok(width), stk, tok(width)], [bf16o, f32s, bf16o, f32s, bf16o]
    prev_args, prev_specs, aliases = _stack_alias(prev, 3, (1, 3))
    return pl.pallas_call(
        functools.partial(_pre_odd_kernel, padded=padded),
        grid=grid,
        in_specs=[tok(d), const(g), const(w)] + prev_specs,
        out_specs=out_specs,
        out_shape=out_shape,
        input_output_aliases=aliases,
        compiler_params=_params(len(grid)),
    )(h, g, w, *prev_args)


def _cast_kernel(x_ref, o_ref):
    o_ref[...] = x_ref[...].astype(o_ref.dtype)


def _to_bf16(w):
    layers, rows, cols = w.shape
    tr = rows // 4
    assert rows % 4 == 0 and tr % 16 == 0
    spec = pl.BlockSpec((None, tr, cols), lambda i, j: (i, j, 0))
    return pl.pallas_call(
        _cast_kernel,
        grid=(layers, rows // tr),
        in_specs=[spec],
        out_specs=spec,
        out_shape=jax.ShapeDtypeStruct(w.shape, BF16),
        compiler_params=_params(2),
    )(w)


def _post_kernel(*refs, n_o, ffn_chunk, final):
    h_ref = refs[0]
    o_refs = refs[1:1 + n_o]
    (p_ref, wout_ref, gffn_ref, wg_ref, wu_ref, wd_ref, gple_ref, wpg_ref, wp_ref) = refs[1 + n_o:10 + n_o]
    gfin_ref = refs[10 + n_o] if final else None
    out_ref = refs[-1]

    h = h_ref[...]
    off = 0
    for o_ref in o_refs:
        wd = o_ref.shape[1]
        h = h + _dot(o_ref[...], wout_ref[off:off + wd, :])
        off += wd

    hn = _rms(h, gffn_ref[...]).astype(BF16)
    hidden = wg_ref.shape[1]
    for c in range(hidden // ffn_chunk):
        cols = slice(c * ffn_chunk, (c + 1) * ffn_chunk)
        gt = _dot(hn, wg_ref[:, cols])
        up = _dot(hn, wu_ref[:, cols])
        act = (gt * _sigmoid(gt) * up).astype(BF16)
        h = h + _dot(act, wd_ref[cols, :])

    gate = _sigmoid(_dot(_rms(h, gple_ref[...]).astype(BF16), wpg_ref[...]))
    h = h + _dot(p_ref[...].astype(BF16), wp_ref[...]) * gate
    if final:
        h = _rms(h, gfin_ref[...])
    out_ref[...] = h


def _post(h, o_parts, p, layer, wout, wout_layer, gffn, wg, wu, wd, gple, wpg, wp, gfin, tm):
    n, d = h.shape
    final = gfin is not None
    hidden = wg.shape[2]
    ffn_chunk = hidden
    tok = lambda a: pl.BlockSpec((tm, a.shape[1]), lambda i: (i, 0))
    row = lambda a: pl.BlockSpec(a.shape, lambda i: (0, 0), pipeline_mode=pl.Buffered(1))
    mat = lambda a, li: pl.BlockSpec((None,) + a.shape[1:], lambda i: (li, 0, 0),
                                     pipeline_mode=pl.Buffered(1))
    consts = [wout, gffn, wg, wu, wd, gple, wpg, wp] + ([gfin] if final else [])
    const_specs = ([mat(wout, wout_layer), row(gffn)] + [mat(a, layer) for a in (wg, wu, wd)]
                   + [row(gple), mat(wpg, layer), mat(wp, layer)] + ([row(gfin)] if final else []))
    p_spec = pl.BlockSpec((None, tm, p.shape[2]), lambda i: (layer, i, 0))
    return pl.pallas_call(
        functools.partial(_post_kernel, n_o=len(o_parts), ffn_chunk=ffn_chunk, final=final),
        grid=(n // tm,),
        in_specs=[tok(h)] + [tok(o) for o in o_parts] + [p_spec] + const_specs,
        out_specs=tok(h),
        out_shape=jax.ShapeDtypeStruct((n, d), F32),
        compiler_params=_params(1),
    )(h, *o_parts, p, *consts)


def _lane_tile(i):
    return slice(i * LANES, (i + 1) * LANES)


def _sb_kernel(q_ref, k_ref, v_ref, tt_ref, o_ref, acc_ref, run_ref, *, tq, q_off):
    qi = pl.program_id(2)
    n_pairs = q_ref.shape[1] // LANES
    lane = lax.broadcasted_iota(jnp.int32, (tq, LANES), 1)
    row_pos = q_off + qi * tq + lax.broadcasted_iota(jnp.int32, (tq, LANES), 0)
    first = lane < HEAD_DIM
    blk0 = (q_off + qi * tq) // SB_KEY_BLOCK
    tt = tt_ref[...]
    heads = range(2 * n_pairs)
    q_heads = []
    for pair in range(n_pairs):
        q2 = q_ref[:, _lane_tile(pair)]
        zero = jnp.zeros_like(q2)
        q_heads += [jnp.where(first, q2, zero), jnp.where(first, zero, q2)]
    acc_ref[...] = jnp.zeros(acc_ref.shape, F32)
    run_ref[...] = jnp.zeros(run_ref.shape, F32)

    def span(first_block, n_blocks, own_block):
        n_halves = 2 * n_blocks
        width = n_halves * LANES
        start = first_block * SB_KEY_BLOCK
        if not isinstance(start, int):
            start = pl.multiple_of(start, SB_KEY_BLOCK)
        kblk = [k_ref[pl.ds(start, width), _lane_tile(pair)] for pair in range(n_pairs)]
        vblk = [v_ref[pl.ds(start, width), _lane_tile(pair)] for pair in range(n_pairs)]
        s = [_dot_nt(q_heads[h], kblk[h // 2]) for h in heads]
        soft = [jnp.log(1.0 + jnp.exp2(-jnp.abs(s[h]))) * LOG2E for h in heads]
        log_beta = [jnp.minimum(s[h], 0.0) - soft[h] for h in heads]
        log_keep = [log_beta[h] - s[h] for h in heads]
        masks = {}
        if own_block:
            for i in (n_halves - 2, n_halves - 1):
                masks[i] = (first_block * SB_KEY_BLOCK + i * LANES + lane) < row_pos
        cs = []
        for h in heads:
            per_half = []
            for i in range(n_halves):
                keep = log_keep[h][:, _lane_tile(i)]
                if i in masks:
                    keep = jnp.where(masks[i], keep, 0.0)
                hi = keep.astype(BF16)
                lo = (keep - hi.astype(F32)).astype(BF16)
                per_half.append(_dot(jnp.concatenate([hi, lo], axis=1), tt))
            cs.append(per_half)
        for h in heads:
            run = run_ref[h]
            w = [None] * n_halves
            for i in reversed(range(n_halves)):
                w_i = jnp.exp2(log_beta[h][:, _lane_tile(i)] + cs[h][i][:, :LANES] + run)
                if i in masks:
                    w_i = jnp.where(masks[i], w_i, 0.0)
                w[i] = w_i.astype(BF16)
                run = run + cs[h][i][:, LANES:]
            acc_ref[h] += _dot(jnp.concatenate(w, axis=1), vblk[h // 2])
            run_ref[h] = run

    @pl.when(blk0 == 0)
    def _():
        span(0, 1, True)

    @pl.when(blk0 > 0)
    def _():
        span(blk0 - 1, 2, True)

    def cond(carry):
        i, live = carry
        return jnp.logical_and(i < blk0 - 1, live > SB_EXIT_LOG2)

    def live_mass():
        return jnp.max(functools.reduce(jnp.maximum, [run_ref[h] for h in heads]))

    def body(carry):
        i, _ = carry
        span(blk0 - 2 - i, 1, False)
        return i + 1, live_mass()

    lax.while_loop(cond, body, (jnp.int32(0), live_mass()))
    for pair in range(n_pairs):
        o_ref[:, _lane_tile(pair)] = jnp.where(first, acc_ref[2 * pair],
                                               acc_ref[2 * pair + 1]).astype(o_ref.dtype)


def _suffix_matrix():
    r = lax.broadcasted_iota(jnp.int32, (2 * LANES, 2 * LANES), 0) % LANES
    c = lax.broadcasted_iota(jnp.int32, (2 * LANES, 2 * LANES), 1)
    return jnp.where((c >= LANES) | (r > c), 1.0, 0.0).astype(BF16)


def _keys_from_past(kernel_fn, n_before, past_rows):
    def wrapped(*refs):
        past_new = refs[n_before:n_before + 4]
        k_buf, v_buf = refs[-2:]
        p = past_rows
        for past, new, buf in ((past_new[0], past_new[2], k_buf), (past_new[1], past_new[3], v_buf)):
            t = new.shape[0]
            if past.shape[1] == p:
                buf[0:p, :] = past[...].T.astype(buf.dtype)
            else:
                heads = past.shape[0] // p
                for hd in range(heads):
                    buf[0:p, _lane_tile(hd)] = past[pl.ds(hd, p, stride=heads), :].astype(buf.dtype)
            buf[p:p + t, :] = new[...]
            buf[p + t:, :] = jnp.zeros((buf.shape[0] - p - t, buf.shape[1]), buf.dtype)
        kernel_fn(*refs[:n_before], k_buf, v_buf, *refs[n_before + 4:-2])
    return wrapped


def _cache_view(cache, layer):
    n_layers, b, p = cache.shape[:3]
    if cache.shape[-1] == LANES:
        return cache.reshape(n_layers, b, -1, LANES), layer, p
    perm = (0, 1) + tuple(range(3, cache.ndim)) + (2,)
    return jnp.transpose(cache, perm).reshape(n_layers, b, -1, p), layer, p


def _key_operands(kernel_fn, n_before, k, v, past, total, tiles=1):
    index = lambda bi, hp, qi: (bi, 0, hp)
    spec = lambda a: pl.BlockSpec((None, a.shape[1], tiles * LANES), index)
    if past is None:
        return kernel_fn, k.shape[1], [k, v], [spec(k), spec(v)], []
    assert tiles * LANES == k.shape[2]
    past_specs = [pl.BlockSpec((None, None) + a.shape[2:], lambda bi, hp, qi, layer=layer: (layer, bi, 0, 0))
                  for a, layer, _ in past]
    buf = pltpu.VMEM((total, tiles * LANES), BF16)
    operands = [past[0][0], past[1][0], k, v]
    return (_keys_from_past(kernel_fn, n_before, past[0][2]), total, operands,
            past_specs + [spec(k), spec(v)], [buf, buf])


def _sb_attn(q, k, v, tq, q_off, past=None, total=None, tiles=1):
    b, t, w = q.shape
    body, keys, key_args, key_specs, key_scratch = _key_operands(
        functools.partial(_sb_kernel, tq=tq, q_off=q_off), 1, k, v, past, total, tiles)
    assert q_off % SB_KEY_BLOCK == 0 and (tq == SB_KEY_BLOCK or (t == tq and tq < SB_KEY_BLOCK))
    assert -(-(q_off + t) // SB_KEY_BLOCK) * SB_KEY_BLOCK <= keys and w % (tiles * LANES) == 0
    tt = _suffix_matrix()
    qspec = pl.BlockSpec((None, tq, tiles * LANES), lambda bi, hp, qi: (bi, qi, hp))
    state = pltpu.VMEM((2 * tiles, tq, LANES), F32)
    return pl.pallas_call(
        body,
        grid=(b, w // (tiles * LANES), t // tq),
        in_specs=[qspec] + key_specs + [pl.BlockSpec(tt.shape, lambda bi, hp, qi: (0, 0))],
        out_specs=qspec,
        out_shape=jax.ShapeDtypeStruct((b, t, w), BF16),
        scratch_shapes=[state, state] + key_scratch,
        compiler_params=_params(3),
    )(q, *key_args, tt)


def _diff_kernel(lam_ref, gain_ref, q_ref, k_ref, v_ref, o_ref, m_ref, l_ref, acc_ref,
                 *, tq, tk, q_off, kv_len, n_diag, lam_init):
    qi = pl.program_id(2)
    n_heads = q_ref.shape[1] // LANES
    lane = lax.broadcasted_iota(jnp.int32, (tq, LANES), 1)
    first = lane < HEAD_DIM
    q_maps = []
    for hd in range(n_heads):
        q2 = q_ref[:, _lane_tile(hd)]
        zero = jnp.zeros_like(q2)
        q_maps += [jnp.where(first, q2, zero), jnp.where(first, zero, q2)]
    diag_rows = min(tq, tk)
    row_chunk = lax.broadcasted_iota(jnp.int32, (diag_rows, tk), 0) // CHUNK
    col = lax.broadcasted_iota(jnp.int32, (diag_rows, tk), 1)
    diag_mask = (col // CHUNK) <= row_chunk
    blk0 = (q_off + qi * tq) // tk
    ones = jnp.ones((tk, LANES), BF16)

    m_ref[...] = jnp.full(m_ref.shape, NEG, F32)
    l_ref[...] = jnp.zeros(l_ref.shape, F32)
    acc_ref[...] = jnp.zeros(acc_ref.shape, F32)

    def block(kb, masked, rows=slice(0, tq)):
        start = pl.multiple_of(kb * tk, tk)
        kblk = [k_ref[pl.ds(start, tk), _lane_tile(hd)] for hd in range(n_heads)]
        vext = [jnp.concatenate([v_ref[pl.ds(start, tk), _lane_tile(hd)], ones], axis=1)
                for hd in range(n_heads)]
        if masked:
            mask = diag_mask
            if kv_len % tk:
                mask = mask & (kb * tk + col < kv_len)
        maps = range(2 * n_heads)
        slabs = range(tk // LANES)
        s = [_dot_nt(q_maps[mp][rows], kblk[mp // 2]) for mp in maps]
        if masked:
            s = [jnp.where(mask, s[mp], NEG) for mp in maps]
        m_old = [m_ref[mp, rows, :] for mp in maps]
        m_new = []
        for mp in maps:
            mx = s[mp][:, :LANES]
            for j in slabs[1:]:
                mx = jnp.maximum(mx, s[mp][:, j * LANES:(j + 1) * LANES])
            m_new.append(jnp.maximum(m_old[mp], jnp.max(mx, axis=-1, keepdims=True)))
        alpha = [jnp.exp2(m_old[mp] - m_new[mp]) for mp in maps]
        p = [jnp.concatenate([jnp.exp2(s[mp][:, j * LANES:(j + 1) * LANES] - m_new[mp]).astype(BF16)
                              for j in slabs], axis=1) for mp in maps]
        pv = [_dot(p[mp], vext[mp // 2]) for mp in maps]
        for mp in maps:
            acc_ref[mp, rows, :] = acc_ref[mp, rows, :] * alpha[mp] + pv[mp][:, :LANES]
            l_ref[mp, rows, :] = l_ref[mp, rows, :] * alpha[mp] + pv[mp][:, LANES:]
            m_ref[mp, rows, :] = m_new[mp]

    for r in range(n_diag):
        rows = slice(r * diag_rows, (r + 1) * diag_rows)
        block(blk0 + r, True, rows)
        for j in range(r):
            block(blk0 + j, False, rows)

    def body(i, carry):
        block(2 * i, False)
        block(2 * i + 1, False)
        return carry

    lax.fori_loop(0, blk0 // 2, body, 0)

    @pl.when(blk0 % 2 == 1)
    def _():
        block(blk0 - 1, False)

    lp = lam_ref[...]
    lam = (jnp.exp(jnp.sum(lp[0:1] * lp[1:2], axis=-1, keepdims=True))
           - jnp.exp(jnp.sum(lp[2:3] * lp[3:4], axis=-1, keepdims=True)) + lam_init)
    for hd in range(n_heads):
        o = acc_ref[2 * hd] / l_ref[2 * hd] - lam * (acc_ref[2 * hd + 1] / l_ref[2 * hd + 1])
        o = o * lax.rsqrt(jnp.mean(o * o, axis=-1, keepdims=True) + EPS)
        o_ref[:, _lane_tile(hd)] = (o * gain_ref[hd] * (1.0 - lam_init)).astype(o_ref.dtype)


def _diff_attn(q, k, v, lam_params, gain, tq, tk, q_off, kv_len, lam_init, past=None, total=None,
               tiles=1):
    b, t, w = q.shape
    n_diag = max(1, tq // tk)
    body, keys, key_args, key_specs, key_scratch = _key_operands(
        functools.partial(_diff_kernel, tq=tq, tk=tk, q_off=q_off, kv_len=kv_len, n_diag=n_diag,
                          lam_init=lam_init), 3, k, v, past, total, tiles)
    assert q_off % tk == 0 and (tq % tk == 0 or t == tq) and keys % tk == 0 and CHUNK % 8 == 0
    assert (q_off + t + tk - 1) // tk * tk <= keys and w % (tiles * LANES) == 0
    gain3 = gain.reshape(gain.shape[0], 1, gain.shape[1])
    qspec = pl.BlockSpec((None, tq, tiles * LANES), lambda bi, hd, qi: (bi, qi, hd))
    return pl.pallas_call(
        body,
        grid=(b, w // (tiles * LANES), t // tq),
        in_specs=[pl.BlockSpec(lam_params.shape, lambda bi, hd, qi: (0, 0)),
                  pl.BlockSpec((tiles, 1, LANES), lambda bi, hd, qi: (hd, 0, 0)),
                  qspec] + key_specs,
        out_specs=qspec,
        out_shape=jax.ShapeDtypeStruct((b, t, w), BF16),
        scratch_shapes=[pltpu.VMEM((2 * tiles, tq, LANES), F32)] * 3 + key_scratch,
        compiler_params=_params(3),
    )(lam_params, gain3, q, *key_args)


def _band_kernel(q_ref, k_ref, v_ref, bias_ref, o_ref, *, rows, n_sub, valid_lo, valid_hi):
    qi = pl.program_id(2)
    n_pairs = q_ref.shape[1] // LANES
    lane = lax.broadcasted_iota(jnp.int32, (rows, LANES), 1)
    first = lane < HEAD_DIM
    ones = jnp.ones((BAND_WINDOW, LANES), BF16)
    kcol = lax.broadcasted_iota(jnp.int32, (1, BAND_WINDOW), 1)

    def sub(j, carry):
        r0 = pl.multiple_of(j * rows, rows)
        ws = pl.multiple_of((qi * n_sub + j) * rows, rows)
        krow = ws + kcol
        penalty = jnp.where((krow >= valid_lo) & (krow < valid_hi), 0.0, NEG)
        for pair in range(n_pairs):
            q2 = q_ref[pl.ds(r0, rows), _lane_tile(pair)]
            zero = jnp.zeros_like(q2)
            kw = k_ref[pl.ds(ws, BAND_WINDOW), _lane_tile(pair)]
            vext = jnp.concatenate([v_ref[pl.ds(ws, BAND_WINDOW), _lane_tile(pair)], ones], axis=1)
            outs = []
            for head in range(2):
                qh = jnp.where(first if head == 0 else jnp.logical_not(first), q2, zero)
                s = _dot_nt(qh, kw) + bias_ref[2 * pair + head] + penalty
                p = jnp.exp2(s - jnp.max(s, axis=-1, keepdims=True))
                pv = _dot(p.astype(BF16), vext)
                outs.append(pv[:, :LANES] / pv[:, LANES:])
            o_ref[pl.ds(r0, rows), _lane_tile(pair)] = jnp.where(first, outs[0], outs[1]).astype(o_ref.dtype)
        return carry

    lax.fori_loop(0, n_sub, sub, 0, unroll=BAND_UNROLL if n_sub % BAND_UNROLL == 0 else 1)


def _band_bias_table(rel_bias, rows):
    heads = rel_bias.shape[0]
    span = BAND_WINDOW + rows - 1
    n_far = BAND_PAST - REL_CLIP + rows
    rb = rel_bias.astype(F32) * LOG2E
    near = rb[:, 2 * REL_CLIP - 1:0:-1]
    assert n_far + near.shape[1] == span
    e = jnp.concatenate([jnp.broadcast_to(rb[:, -1:], (heads, n_far)), near,
                         jnp.zeros((heads, 1), F32)], axis=1)
    skew = jnp.broadcast_to(e[:, None, :], (heads, rows, span + 1)).reshape(heads, rows * (span + 1))
    skew = skew[:, :rows * span].reshape(heads, rows, span)
    bias = skew[:, :, rows - 1:]
    i = jnp.arange(rows, dtype=jnp.int32)[:, None]
    rel_key = jnp.arange(BAND_WINDOW, dtype=jnp.int32)[None, :] - BAND_PAST
    kc = jnp.floor_divide(rel_key, CHUNK)
    qc = i // CHUNK
    in_band = (kc <= qc) & (kc >= qc - BAND_PAST // CHUNK)
    return jnp.where(in_band[None], bias, NEG)


def _band_attn(q, k, v, bias_tab, rows, n_sub, valid_lo, valid_hi, past=None, total=None, tiles=1):
    b, t, w = q.shape
    tq = rows * n_sub
    body, keys, key_args, key_specs, key_scratch = _key_operands(
        functools.partial(_band_kernel, rows=rows, n_sub=n_sub, valid_lo=valid_lo, valid_hi=valid_hi),
        1, k, v, past, total, tiles)
    assert t % tq == 0 and (t - rows) + BAND_WINDOW <= keys and w % (tiles * LANES) == 0
    qspec = pl.BlockSpec((None, tq, tiles * LANES), lambda bi, hp, qi: (bi, qi, hp))
    return pl.pallas_call(
        body,
        grid=(b, w // (tiles * LANES), t // tq),
        in_specs=[qspec] + key_specs +
                 [pl.BlockSpec((2 * tiles, rows, BAND_WINDOW), lambda bi, hp, qi: (hp, 0, 0))],
        out_specs=qspec,
        out_shape=jax.ShapeDtypeStruct((b, t, w), BF16),
        scratch_shapes=key_scratch,
        compiler_params=_params(3),
    )(q, *key_args, bias_tab)


def _rope_tables(pos, reps):
    half = HEAD_DIM // 2
    inv = ROPE_THETA ** (-jnp.arange(half, dtype=F32) / half)
    ang = pos.astype(F32)[:, None] * inv[None, :]
    cos, sin = jnp.cos(ang), jnp.sin(ang)
    cos = jnp.tile(cos, (reps, LANES // half))
    sin_signed = jnp.tile(jnp.concatenate([-sin, sin], axis=1), (reps, LANES // HEAD_DIM))
    return cos, sin_signed


def _trunk(x, p, past, weights, cfg):
    (norm_mix, w_in_even, w_out_even, diff_lambda, diff_norm, w_in_odd, w_out_odd, rel_bias,
     norm_ffn, w_gate, w_up, w_down, norm_ple, w_ple_gate, w_ple, norm_final) = weights
    b, t, d = x.shape
    n = b * t
    depth = norm_mix.shape[0]
    tm = cfg["tm"]
    q_off = 0 if past is None else past[0].shape[2]
    pos = q_off + jnp.arange(t, dtype=jnp.int32)
    if past is None:
        cos, sin_signed = _rope_tables(pos, 1)
    else:
        cos, sin_signed = _rope_tables(pos, tm // t)
    row = lambda a: a.reshape(1, -1)
    seq = lambda a: a.reshape(b, t, a.shape[-1])
    tiles_of = lambda a: a.shape[-1] // LANES if cfg["whole_width"] else 1

    h = x.reshape(n, d)
    n_even, n_odd = w_in_even.shape[0], w_in_odd.shape[0]
    even_state = None
    odd_state = None
    for li in range(depth):
        if li % 2 == 0:
            e = li // 2
            qa, ka, kab, va, vab, qb, kb, kbb, vb, vbb = _pre_even(
                h, row(norm_mix[li]), w_in_even[e], cos, sin_signed, tm, e, n_even, even_state,
                batch=b if past is None else None)
            even_state = (ka, va, kb, vb)
            lam_init = 0.8 - 0.6 * math.exp(-0.3 * li)
            keys = [seq(a) for a in (kab, vab, kbb, vbb)]
            if past is None:
                sb_past = df_past = total = None
                kv_len = t
            else:
                total = cfg["keys_total"]
                sb_past = (_cache_view(past[0], e), _cache_view(past[1], e))
                df_past = (_cache_view(past[2], e), _cache_view(past[3], e))
                kv_len = q_off + t
            o_a = _sb_attn(seq(qa), keys[0], keys[1], cfg["sb_tq"], q_off, sb_past, total,
                           qa.shape[-1] // LANES if cfg["whole_width"] else cfg["sb_tiles"])
            o_b = _diff_attn(seq(qb), keys[2], keys[3], diff_lambda[e], diff_norm[e],
                             cfg["diff_tq"], cfg["diff_tk"], q_off, kv_len, lam_init, df_past, total,
                             tiles_of(qb))
            o_parts = [o_a.reshape(n, -1), o_b.reshape(n, -1)]
            w_out = w_out_even
        else:
            od = li // 2
            rows = cfg["band_rows"]
            bias_tab = _band_bias_table(rel_bias[od], rows)
            if past is None:
                assert t >= BAND_PAST
                q, k, kk, v, vv = _pre_odd(h, row(norm_mix[li]), w_in_odd[od], tm, od, n_odd,
                                           odd_state, batch=b)
                valid_lo, valid_hi = BAND_PAST, BAND_PAST + t
                bd_past = total = None
            else:
                q, k, kbf, v, vbf = _pre_odd(h, row(norm_mix[li]), w_in_odd[od], tm, od, n_odd,
                                             odd_state)
                kk, vv = seq(kbf), seq(vbf)
                bd_past = (_cache_view(past[4], od), _cache_view(past[5], od))
                cache_rows = bd_past[0][2]
                assert cache_rows == BAND_PAST
                valid_lo, valid_hi = 0, cache_rows + t
                total = BAND_WINDOW
            odd_state = (k, v)
            o = _band_attn(seq(q), kk, vv, bias_tab, rows, cfg["band_sub"], valid_lo, valid_hi,
                           bd_past, total, tiles_of(q))
            o_parts = [o.reshape(n, -1)]
            w_out = w_out_odd
        gfin = row(norm_final) if li == depth - 1 else None
        h = _post(h, o_parts, p.reshape(depth, n, -1), li, w_out, li // 2, row(norm_ffn[li]), w_gate,
                  w_up, w_down, row(norm_ple[li]), w_ple_gate, w_ple, gfin, tm)
    y = h.reshape(b, t, d)
    rows = lambda a, shp: a.reshape((a.shape[0], b, -1) + shp)

    def cols(a, shp):
        a = a.reshape(a.shape[:2] + shp + a.shape[3:])
        return jnp.moveaxis(a, -1, 2)

    heads = cols if past is None else rows
    sb_k, sb_v, df_k, df_v = even_state
    bd_k, bd_v = odd_state
    state = (heads(sb_k, (H_SB, HEAD_DIM)), heads(sb_v, (H_SB, HEAD_DIM)),
             heads(df_k, (H_DIFF, 2, HEAD_DIM)), rows(df_v, (H_DIFF, 2 * HEAD_DIM)),
             heads(bd_k, (H_BAND, HEAD_DIM)), heads(bd_v, (H_BAND, HEAD_DIM)))
    return y, state


def kernel(x_prompt, x_sample, cache_sb_k, cache_sb_v, cache_diff_k, cache_diff_v, cache_band_k, cache_band_v, p_prompt, p_sample, norm_mix, w_in_even, w_out_even, diff_lambda, diff_norm, w_in_odd, w_out_odd, rel_bias, norm_ffn, w_gate, w_up, w_down, norm_ple, w_ple_gate, w_ple, norm_final):
    bf = _to_bf16
    weights = (norm_mix, bf(w_in_even), bf(w_out_even), diff_lambda, diff_norm, bf(w_in_odd),
               bf(w_out_odd), rel_bias, norm_ffn, bf(w_gate), bf(w_up), bf(w_down), norm_ple,
               bf(w_ple_gate), bf(w_ple), norm_final)
    t_p = x_prompt.shape[1]
    t_s = x_sample.shape[1]
    past_len = cache_sb_k.shape[2]
    diff_tk = 512
    cfg_p = dict(tm=512, sb_tq=256, sb_tiles=2, diff_tq=min(1024, t_p), diff_tk=diff_tk, band_rows=LANES,
                 band_sub=min(8, t_p // LANES), whole_width=False)
    keys_total = -(-(past_len + t_s) // diff_tk) * diff_tk
    cfg_s = dict(tm=min(512, x_sample.shape[0] * t_s), sb_tq=t_s, diff_tq=t_s, diff_tk=diff_tk, band_rows=t_s, band_sub=1,
                 keys_total=keys_total, whole_width=True)
    y_p, st_p = _trunk(x_prompt, p_prompt, None, weights, cfg_p)
    past = (cache_sb_k, cache_sb_v, cache_diff_k, cache_diff_v, cache_band_k, cache_band_v)
    y_s, st_s = _trunk(x_sample, p_sample, past, weights, cfg_s)
    return (y_p, y_s) + tuple(st_p) + tuple(st_s)
```

```python
import functools
import math

import jax
import jax.numpy as jnp
from jax import lax
from jax.experimental import pallas as pl
from jax.experimental.pallas import tpu as pltpu

CHUNK = 64
HEAD_DIM = 64
H_SB = 8
H_DIFF = 4
H_BAND = 16
BAND_PAST = 8 * CHUNK
REL_CLIP = 128
ROPE_THETA = 10000.0
EPS = 1e-6
NEG = -1e30
SCALE = HEAD_DIM ** -0.5
LOG2E = 1.4426950408889634
QSCALE = SCALE * LOG2E

LANES = 128
VMEM_LIMIT = 56 * 1024 * 1024

SB_KEY_BLOCK = 2 * LANES
SB_EXIT_LOG2 = -160.0
BAND_WINDOW = BAND_PAST + LANES
BAND_UNROLL = 8

F32 = jnp.float32
BF16 = jnp.bfloat16


def _rms(x, g):
    return x * lax.rsqrt(jnp.mean(x * x, axis=-1, keepdims=True) + EPS) * g


def _sigmoid(x):
    return 1.0 / (1.0 + jnp.exp(-x))


def _dot(a, b):
    return jnp.dot(a, b, preferred_element_type=F32)


def _dot_nt(a, b):
    return lax.dot_general(a, b, (((1,), (1,)), ((), ())), preferred_element_type=F32)


def _params(n_axes):
    return pltpu.CompilerParams(dimension_semantics=("arbitrary",) * n_axes,
                                vmem_limit_bytes=VMEM_LIMIT)


def _rope(x, cos, sin_signed, first_half):
    outs = []
    for j in range(x.shape[1] // LANES):
        xj = x[:, j * LANES:(j + 1) * LANES]
        partner = jnp.where(first_half, pltpu.roll(xj, LANES - HEAD_DIM // 2, 1),
                            pltpu.roll(xj, HEAD_DIM // 2, 1))
        outs.append(xj * cos + partner * sin_signed)
    return jnp.concatenate(outs, axis=1)


def _pre_even_kernel(h_ref, g_ref, w_ref, cos_ref, sin_ref, *rest, transposed):
    flip = (lambda a: a.T) if transposed else (lambda a: a)
    (qa_ref, ka_ref, kab_ref, va_ref, vab_ref, qb_ref, kb_ref, kbb_ref, vb_ref, vbb_ref) = rest[-10:]
    hn = _rms(h_ref[...], g_ref[...]).astype(BF16)
    width = qa_ref.shape[1]

    def proj(c):
        return _dot(hn, w_ref[:, c * width:(c + 1) * width])

    cos = cos_ref[...]
    sin_signed = sin_ref[...]
    lane = lax.broadcasted_iota(jnp.int32, cos.shape, 1)
    first_half = (lane % HEAD_DIM) < (HEAD_DIM // 2)

    qa_ref[...] = (proj(0) * QSCALE).astype(BF16)
    ka = proj(1)
    ka_ref[...] = flip(ka)
    kab_ref[...] = ka.astype(BF16)
    va = proj(2)
    va_ref[...] = flip(va)
    vab_ref[...] = va.astype(BF16)
    qb_ref[...] = (_rope(proj(3), cos, sin_signed, first_half) * QSCALE).astype(BF16)
    kb = _rope(proj(4), cos, sin_signed, first_half)
    kb_ref[...] = flip(kb)
    kbb_ref[...] = kb.astype(BF16)
    vb = proj(5)
    for hd in range(H_DIFF):
        vb_ref[pl.ds(hd, vb.shape[0], stride=H_DIFF), :] = vb[:, _lane_tile(hd)]
    vbb_ref[...] = vb.astype(BF16)


def _pre_odd_kernel(h_ref, g_ref, w_ref, *rest, padded):
    q_ref, k_ref, kb_ref, v_ref, vb_ref = rest[-5:]
    width = q_ref.shape[1]

    def project(keep_f32):
        hn = _rms(h_ref[...], g_ref[...]).astype(BF16)
        q_ref[...] = (_dot(hn, w_ref[:, :width]) * QSCALE).astype(BF16)
        k = _dot(hn, w_ref[:, width:2 * width])
        kb_ref[...] = k.astype(BF16)
        v = _dot(hn, w_ref[:, 2 * width:])
        vb_ref[...] = v.astype(BF16)
        keep_f32(k, v)

    def store_f32(k, v):
        k_ref[...] = k.T if padded else k
        v_ref[...] = v.T if padded else v

    if not padded:
        project(store_f32)
        return

    j = pl.program_id(1)

    @pl.when(j == 0)
    def _():
        kb_ref[...] = jnp.zeros(kb_ref.shape, BF16)
        vb_ref[...] = jnp.zeros(vb_ref.shape, BF16)

    @pl.when(j > 0)
    def _():
        project(lambda k, v: pl.when(j == pl.num_programs(1) - 1)(lambda: store_f32(k, v)))


def _stack_alias(prev, n_fixed_inputs, out_positions):
    if prev is None:
        return [], [], {}
    specs = [pl.BlockSpec(memory_space=pl.ANY)] * len(prev)
    aliases = {n_fixed_inputs + i: o for i, o in enumerate(out_positions)}
    return list(prev), specs, aliases


def _pre_even(h, g, w, cos, sin_signed, tm, slot, n_slots, prev, batch=None):
    n, d = h.shape
    width = w.shape[1] // 6
    n_pos_blocks = cos.shape[0] // tm
    tok = lambda wd: pl.BlockSpec((tm, wd), lambda i: (i, 0))
    const = lambda a: pl.BlockSpec(a.shape, lambda i: (0, 0))
    pos = pl.BlockSpec((tm, LANES), lambda i: (i % n_pos_blocks, 0))
    stk = pl.BlockSpec((None, tm, width), lambda i: (slot, i, 0))
    f32s = jax.ShapeDtypeStruct((n_slots, n, width), F32)
    bf16o = jax.ShapeDtypeStruct((n, width), BF16)
    t, s = tok(width), stk
    st, f32t = s, f32s
    assert width == H_DIFF * LANES
    sv = pl.BlockSpec((None, tm * H_DIFF, LANES), lambda i: (slot, i, 0))
    f32v = jax.ShapeDtypeStruct((n_slots, n * H_DIFF, LANES), F32)
    if batch is not None:
        tiles = n // batch // tm
        st = pl.BlockSpec((None, None, width, tm), lambda i: (slot, i // tiles, 0, i % tiles))
        f32t = jax.ShapeDtypeStruct((n_slots, batch, width, n // batch), F32)
    prev_args, prev_specs, aliases = _stack_alias(prev, 5, (1, 3, 6, 8))
    return pl.pallas_call(
        functools.partial(_pre_even_kernel, transposed=batch is not None),
        grid=(n // tm,),
        in_specs=[tok(d), const(g), const(w), pos, pos] + prev_specs,
        out_specs=[t, st, t, st, t, t, st, t, sv, t],
        out_shape=[bf16o, f32t, bf16o, f32t, bf16o, bf16o, f32t, bf16o, f32v, bf16o],
        input_output_aliases=aliases,
        compiler_params=_params(1),
    )(h, g, w, cos, sin_signed, *prev_args)


def _pre_odd(h, g, w, tm, slot, n_slots, prev, batch=None):
    n, d = h.shape
    width = w.shape[1] // 3
    padded = batch is not None
    bf16o = jax.ShapeDtypeStruct((n, width), BF16)
    if padded:
        assert tm == BAND_PAST
        tiles = n // batch // tm
        grid = (batch, tiles + 1)
        tile = lambda bi, j: bi * tiles + jnp.maximum(j - 1, 0)
        tok = lambda wd: pl.BlockSpec((tm, wd), lambda bi, j: (tile(bi, j), 0))
        const = lambda a: pl.BlockSpec(a.shape, lambda bi, j: (0, 0))
        stk = pl.BlockSpec((None, None, width, tm), lambda bi, j: (slot, bi, 0, 0))
        pad = pl.BlockSpec((None, tm, width), lambda bi, j: (bi, j, 0))
        f32s = jax.ShapeDtypeStruct((n_slots, batch, width, tm), F32)
        bf16p = jax.ShapeDtypeStruct((batch, (tiles + 1) * tm, width), BF16)
        out_specs, out_shape = [tok(width), stk, pad, stk, pad], [bf16o, f32s, bf16p, f32s, bf16p]
    else:
        grid = (n // tm,)
        tok = lambda wd: pl.BlockSpec((tm, wd), lambda i: (i, 0))
        const = lambda a: pl.BlockSpec(a.shape, lambda i: (0, 0))
        stk = pl.BlockSpec((None, tm, width), lambda i: (slot, i, 0))
        f32s = jax.ShapeDtypeStruct((n_slots, n, width), F32)
        out_specs, out_shape = [tok(width), stk, tok(width), stk, tok(width)], [bf16o, f32s, bf16o, f32s, bf16o]
    prev_args, prev_specs, aliases = _stack_alias(prev, 3, (1, 3))
    return pl.pallas_call(
        functools.partial(_pre_odd_kernel, padded=padded),
        grid=grid,
        in_specs=[tok(d), const(g), const(w)] + prev_specs,
        out_specs=out_specs,
        out_shape=out_shape,
        input_output_aliases=aliases,
        compiler_params=_params(len(grid)),
    )(h, g, w, *prev_args)


def _cast_kernel(x_ref, o_ref):
    o_ref[...] = x_ref[...].astype(o_ref.dtype)


def _to_bf16(w):
    layers, rows, cols = w.shape
    tr = rows // 4
    assert rows % 4 == 0 and tr % 16 == 0
    spec = pl.BlockSpec((None, tr, cols), lambda i, j: (i, j, 0))
    return pl.pallas_call(
        _cast_kernel,
        grid=(layers, rows // tr),
        in_specs=[spec],
        out_specs=spec,
        out_shape=jax.ShapeDtypeStruct(w.shape, BF16),
        compiler_params=_params(2),
    )(w)


def _post_kernel(*refs, n_o, ffn_chunk, final):
    h_ref = refs[0]
    o_refs = refs[1:1 + n_o]
    (p_ref, wout_ref, gffn_ref, wg_ref, wu_ref, wd_ref, gple_ref, wpg_ref, wp_ref) = refs[1 + n_o:10 + n_o]
    gfin_ref = refs[10 + n_o] if final else None
    out_ref = refs[-1]

    h = h_ref[...]
    off = 0
    for o_ref in o_refs:
        wd = o_ref.shape[1]
        h = h + _dot(o_ref[...], wout_ref[off:off + wd, :])
        off += wd

    hn = _rms(h, gffn_ref[...]).astype(BF16)
    hidden = wg_ref.shape[1]
    for c in range(hidden // ffn_chunk):
        cols = slice(c * ffn_chunk, (c + 1) * ffn_chunk)
        gt = _dot(hn, wg_ref[:, cols])
        up = _dot(hn, wu_ref[:, cols])
        act = (gt * _sigmoid(gt) * up).astype(BF16)
        h = h + _dot(act, wd_ref[cols, :])

    gate = _sigmoid(_dot(_rms(h, gple_ref[...]).astype(BF16), wpg_ref[...]))
    h = h + _dot(p_ref[...].astype(BF16), wp_ref[...]) * gate
    if final:
        h = _rms(h, gfin_ref[...])
    out_ref[...] = h


def _post(h, o_parts, p, layer, wout, wout_layer, gffn, wg, wu, wd, gple, wpg, wp, gfin, tm):
    n, d = h.shape
    final = gfin is not None
    hidden = wg.shape[2]
    ffn_chunk = hidden
    tok = lambda a: pl.BlockSpec((tm, a.shape[1]), lambda i: (i, 0))
    row = lambda a: pl.BlockSpec(a.shape, lambda i: (0, 0), pipeline_mode=pl.Buffered(1))
    mat = lambda a, li: pl.BlockSpec((None,) + a.shape[1:], lambda i: (li, 0, 0),
                                     pipeline_mode=pl.Buffered(1))
    consts = [wout, gffn, wg, wu, wd, gple, wpg, wp] + ([gfin] if final else [])
    const_specs = ([mat(wout, wout_layer), row(gffn)] + [mat(a, layer) for a in (wg, wu, wd)]
                   + [row(gple), mat(wpg, layer), mat(wp, layer)] + ([row(gfin)] if final else []))
    p_spec = pl.BlockSpec((None, tm, p.shape[2]), lambda i: (layer, i, 0))
    return pl.pallas_call(
        functools.partial(_post_kernel, n_o=len(o_parts), ffn_chunk=ffn_chunk, final=final),
        grid=(n // tm,),
        in_specs=[tok(h)] + [tok(o) for o in o_parts] + [p_spec] + const_specs,
        out_specs=tok(h),
        out_shape=jax.ShapeDtypeStruct((n, d), F32),
        compiler_params=_params(1),
    )(h, *o_parts, p, *consts)


def _lane_tile(i):
    return slice(i * LANES, (i + 1) * LANES)


def _sb_kernel(q_ref, k_ref, v_ref, tt_ref, o_ref, acc_ref, run_ref, *, tq, q_off):
    qi = pl.program_id(2)
    n_pairs = q_ref.shape[1] // LANES
    lane = lax.broadcasted_iota(jnp.int32, (tq, LANES), 1)
    row_pos = q_off + qi * tq + lax.broadcasted_iota(jnp.int32, (tq, LANES), 0)
    first = lane < HEAD_DIM
    blk0 = (q_off + qi * tq) // SB_KEY_BLOCK
    tt = tt_ref[...]
    heads = range(2 * n_pairs)
    q_heads = []
    for pair in range(n_pairs):
        q2 = q_ref[:, _lane_tile(pair)]
        zero = jnp.zeros_like(q2)
        q_heads += [jnp.where(first, q2, zero), jnp.where(first, zero, q2)]
    acc_ref[...] = jnp.zeros(acc_ref.shape, F32)
    run_ref[...] = jnp.zeros(run_ref.shape, F32)

    def span(first_block, n_blocks, own_block):
        n_halves = 2 * n_blocks
        width = n_halves * LANES
        start = first_block * SB_KEY_BLOCK
        if not isinstance(start, int):
            start = pl.multiple_of(start, SB_KEY_BLOCK)
        kblk = [k_ref[pl.ds(start, width), _lane_tile(pair)] for pair in range(n_pairs)]
        vblk = [v_ref[pl.ds(start, width), _lane_tile(pair)] for pair in range(n_pairs)]
        s = [_dot_nt(q_heads[h], kblk[h // 2]) for h in heads]
        soft = [jnp.log(1.0 + jnp.exp2(-jnp.abs(s[h]))) * LOG2E for h in heads]
        log_beta = [jnp.minimum(s[h], 0.0) - soft[h] for h in heads]
        log_keep = [log_beta[h] - s[h] for h in heads]
        masks = {}
        if own_block:
            for i in (n_halves - 2, n_halves - 1):
                masks[i] = (first_block * SB_KEY_BLOCK + i * LANES + lane) < row_pos
        cs = []
        for h in heads:
            per_half = []
            for i in range(n_halves):
                keep = log_keep[h][:, _lane_tile(i)]
                if i in masks:
                    keep = jnp.where(masks[i], keep, 0.0)
                hi = keep.astype(BF16)
                lo = (keep - hi.astype(F32)).astype(BF16)
                per_half.append(_dot(jnp.concatenate([hi, lo], axis=1), tt))
            cs.append(per_half)
        for h in heads:
            run = run_ref[h]
            w = [None] * n_halves
            for i in reversed(range(n_halves)):
                w_i = jnp.exp2(log_beta[h][:, _lane_tile(i)] + cs[h][i][:, :LANES] + run)
                if i in masks:
                    w_i = jnp.where(masks[i], w_i, 0.0)
                w[i] = w_i.astype(BF16)
                run = run + cs[h][i][:, LANES:]
            acc_ref[h] += _dot(jnp.concatenate(w, axis=1), vblk[h // 2])
            run_ref[h] = run

    @pl.when(blk0 == 0)
    def _():
        span(0, 1, True)

    @pl.when(blk0 > 0)
    def _():
        span(blk0 - 1, 2, True)

    def cond(carry):
        i, live = carry
        return jnp.logical_and(i < blk0 - 1, live > SB_EXIT_LOG2)

    def live_mass():
        return jnp.max(functools.reduce(jnp.maximum, [run_ref[h] for h in heads]))

    def body(carry):
        i, _ = carry
        span(blk0 - 2 - i, 1, False)
        return i + 1, live_mass()

    lax.while_loop(cond, body, (jnp.int32(0), live_mass()))
    for pair in range(n_pairs):
        o_ref[:, _lane_tile(pair)] = jnp.where(first, acc_ref[2 * pair],
                                               acc_ref[2 * pair + 1]).astype(o_ref.dtype)


def _suffix_matrix():
    r = lax.broadcasted_iota(jnp.int32, (2 * LANES, 2 * LANES), 0) % LANES
    c = lax.broadcasted_iota(jnp.int32, (2 * LANES, 2 * LANES), 1)
    return jnp.where((c >= LANES) | (r > c), 1.0, 0.0).astype(BF16)


def _keys_from_past(kernel_fn, n_before, past_rows):
    def wrapped(*refs):
        past_new = refs[n_before:n_before + 4]
        k_buf, v_buf = refs[-2:]
        p = past_rows
        for past, new, buf in ((past_new[0], past_new[2], k_buf), (past_new[1], past_new[3], v_buf)):
            t = new.shape[0]
            if past.shape[1] == p:
                buf[0:p, :] = past[...].T.astype(buf.dtype)
            else:
                heads = past.shape[0] // p
                for hd in range(heads):
                    buf[0:p, _lane_tile(hd)] = past[pl.ds(hd, p, stride=heads), :].astype(buf.dtype)
            buf[p:p + t, :] = new[...]
            buf[p + t:, :] = jnp.zeros((buf.shape[0] - p - t, buf.shape[1]), buf.dtype)
        kernel_fn(*refs[:n_before], k_buf, v_buf, *refs[n_before + 4:-2])
    return wrapped


def _cache_view(cache, layer):
    n_layers, b, p = cache.shape[:3]
    if cache.shape[-1] == LANES:
        return cache.reshape(n_layers, b, -1, LANES), layer, p
    perm = (0, 1) + tuple(range(3, cache.ndim)) + (2,)
    return jnp.transpose(cache, perm).reshape(n_layers, b, -1, p), layer, p


def _key_operands(kernel_fn, n_before, k, v, past, total, tiles=1):
    index = lambda bi, hp, qi: (bi, 0, hp)
    spec = lambda a: pl.BlockSpec((None, a.shape[1], tiles * LANES), index)
    if past is None:
        return kernel_fn, k.shape[1], [k, v], [spec(k), spec(v)], []
    assert tiles * LANES == k.shape[2]
    past_specs = [pl.BlockSpec((None, None) + a.shape[2:], lambda bi, hp, qi, layer=layer: (layer, bi, 0, 0))
                  for a, layer, _ in past]
    buf = pltpu.VMEM((total, tiles * LANES), BF16)
    operands = [past[0][0], past[1][0], k, v]
    return (_keys_from_past(kernel_fn, n_before, past[0][2]), total, operands,
            past_specs + [spec(k), spec(v)], [buf, buf])


def _sb_attn(q, k, v, tq, q_off, past=None, total=None, tiles=1):
    b, t, w = q.shape
    body, keys, key_args, key_specs, key_scratch = _key_operands(
        functools.partial(_sb_kernel, tq=tq, q_off=q_off), 1, k, v, past, total, tiles)
    assert q_off % SB_KEY_BLOCK == 0 and (tq == SB_KEY_BLOCK or (t == tq and tq < SB_KEY_BLOCK))
    assert -(-(q_off + t) // SB_KEY_BLOCK) * SB_KEY_BLOCK <= keys and w % (tiles * LANES) == 0
    tt = _suffix_matrix()
    qspec = pl.BlockSpec((None, tq, tiles * LANES), lambda bi, hp, qi: (bi, qi, hp))
    state = pltpu.VMEM((2 * tiles, tq, LANES), F32)
    return pl.pallas_call(
        body,
        grid=(b, w // (tiles * LANES), t // tq),
        in_specs=[qspec] + key_specs + [pl.BlockSpec(tt.shape, lambda bi, hp, qi: (0, 0))],
        out_specs=qspec,
        out_shape=jax.ShapeDtypeStruct((b, t, w), BF16),
        scratch_shapes=[state, state] + key_scratch,
        compiler_params=_params(3),
    )(q, *key_args, tt)


def _diff_kernel(lam_ref, gain_ref, q_ref, k_ref, v_ref, o_ref, m_ref, l_ref, acc_ref,
                 *, tq, tk, q_off, kv_len, n_diag, lam_init):
    qi = pl.program_id(2)
    n_heads = q_ref.shape[1] // LANES
    lane = lax.broadcasted_iota(jnp.int32, (tq, LANES), 1)
    first = lane < HEAD_DIM
    q_maps = []
    for hd in range(n_heads):
        q2 = q_ref[:, _lane_tile(hd)]
        zero = jnp.zeros_like(q2)
        q_maps += [jnp.where(first, q2, zero), jnp.where(first, zero, q2)]
    diag_rows = min(tq, tk)
    row_chunk = lax.broadcasted_iota(jnp.int32, (diag_rows, tk), 0) // CHUNK
    col = lax.broadcasted_iota(jnp.int32, (diag_rows, tk), 1)
    diag_mask = (col // CHUNK) <= row_chunk
    blk0 = (q_off + qi * tq) // tk
    ones = jnp.ones((tk, LANES), BF16)

    m_ref[...] = jnp.full(m_ref.shape, NEG, F32)
    l_ref[...] = jnp.zeros(l_ref.shape, F32)
    acc_ref[...] = jnp.zeros(acc_ref.shape, F32)

    def block(kb, masked, rows=slice(0, tq)):
        start = pl.multiple_of(kb * tk, tk)
        kblk = [k_ref[pl.ds(start, tk), _lane_tile(hd)] for hd in range(n_heads)]
        vext = [jnp.concatenate([v_ref[pl.ds(start, tk), _lane_tile(hd)], ones], axis=1)
                for hd in range(n_heads)]
        if masked:
            mask = diag_mask
            if kv_len % tk:
                mask = mask & (kb * tk + col < kv_len)
        maps = range(2 * n_heads)
        slabs = range(tk // LANES)
        s = [_dot_nt(q_maps[mp][rows], kblk[mp // 2]) for mp in maps]
        if masked:
            s = [jnp.where(mask, s[mp], NEG) for mp in maps]
        m_old = [m_ref[mp, rows, :] for mp in maps]
        m_new = []
        for mp in maps:
            mx = s[mp][:, :LANES]
            for j in slabs[1:]:
                mx = jnp.maximum(mx, s[mp][:, j * LANES:(j + 1) * LANES])
            m_new.append(jnp.maximum(m_old[mp], jnp.max(mx, axis=-1, keepdims=True)))
        alpha = [jnp.exp2(m_old[mp] - m_new[mp]) for mp in maps]
        p = [jnp.concatenate([jnp.exp2(s[mp][:, j * LANES:(j + 1) * LANES] - m_new[mp]).astype(BF16)
                              for j in slabs], axis=1) for mp in maps]
        pv = [_dot(p[mp], vext[mp // 2]) for mp in maps]
        for mp in maps:
            acc_ref[mp, rows, :] = acc_ref[mp, rows, :] * alpha[mp] + pv[mp][:, :LANES]
            l_ref[mp, rows, :] = l_ref[mp, rows, :] * alpha[mp] + pv[mp][:, LANES:]
            m_ref[mp, rows, :] = m_new[mp]

    for r in range(n_diag):
        rows = slice(r * diag_rows, (r + 1) * diag_rows)
        block(blk0 + r, True, rows)
        for j in range(r):
            block(blk0 + j, False, rows)

    def body(i, carry):
        block(2 * i, False)
        block(2 * i + 1, False)
        return carry

    lax.fori_loop(0, blk0 // 2, body, 0)

    @pl.when(blk0 % 2 == 1)
    def _():
        block(blk0 - 1, False)

    lp = lam_ref[...]
    lam = (jnp.exp(jnp.sum(lp[0:1] * lp[1:2], axis=-1, keepdims=True))
           - jnp.exp(jnp.sum(lp[2:3] * lp[3:4], axis=-1, keepdims=True)) + lam_init)
    for hd in range(n_heads):
        o = acc_ref[2 * hd] / l_ref[2 * hd] - lam * (acc_ref[2 * hd + 1] / l_ref[2 * hd + 1])
        o = o * lax.rsqrt(jnp.mean(o * o, axis=-1, keepdims=True) + EPS)
        o_ref[:, _lane_tile(hd)] = (o * gain_ref[hd] * (1.0 - lam_init)).astype(o_ref.dtype)


def _diff_attn(q, k, v, lam_params, gain, tq, tk, q_off, kv_len, lam_init, past=None, total=None,
               tiles=1):
    b, t, w = q.shape
    n_diag = max(1, tq // tk)
    body, keys, key_args, key_specs, key_scratch = _key_operands(
        functools.partial(_diff_kernel, tq=tq, tk=tk, q_off=q_off, kv_len=kv_len, n_diag=n_diag,
                          lam_init=lam_init), 3, k, v, past, total, tiles)
    assert q_off % tk == 0 and (tq % tk == 0 or t == tq) and keys % tk == 0 and CHUNK % 8 == 0
    assert (q_off + t + tk - 1) // tk * tk <= keys and w % (tiles * LANES) == 0
    gain3 = gain.reshape(gain.shape[0], 1, gain.shape[1])
    qspec = pl.BlockSpec((None, tq, tiles * LANES), lambda bi, hd, qi: (bi, qi, hd))
    return pl.pallas_call(
        body,
        grid=(b, w // (tiles * LANES), t // tq),
        in_specs=[pl.BlockSpec(lam_params.shape, lambda bi, hd, qi: (0, 0)),
                  pl.BlockSpec((tiles, 1, LANES), lambda bi, hd, qi: (hd, 0, 0)),
                  qspec] + key_specs,
        out_specs=qspec,
        out_shape=jax.ShapeDtypeStruct((b, t, w), BF16),
        scratch_shapes=[pltpu.VMEM((2 * tiles, tq, LANES), F32)] * 3 + key_scratch,
        compiler_params=_params(3),
    )(lam_params, gain3, q, *key_args)


def _band_kernel(q_ref, k_ref, v_ref, bias_ref, o_ref, *, rows, n_sub, valid_lo, valid_hi):
    qi = pl.program_id(2)
    n_pairs = q_ref.shape[1] // LANES
    lane = lax.broadcasted_iota(jnp.int32, (rows, LANES), 1)
    first = lane < HEAD_DIM
    ones = jnp.ones((BAND_WINDOW, LANES), BF16)
    kcol = lax.broadcasted_iota(jnp.int32, (1, BAND_WINDOW), 1)

    def sub(j, carry):
        r0 = pl.multiple_of(j * rows, rows)
        ws = pl.multiple_of((qi * n_sub + j) * rows, rows)
        krow = ws + kcol
        penalty = jnp.where((krow >= valid_lo) & (krow < valid_hi), 0.0, NEG)
        for pair in range(n_pairs):
            q2 = q_ref[pl.ds(r0, rows), _lane_tile(pair)]
            zero = jnp.zeros_like(q2)
            kw = k_ref[pl.ds(ws, BAND_WINDOW), _lane_tile(pair)]
            vext = jnp.concatenate([v_ref[pl.ds(ws, BAND_WINDOW), _lane_tile(pair)], ones], axis=1)
            outs = []
            for head in range(2):
                qh = jnp.where(first if head == 0 else jnp.logical_not(first), q2, zero)
                s = _dot_nt(qh, kw) + bias_ref[2 * pair + head] + penalty
                p = jnp.exp2(s - jnp.max(s, axis=-1, keepdims=True))
                pv = _dot(p.astype(BF16), vext)
                outs.append(pv[:, :LANES] / pv[:, LANES:])
            o_ref[pl.ds(r0, rows), _lane_tile(pair)] = jnp.where(first, outs[0], outs[1]).astype(o_ref.dtype)
        return carry

    lax.fori_loop(0, n_sub, sub, 0, unroll=BAND_UNROLL if n_sub % BAND_UNROLL == 0 else 1)


def _band_bias_table(rel_bias, rows):
    heads = rel_bias.shape[0]
    span = BAND_WINDOW + rows - 1
    n_far = BAND_PAST - REL_CLIP + rows
    rb = rel_bias.astype(F32) * LOG2E
    near = rb[:, 2 * REL_CLIP - 1:0:-1]
    assert n_far + near.shape[1] == span
    e = jnp.concatenate([jnp.broadcast_to(rb[:, -1:], (heads, n_far)), near,
                         jnp.zeros((heads, 1), F32)], axis=1)
    skew = jnp.broadcast_to(e[:, None, :], (heads, rows, span + 1)).reshape(heads, rows * (span + 1))
    skew = skew[:, :rows * span].reshape(heads, rows, span)
    bias = skew[:, :, rows - 1:]
    i = jnp.arange(rows, dtype=jnp.int32)[:, None]
    rel_key = jnp.arange(BAND_WINDOW, dtype=jnp.int32)[None, :] - BAND_PAST
    kc = jnp.floor_divide(rel_key, CHUNK)
    qc = i // CHUNK
    in_band = (kc <= qc) & (kc >= qc - BAND_PAST // CHUNK)
    return jnp.where(in_band[None], bias, NEG)


def _band_attn(q, k, v, bias_tab, rows, n_sub, valid_lo, valid_hi, past=None, total=None, tiles=1):
    b, t, w = q.shape
    tq = rows * n_sub
    body, keys, key_args, key_specs, key_scratch = _key_operands(
        functools.partial(_band_kernel, rows=rows, n_sub=n_sub, valid_lo=valid_lo, valid_hi=valid_hi),
        1, k, v, past, total, tiles)
    assert t % tq == 0 and (t - rows) + BAND_WINDOW <= keys and w % (tiles * LANES) == 0
    qspec = pl.BlockSpec((None, tq, tiles * LANES), lambda bi, hp, qi: (bi, qi, hp))
    return pl.pallas_call(
        body,
        grid=(b, w // (tiles * LANES), t // tq),
        in_specs=[qspec] + key_specs +
                 [pl.BlockSpec((2 * tiles, rows, BAND_WINDOW), lambda bi, hp, qi: (hp, 0, 0))],
        out_specs=qspec,
        out_shape=jax.ShapeDtypeStruct((b, t, w), BF16),
        scratch_shapes=key_scratch,
        compiler_params=_params(3),
    )(q, *key_args, bias_tab)


def _rope_tables(pos, reps):
    half = HEAD_DIM // 2
    inv = ROPE_THETA ** (-jnp.arange(half, dtype=F32) / half)
    ang = pos.astype(F32)[:, None] * inv[None, :]
    cos, sin = jnp.cos(ang), jnp.sin(ang)
    cos = jnp.tile(cos, (reps, LANES // half))
    sin_signed = jnp.tile(jnp.concatenate([-sin, sin], axis=1), (reps, LANES // HEAD_DIM))
    return cos, sin_signed


def _trunk(x, p, past, weights, cfg):
    (norm_mix, w_in_even, w_out_even, diff_lambda, diff_norm, w_in_odd, w_out_odd, rel_bias,
     norm_ffn, w_gate, w_up, w_down, norm_ple, w_ple_gate, w_ple, norm_final) = weights
    b, t, d = x.shape
    n = b * t
    depth = norm_mix.shape[0]
    tm = cfg["tm"]
    q_off = 0 if past is None else past[0].shape[2]
    pos = q_off + jnp.arange(t, dtype=jnp.int32)
    if past is None:
        cos, sin_signed = _rope_tables(pos, 1)
    else:
        cos, sin_signed = _rope_tables(pos, tm // t)
    row = lambda a: a.reshape(1, -1)
    seq = lambda a: a.reshape(b, t, a.shape[-1])
    tiles_of = lambda a: a.shape[-1] // LANES if cfg["whole_width"] else 1

    h = x.reshape(n, d)
    n_even, n_odd = w_in_even.shape[0], w_in_odd.shape[0]
    even_state = None
    odd_state = None
    for li in range(depth):
        if li % 2 == 0:
            e = li // 2
            qa, ka, kab, va, vab, qb, kb, kbb, vb, vbb = _pre_even(
                h, row(norm_mix[li]), w_in_even[e], cos, sin_signed, tm, e, n_even, even_state,
                batch=b if past is None else None)
            even_state = (ka, va, kb, vb)
            lam_init = 0.8 - 0.6 * math.exp(-0.3 * li)
            keys = [seq(a) for a in (kab, vab, kbb, vbb)]
            if past is None:
                sb_past = df_past = total = None
                kv_len = t
            else:
                total = cfg["keys_total"]
                sb_past = (_cache_view(past[0], e), _cache_view(past[1], e))
                df_past = (_cache_view(past[2], e), _cache_view(past[3], e))
                kv_len = q_off + t
            o_a = _sb_attn(seq(qa), keys[0], keys[1], cfg["sb_tq"], q_off, sb_past, total,
                           qa.shape[-1] // LANES if cfg["whole_width"] else cfg["sb_tiles"])
            o_b = _diff_attn(seq(qb), keys[2], keys[3], diff_lambda[e], diff_norm[e],
                             cfg["diff_tq"], cfg["diff_tk"], q_off, kv_len, lam_init, df_past, total,
                             qb.shape[-1] // LANES if cfg["whole_width"] else cfg["diff_tiles"])
            o_parts = [o_a.reshape(n, -1), o_b.reshape(n, -1)]
            w_out = w_out_even
        else:
            od = li // 2
            rows = cfg["band_rows"]
            bias_tab = _band_bias_table(rel_bias[od], rows)
            if past is None:
                assert t >= BAND_PAST
                q, k, kk, v, vv = _pre_odd(h, row(norm_mix[li]), w_in_odd[od], tm, od, n_odd,
                                           odd_state, batch=b)
                valid_lo, valid_hi = BAND_PAST, BAND_PAST + t
                bd_past = total = None
            else:
                q, k, kbf, v, vbf = _pre_odd(h, row(norm_mix[li]), w_in_odd[od], tm, od, n_odd,
                                             odd_state)
                kk, vv = seq(kbf), seq(vbf)
                bd_past = (_cache_view(past[4], od), _cache_view(past[5], od))
                cache_rows = bd_past[0][2]
                assert cache_rows == BAND_PAST
                valid_lo, valid_hi = 0, cache_rows + t
                total = BAND_WINDOW
            odd_state = (k, v)
            o = _band_attn(seq(q), kk, vv, bias_tab, rows, cfg["band_sub"], valid_lo, valid_hi,
                           bd_past, total, tiles_of(q))
            o_parts = [o.reshape(n, -1)]
            w_out = w_out_odd
        gfin = row(norm_final) if li == depth - 1 else None
        h = _post(h, o_parts, p.reshape(depth, n, -1), li, w_out, li // 2, row(norm_ffn[li]), w_gate,
                  w_up, w_down, row(norm_ple[li]), w_ple_gate, w_ple, gfin, tm)
    y = h.reshape(b, t, d)
    rows = lambda a, shp: a.reshape((a.shape[0], b, -1) + shp)

    def cols(a, shp):
        a = a.reshape(a.shape[:2] + shp + a.shape[3:])
        return jnp.moveaxis(a, -1, 2)

    heads = cols if past is None else rows
    sb_k, sb_v, df_k, df_v = even_state
    bd_k, bd_v = odd_state
    state = (heads(sb_k, (H_SB, HEAD_DIM)), heads(sb_v, (H_SB, HEAD_DIM)),
             heads(df_k, (H_DIFF, 2, HEAD_DIM)), rows(df_v, (H_DIFF, 2 * HEAD_DIM)),
             heads(bd_k, (H_BAND, HEAD_DIM)), heads(bd_v, (H_BAND, HEAD_DIM)))
    return y, state


def kernel(x_prompt, x_sample, cache_sb_k, cache_sb_v, cache_diff_k, cache_diff_v, cache_band_k, cache_band_v, p_prompt, p_sample, norm_mix, w_in_even, w_out_even, diff_lambda, diff_norm, w_in_odd, w_out_odd, rel_bias, norm_ffn, w_gate, w_up, w_down, norm_ple, w_ple_gate, w_ple, norm_final):
    bf = _to_bf16
    weights = (norm_mix, bf(w_in_even), bf(w_out_even), diff_lambda, diff_norm, bf(w_in_odd),
               bf(w_out_odd), rel_bias, norm_ffn, bf(w_gate), bf(w_up), bf(w_down), norm_ple,
               bf(w_ple_gate), bf(w_ple), norm_final)
    t_p = x_prompt.shape[1]
    t_s = x_sample.shape[1]
    past_len = cache_sb_k.shape[2]
    diff_tk = 512
    cfg_p = dict(tm=512, sb_tq=256, sb_tiles=2, diff_tq=min(1024, t_p), diff_tk=diff_tk, diff_tiles=2,
                 band_rows=LANES,
                 band_sub=min(8, t_p // LANES), whole_width=False)
    keys_total = -(-(past_len + t_s) // diff_tk) * diff_tk
    cfg_s = dict(tm=min(512, x_sample.shape[0] * t_s), sb_tq=t_s, diff_tq=t_s, diff_tk=diff_tk, band_rows=t_s, band_sub=1,
                 keys_total=keys_total, whole_width=True)
    y_p, st_p = _trunk(x_prompt, p_prompt, None, weights, cfg_p)
    past = (cache_sb_k, cache_sb_v, cache_diff_k, cache_diff_v, cache_band_k, cache_band_v)
    y_s, st_s = _trunk(x_sample, p_sample, past, weights, cfg_s)
    return (y_p, y_s) + tuple(st_p) + tuple(st_s)
```

```python
import functools
import math

import jax
import jax.numpy as jnp
from jax import lax
from jax.experimental import pallas as pl
from jax.experimental.pallas import tpu as pltpu

CHUNK = 64
HEAD_DIM = 64
H_SB = 8
H_DIFF = 4
H_BAND = 16
BAND_PAST = 8 * CHUNK
REL_CLIP = 128
ROPE_THETA = 10000.0
EPS = 1e-6
NEG = -1e30
SCALE = HEAD_DIM ** -0.5
LOG2E = 1.4426950408889634
QSCALE = SCALE * LOG2E

LANES = 128
VMEM_LIMIT = 56 * 1024 * 1024

SB_KEY_BLOCK = 2 * LANES
SB_EXIT_LOG2 = -160.0
BAND_WINDOW = BAND_PAST + LANES
BAND_UNROLL = 8

F32 = jnp.float32
BF16 = jnp.bfloat16


def _rms(x, g):
    return x * lax.rsqrt(jnp.mean(x * x, axis=-1, keepdims=True) + EPS) * g


def _sigmoid(x):
    return 1.0 / (1.0 + jnp.exp(-x))


def _dot(a, b):
    return jnp.dot(a, b, preferred_element_type=F32)


def _dot_nt(a, b):
    return lax.dot_general(a, b, (((1,), (1,)), ((), ())), preferred_element_type=F32)


def _params(n_axes):
    return pltpu.CompilerParams(dimension_semantics=("arbitrary",) * n_axes,
                                vmem_limit_bytes=VMEM_LIMIT)


def _rope(x, cos, sin_signed, first_half):
    outs = []
    for j in range(x.shape[1] // LANES):
        xj = x[:, j * LANES:(j + 1) * LANES]
        partner = jnp.where(first_half, pltpu.roll(xj, LANES - HEAD_DIM // 2, 1),
                            pltpu.roll(xj, HEAD_DIM // 2, 1))
        outs.append(xj * cos + partner * sin_signed)
    return jnp.concatenate(outs, axis=1)


def _pre_even_kernel(h_ref, g_ref, w_ref, cos_ref, sin_ref, *rest, transposed):
    flip = (lambda a: a.T) if transposed else (lambda a: a)
    (qa_ref, ka_ref, kab_ref, va_ref, vab_ref, qb_ref, kb_ref, kbb_ref, vb_ref, vbb_ref) = rest[-10:]
    hn = _rms(h_ref[...], g_ref[...]).astype(BF16)
    width = qa_ref.shape[1]

    def proj(c):
        return _dot(hn, w_ref[:, c * width:(c + 1) * width])

    cos = cos_ref[...]
    sin_signed = sin_ref[...]
    lane = lax.broadcasted_iota(jnp.int32, cos.shape, 1)
    first_half = (lane % HEAD_DIM) < (HEAD_DIM // 2)

    qa_ref[...] = (proj(0) * QSCALE).astype(BF16)
    ka = proj(1)
    ka_ref[...] = flip(ka)
    kab_ref[...] = ka.astype(BF16)
    va = proj(2)
    va_ref[...] = flip(va)
    vab_ref[...] = va.astype(BF16)
    qb_ref[...] = (_rope(proj(3), cos, sin_signed, first_half) * QSCALE).astype(BF16)
    kb = _rope(proj(4), cos, sin_signed, first_half)
    kb_ref[...] = flip(kb)
    kbb_ref[...] = kb.astype(BF16)
    vb = proj(5)
    for hd in range(H_DIFF):
        vb_ref[pl.ds(hd, vb.shape[0], stride=H_DIFF), :] = vb[:, _lane_tile(hd)]
    vbb_ref[...] = vb.astype(BF16)


def _pre_odd_kernel(h_ref, g_ref, w_ref, *rest, padded):
    q_ref, k_ref, kb_ref, v_ref, vb_ref = rest[-5:]
    width = q_ref.shape[1]

    def project(keep_f32):
        hn = _rms(h_ref[...], g_ref[...]).astype(BF16)
        q_ref[...] = (_dot(hn, w_ref[:, :width]) * QSCALE).astype(BF16)
        k = _dot(hn, w_ref[:, width:2 * width])
        kb_ref[...] = k.astype(BF16)
        v = _dot(hn, w_ref[:, 2 * width:])
        vb_ref[...] = v.astype(BF16)
        keep_f32(k, v)

    def store_f32(k, v):
        k_ref[...] = k.T if padded else k
        v_ref[...] = v.T if padded else v

    if not padded:
        project(store_f32)
        return

    j = pl.program_id(1)

    @pl.when(j == 0)
    def _():
        kb_ref[...] = jnp.zeros(kb_ref.shape, BF16)
        vb_ref[...] = jnp.zeros(vb_ref.shape, BF16)

    @pl.when(j > 0)
    def _():
        project(lambda k, v: pl.when(j == pl.num_programs(1) - 1)(lambda: store_f32(k, v)))


def _stack_alias(prev, n_fixed_inputs, out_positions):
    if prev is None:
        return [], [], {}
    specs = [pl.BlockSpec(memory_space=pl.ANY)] * len(prev)
    aliases = {n_fixed_inputs + i: o for i, o in enumerate(out_positions)}
    return list(prev), specs, aliases


def _pre_even(h, g, w, cos, sin_signed, tm, slot, n_slots, prev, batch=None):
    n, d = h.shape
    width = w.shape[1] // 6
    n_pos_blocks = cos.shape[0] // tm
    tok = lambda wd: pl.BlockSpec((tm, wd), lambda i: (i, 0))
    const = lambda a: pl.BlockSpec(a.shape, lambda i: (0, 0))
    pos = pl.BlockSpec((tm, LANES), lambda i: (i % n_pos_blocks, 0))
    stk = pl.BlockSpec((None, tm, width), lambda i: (slot, i, 0))
    f32s = jax.ShapeDtypeStruct((n_slots, n, width), F32)
    bf16o = jax.ShapeDtypeStruct((n, width), BF16)
    t, s = tok(width), stk
    st, f32t = s, f32s
    assert width == H_DIFF * LANES
    sv = pl.BlockSpec((None, tm * H_DIFF, LANES), lambda i: (slot, i, 0))
    f32v = jax.ShapeDtypeStruct((n_slots, n * H_DIFF, LANES), F32)
    if batch is not None:
        tiles = n // batch // tm
        st = pl.BlockSpec((None, None, width, tm), lambda i: (slot, i // tiles, 0, i % tiles))
        f32t = jax.ShapeDtypeStruct((n_slots, batch, width, n // batch), F32)
    prev_args, prev_specs, aliases = _stack_alias(prev, 5, (1, 3, 6, 8))
    return pl.pallas_call(
        functools.partial(_pre_even_kernel, transposed=batch is not None),
        grid=(n // tm,),
        in_specs=[tok(d), const(g), const(w), pos, pos] + prev_specs,
        out_specs=[t, st, t, st, t, t, st, t, sv, t],
        out_shape=[bf16o, f32t, bf16o, f32t, bf16o, bf16o, f32t, bf16o, f32v, bf16o],
        input_output_aliases=aliases,
        compiler_params=_params(1),
    )(h, g, w, cos, sin_signed, *prev_args)


def _pre_odd(h, g, w, tm, slot, n_slots, prev, batch=None):
    n, d = h.shape
    width = w.shape[1] // 3
    padded = batch is not None
    bf16o = jax.ShapeDtypeStruct((n, width), BF16)
    if padded:
        assert tm == BAND_PAST
        tiles = n // batch // tm
        grid = (batch, tiles + 1)
        tile = lambda bi, j: bi * tiles + jnp.maximum(j - 1, 0)
        tok = lambda wd: pl.BlockSpec((tm, wd), lambda bi, j: (tile(bi, j), 0))
        const = lambda a: pl.BlockSpec(a.shape, lambda bi, j: (0, 0))
        stk = pl.BlockSpec((None, None, width, tm), lambda bi, j: (slot, bi, 0, 0))
        pad = pl.BlockSpec((None, tm, width), lambda bi, j: (bi, j, 0))
        f32s = jax.ShapeDtypeStruct((n_slots, batch, width, tm), F32)
        bf16p = jax.ShapeDtypeStruct((batch, (tiles + 1) * tm, width), BF16)
        out_specs, out_shape = [tok(width), stk, pad, stk, pad], [bf16o, f32s, bf16p, f32s, bf16p]
    else:
        grid = (n // tm,)
        tok = lambda wd: pl.BlockSpec((tm, wd), lambda i: (i, 0))
        const = lambda a: pl.BlockSpec(a.shape, lambda i: (0, 0))
        stk = pl.BlockSpec((None, tm, width), lambda i: (slot, i, 0))
        f32s = jax.ShapeDtypeStruct((n_slots, n, width), F32)
        out_specs, out_shape = [tok(width), stk, tok(width), stk, tok(width)], [bf16o, f32s, bf16o, f32s, bf16o]
    prev_args, prev_specs, aliases = _stack_alias(prev, 3, (1, 3))
    return pl.pallas_call(
        functools.partial(_pre_odd_kernel, padded=padded),
        grid=grid,
        in_specs=[tok(d), const(g), const(w)] + prev_specs,
        out_specs=out_specs,
        out_shape=out_shape,
        input_output_aliases=aliases,
        compiler_params=_params(len(grid)),
    )(h, g, w, *prev_args)


def _cast_kernel(x_ref, o_ref):
    o_ref[...] = x_ref[...].astype(o_ref.dtype)


def _to_bf16(w):
    layers, rows, cols = w.shape
    tr = rows // 4
    assert rows % 4 == 0 and tr % 16 == 0
    spec = pl.BlockSpec((None, tr, cols), lambda i, j: (i, j, 0))
    return pl.pallas_call(
        _cast_kernel,
        grid=(layers, rows // tr),
        in_specs=[spec],
        out_specs=spec,
        out_shape=jax.ShapeDtypeStruct(w.shape, BF16),
        compiler_params=_params(2),
    )(w)


def _post_kernel(*refs, n_o, ffn_chunk, final):
    h_ref = refs[0]
    o_refs = refs[1:1 + n_o]
    (p_ref, wout_ref, gffn_ref, wg_ref, wu_ref, wd_ref, gple_ref, wpg_ref, wp_ref) = refs[1 + n_o:10 + n_o]
    gfin_ref = refs[10 + n_o] if final else None
    out_ref = refs[-1]

    h = h_ref[...]
    off = 0
    for o_ref in o_refs:
        wd = o_ref.shape[1]
        h = h + _dot(o_ref[...], wout_ref[off:off + wd, :])
        off += wd

    hn = _rms(h, gffn_ref[...]).astype(BF16)
    hidden = wg_ref.shape[1]
    for c in range(hidden // ffn_chunk):
        cols = slice(c * ffn_chunk, (c + 1) * ffn_chunk)
        gt = _dot(hn, wg_ref[:, cols])
        up = _dot(hn, wu_ref[:, cols])
        act = (gt * _sigmoid(gt) * up).astype(BF16)
        h = h + _dot(act, wd_ref[cols, :])

    gate = _sigmoid(_dot(_rms(h, gple_ref[...]).astype(BF16), wpg_ref[...]))
    h = h + _dot(p_ref[...].astype(BF16), wp_ref[...]) * gate
    if final:
        h = _rms(h, gfin_ref[...])
    out_ref[...] = h


def _post(h, o_parts, p, layer, wout, wout_layer, gffn, wg, wu, wd, gple, wpg, wp, gfin, tm):
    n, d = h.shape
    final = gfin is not None
    hidden = wg.shape[2]
    ffn_chunk = hidden
    tok = lambda a: pl.BlockSpec((tm, a.shape[1]), lambda i: (i, 0))
    row = lambda a: pl.BlockSpec(a.shape, lambda i: (0, 0), pipeline_mode=pl.Buffered(1))
    mat = lambda a, li: pl.BlockSpec((None,) + a.shape[1:], lambda i: (li, 0, 0),
                                     pipeline_mode=pl.Buffered(1))
    consts = [wout, gffn, wg, wu, wd, gple, wpg, wp] + ([gfin] if final else [])
    const_specs = ([mat(wout, wout_layer), row(gffn)] + [mat(a, layer) for a in (wg, wu, wd)]
                   + [row(gple), mat(wpg, layer), mat(wp, layer)] + ([row(gfin)] if final else []))
    p_spec = pl.BlockSpec((None, tm, p.shape[2]), lambda i: (layer, i, 0))
    return pl.pallas_call(
        functools.partial(_post_kernel, n_o=len(o_parts), ffn_chunk=ffn_chunk, final=final),
        grid=(n // tm,),
        in_specs=[tok(h)] + [tok(o) for o in o_parts] + [p_spec] + const_specs,
        out_specs=tok(h),
        out_shape=jax.ShapeDtypeStruct((n, d), F32),
        compiler_params=_params(1),
    )(h, *o_parts, p, *consts)


def _lane_tile(i):
    return slice(i * LANES, (i + 1) * LANES)


def _sb_kernel(q_ref, k_ref, v_ref, tt_ref, o_ref, acc_ref, run_ref, *, tq, q_off):
    qi = pl.program_id(2)
    n_pairs = q_ref.shape[1] // LANES
    lane = lax.broadcasted_iota(jnp.int32, (tq, LANES), 1)
    row_pos = q_off + qi * tq + lax.broadcasted_iota(jnp.int32, (tq, LANES), 0)
    first = lane < HEAD_DIM
    blk0 = (q_off + qi * tq) // SB_KEY_BLOCK
    tt = tt_ref[...]
    heads = range(2 * n_pairs)
    q_heads = []
    for pair in range(n_pairs):
        q2 = q_ref[:, _lane_tile(pair)]
        zero = jnp.zeros_like(q2)
        q_heads += [jnp.where(first, q2, zero), jnp.where(first, zero, q2)]
    acc_ref[...] = jnp.zeros(acc_ref.shape, F32)
    run_ref[...] = jnp.zeros(run_ref.shape, F32)

    def span(first_block, n_blocks, own_block):
        n_halves = 2 * n_blocks
        width = n_halves * LANES
        start = first_block * SB_KEY_BLOCK
        if not isinstance(start, int):
            start = pl.multiple_of(start, SB_KEY_BLOCK)
        kblk = [k_ref[pl.ds(start, width), _lane_tile(pair)] for pair in range(n_pairs)]
        vblk = [v_ref[pl.ds(start, width), _lane_tile(pair)] for pair in range(n_pairs)]
        s = [_dot_nt(q_heads[h], kblk[h // 2]) for h in heads]
        soft = [jnp.log(1.0 + jnp.exp2(-jnp.abs(s[h]))) * LOG2E for h in heads]
        log_beta = [jnp.minimum(s[h], 0.0) - soft[h] for h in heads]
        log_keep = [log_beta[h] - s[h] for h in heads]
        masks = {}
        if own_block:
            for i in (n_halves - 2, n_halves - 1):
                masks[i] = (first_block * SB_KEY_BLOCK + i * LANES + lane) < row_pos
        cs = []
        for h in heads:
            per_half = []
            for i in range(n_halves):
                keep = log_keep[h][:, _lane_tile(i)]
                if i in masks:
                    keep = jnp.where(masks[i], keep, 0.0)
                hi = keep.astype(BF16)
                lo = (keep - hi.astype(F32)).astype(BF16)
                per_half.append(_dot(jnp.concatenate([hi, lo], axis=1), tt))
            cs.append(per_half)
        for h in heads:
            run = run_ref[h]
            w = [None] * n_halves
            for i in reversed(range(n_halves)):
                w_i = jnp.exp2(log_beta[h][:, _lane_tile(i)] + cs[h][i][:, :LANES] + run)
                if i in masks:
                    w_i = jnp.where(masks[i], w_i, 0.0)
                w[i] = w_i.astype(BF16)
                run = run + cs[h][i][:, LANES:]
            acc_ref[h] += _dot(jnp.concatenate(w, axis=1), vblk[h // 2])
            run_ref[h] = run

    @pl.when(blk0 == 0)
    def _():
        span(0, 1, True)

    @pl.when(blk0 > 0)
    def _():
        span(blk0 - 1, 2, True)

    def cond(carry):
        i, live = carry
        return jnp.logical_and(i < blk0 - 1, live > SB_EXIT_LOG2)

    def live_mass():
        return jnp.max(functools.reduce(jnp.maximum, [run_ref[h] for h in heads]))

    def body(carry):
        i, _ = carry
        span(blk0 - 2 - i, 1, False)
        return i + 1, live_mass()

    lax.while_loop(cond, body, (jnp.int32(0), live_mass()))
    for pair in range(n_pairs):
        o_ref[:, _lane_tile(pair)] = jnp.where(first, acc_ref[2 * pair],
                                               acc_ref[2 * pair + 1]).astype(o_ref.dtype)


def _suffix_matrix():
    r = lax.broadcasted_iota(jnp.int32, (2 * LANES, 2 * LANES), 0) % LANES
    c = lax.broadcasted_iota(jnp.int32, (2 * LANES, 2 * LANES), 1)
    return jnp.where((c >= LANES) | (r > c), 1.0, 0.0).astype(BF16)


def _keys_from_past(kernel_fn, n_before, past_rows):
    def wrapped(*refs):
        past_new = refs[n_before:n_before + 4]
        k_buf, v_buf = refs[-2:]
        p = past_rows
        for past, new, buf in ((past_new[0], past_new[2], k_buf), (past_new[1], past_new[3], v_buf)):
            t = new.shape[0]
            if past.shape[1] == p:
                buf[0:p, :] = past[...].T.astype(buf.dtype)
            else:
                heads = past.shape[0] // p
                for hd in range(heads):
                    buf[0:p, _lane_tile(hd)] = past[pl.ds(hd, p, stride=heads), :].astype(buf.dtype)
            buf[p:p + t, :] = new[...]
            buf[p + t:, :] = jnp.zeros((buf.shape[0] - p - t, buf.shape[1]), buf.dtype)
        kernel_fn(*refs[:n_before], k_buf, v_buf, *refs[n_before + 4:-2])
    return wrapped


def _cache_view(cache, layer):
    n_layers, b, p = cache.shape[:3]
    if cache.shape[-1] == LANES:
        return cache.reshape(n_layers, b, -1, LANES), layer, p
    perm = (0, 1) + tuple(range(3, cache.ndim)) + (2,)
    return jnp.transpose(cache, perm).reshape(n_layers, b, -1, p), layer, p


def _key_operands(kernel_fn, n_before, k, v, past, total, tiles=1):
    index = lambda bi, hp, qi: (bi, 0, hp)
    spec = lambda a: pl.BlockSpec((None, a.shape[1], tiles * LANES), index)
    if past is None:
        return kernel_fn, k.shape[1], [k, v], [spec(k), spec(v)], []
    assert tiles * LANES == k.shape[2]
    past_specs = [pl.BlockSpec((None, None) + a.shape[2:], lambda bi, hp, qi, layer=layer: (layer, bi, 0, 0))
                  for a, layer, _ in past]
    buf = pltpu.VMEM((total, tiles * LANES), BF16)
    operands = [past[0][0], past[1][0], k, v]
    return (_keys_from_past(kernel_fn, n_before, past[0][2]), total, operands,
            past_specs + [spec(k), spec(v)], [buf, buf])


def _sb_attn(q, k, v, tq, q_off, past=None, total=None, tiles=1):
    b, t, w = q.shape
    body, keys, key_args, key_specs, key_scratch = _key_operands(
        functools.partial(_sb_kernel, tq=tq, q_off=q_off), 1, k, v, past, total, tiles)
    assert q_off % SB_KEY_BLOCK == 0 and (tq == SB_KEY_BLOCK or (t == tq and tq < SB_KEY_BLOCK))
    assert -(-(q_off + t) // SB_KEY_BLOCK) * SB_KEY_BLOCK <= keys and w % (tiles * LANES) == 0
    tt = _suffix_matrix()
    qspec = pl.BlockSpec((None, tq, tiles * LANES), lambda bi, hp, qi: (bi, qi, hp))
    state = pltpu.VMEM((2 * tiles, tq, LANES), F32)
    return pl.pallas_call(
        body,
        grid=(b, w // (tiles * LANES), t // tq),
        in_specs=[qspec] + key_specs + [pl.BlockSpec(tt.shape, lambda bi, hp, qi: (0, 0))],
        out_specs=qspec,
        out_shape=jax.ShapeDtypeStruct((b, t, w), BF16),
        scratch_shapes=[state, state] + key_scratch,
        compiler_params=_params(3),
    )(q, *key_args, tt)


def _diff_kernel(lam_ref, gain_ref, q_ref, k_ref, v_ref, o_ref, m_ref, l_ref, acc_ref,
                 *, tq, tk, q_off, kv_len, n_diag, lam_init):
    qi = pl.program_id(2)
    n_heads = q_ref.shape[1] // LANES
    lane = lax.broadcasted_iota(jnp.int32, (tq, LANES), 1)
    first = lane < HEAD_DIM
    q_maps = []
    for hd in range(n_heads):
        q2 = q_ref[:, _lane_tile(hd)]
        zero = jnp.zeros_like(q2)
        q_maps += [jnp.where(first, q2, zero), jnp.where(first, zero, q2)]
    diag_rows = min(tq, tk)
    row_chunk = lax.broadcasted_iota(jnp.int32, (diag_rows, tk), 0) // CHUNK
    col = lax.broadcasted_iota(jnp.int32, (diag_rows, tk), 1)
    diag_mask = (col // CHUNK) <= row_chunk
    blk0 = (q_off + qi * tq) // tk
    ones = jnp.ones((tk, LANES), BF16)

    m_ref[...] = jnp.full(m_ref.shape, NEG, F32)
    l_ref[...] = jnp.zeros(l_ref.shape, F32)
    acc_ref[...] = jnp.zeros(acc_ref.shape, F32)

    def block(kb, masked, rows=slice(0, tq)):
        start = pl.multiple_of(kb * tk, tk)
        kblk = [k_ref[pl.ds(start, tk), _lane_tile(hd)] for hd in range(n_heads)]
        vext = [jnp.concatenate([v_ref[pl.ds(start, tk), _lane_tile(hd)], ones], axis=1)
                for hd in range(n_heads)]
        if masked:
            mask = diag_mask
            if kv_len % tk:
                mask = mask & (kb * tk + col < kv_len)
        maps = range(2 * n_heads)
        slabs = range(tk // LANES)
        s = [_dot_nt(q_maps[mp][rows], kblk[mp // 2]) for mp in maps]
        if masked:
            s = [jnp.where(mask, s[mp], NEG) for mp in maps]
        m_old = [m_ref[mp, rows, :] for mp in maps]
        m_new = []
        for mp in maps:
            mx = s[mp][:, :LANES]
            for j in slabs[1:]:
                mx = jnp.maximum(mx, s[mp][:, j * LANES:(j + 1) * LANES])
            m_new.append(jnp.maximum(m_old[mp], jnp.max(mx, axis=-1, keepdims=True)))
        alpha = [jnp.exp2(m_old[mp] - m_new[mp]) for mp in maps]
        p = [jnp.concatenate([jnp.exp2(s[mp][:, j * LANES:(j + 1) * LANES] - m_new[mp]).astype(BF16)
                              for j in slabs], axis=1) for mp in maps]
        pv = [_dot(p[mp], vext[mp // 2]) for mp in maps]
        for mp in maps:
            acc_ref[mp, rows, :] = acc_ref[mp, rows, :] * alpha[mp] + pv[mp][:, :LANES]
            l_ref[mp, rows, :] = l_ref[mp, rows, :] * alpha[mp] + pv[mp][:, LANES:]
            m_ref[mp, rows, :] = m_new[mp]

    for r in range(n_diag):
        rows = slice(r * diag_rows, (r + 1) * diag_rows)
        block(blk0 + r, True, rows)
        for j in range(r):
            block(blk0 + j, False, rows)

    def body(i, carry):
        block(2 * i, False)
        block(2 * i + 1, False)
        return carry

    lax.fori_loop(0, blk0 // 2, body, 0)

    @pl.when(blk0 % 2 == 1)
    def _():
        block(blk0 - 1, False)

    lp = lam_ref[...]
    lam = (jnp.exp(jnp.sum(lp[0:1] * lp[1:2], axis=-1, keepdims=True))
           - jnp.exp(jnp.sum(lp[2:3] * lp[3:4], axis=-1, keepdims=True)) + lam_init)
    for hd in range(n_heads):
        o = acc_ref[2 * hd] / l_ref[2 * hd] - lam * (acc_ref[2 * hd + 1] / l_ref[2 * hd + 1])
        o = o * lax.rsqrt(jnp.mean(o * o, axis=-1, keepdims=True) + EPS)
        o_ref[:, _lane_tile(hd)] = (o * gain_ref[hd] * (1.0 - lam_init)).astype(o_ref.dtype)


def _diff_attn(q, k, v, lam_params, gain, tq, tk, q_off, kv_len, lam_init, past=None, total=None,
               tiles=1):
    b, t, w = q.shape
    n_diag = max(1, tq // tk)
    body, keys, key_args, key_specs, key_scratch = _key_operands(
        functools.partial(_diff_kernel, tq=tq, tk=tk, q_off=q_off, kv_len=kv_len, n_diag=n_diag,
                          lam_init=lam_init), 3, k, v, past, total, tiles)
    assert q_off % tk == 0 and (tq % tk == 0 or t == tq) and keys % tk == 0 and CHUNK % 8 == 0
    assert (q_off + t + tk - 1) // tk * tk <= keys and w % (tiles * LANES) == 0
    gain3 = gain.reshape(gain.shape[0], 1, gain.shape[1])
    qspec = pl.BlockSpec((None, tq, tiles * LANES), lambda bi, hd, qi: (bi, qi, hd))
    return pl.pallas_call(
        body,
        grid=(b, w // (tiles * LANES), t // tq),
        in_specs=[pl.BlockSpec(lam_params.shape, lambda bi, hd, qi: (0, 0)),
                  pl.BlockSpec((tiles, 1, LANES), lambda bi, hd, qi: (hd, 0, 0)),
                  qspec] + key_specs,
        out_specs=qspec,
        out_shape=jax.ShapeDtypeStruct((b, t, w), BF16),
        scratch_shapes=[pltpu.VMEM((2 * tiles, tq, LANES), F32)] * 3 + key_scratch,
        compiler_params=_params(3),
    )(lam_params, gain3, q, *key_args)


def _band_kernel(q_ref, k_ref, v_ref, bias_ref, o_ref, *, rows, n_sub, valid_lo, valid_hi):
    qi = pl.program_id(2)
    n_pairs = q_ref.shape[1] // LANES
    lane = lax.broadcasted_iota(jnp.int32, (rows, LANES), 1)
    first = lane < HEAD_DIM
    ones = jnp.ones((BAND_WINDOW, LANES), BF16)
    kcol = lax.broadcasted_iota(jnp.int32, (1, BAND_WINDOW), 1)

    def sub(j, carry):
        r0 = pl.multiple_of(j * rows, rows)
        ws = pl.multiple_of((qi * n_sub + j) * rows, rows)
        krow = ws + kcol
        penalty = jnp.where((krow >= valid_lo) & (krow < valid_hi), 0.0, NEG)
        for pair in range(n_pairs):
            q2 = q_ref[pl.ds(r0, rows), _lane_tile(pair)]
            zero = jnp.zeros_like(q2)
            kw = k_ref[pl.ds(ws, BAND_WINDOW), _lane_tile(pair)]
            vext = jnp.concatenate([v_ref[pl.ds(ws, BAND_WINDOW), _lane_tile(pair)], ones], axis=1)
            outs = []
            for head in range(2):
                qh = jnp.where(first if head == 0 else jnp.logical_not(first), q2, zero)
                s = _dot_nt(qh, kw) + bias_ref[2 * pair + head] + penalty
                p = jnp.exp2(s - jnp.max(s, axis=-1, keepdims=True))
                pv = _dot(p.astype(BF16), vext)
                outs.append(pv[:, :LANES] / pv[:, LANES:])
            o_ref[pl.ds(r0, rows), _lane_tile(pair)] = jnp.where(first, outs[0], outs[1]).astype(o_ref.dtype)
        return carry

    lax.fori_loop(0, n_sub, sub, 0, unroll=BAND_UNROLL if n_sub % BAND_UNROLL == 0 else 1)


def _band_bias_table(rel_bias, rows):
    heads = rel_bias.shape[0]
    span = BAND_WINDOW + rows - 1
    n_far = BAND_PAST - REL_CLIP + rows
    rb = rel_bias.astype(F32) * LOG2E
    near = rb[:, 2 * REL_CLIP - 1:0:-1]
    assert n_far + near.shape[1] == span
    e = jnp.concatenate([jnp.broadcast_to(rb[:, -1:], (heads, n_far)), near,
                         jnp.zeros((heads, 1), F32)], axis=1)
    skew = jnp.broadcast_to(e[:, None, :], (heads, rows, span + 1)).reshape(heads, rows * (span + 1))
    skew = skew[:, :rows * span].reshape(heads, rows, span)
    bias = skew[:, :, rows - 1:]
    i = jnp.arange(rows, dtype=jnp.int32)[:, None]
    rel_key = jnp.arange(BAND_WINDOW, dtype=jnp.int32)[None, :] - BAND_PAST
    kc = jnp.floor_divide(rel_key, CHUNK)
    qc = i // CHUNK
    in_band = (kc <= qc) & (kc >= qc - BAND_PAST // CHUNK)
    return jnp.where(in_band[None], bias, NEG)


def _band_attn(q, k, v, bias_tab, rows, n_sub, valid_lo, valid_hi, past=None, total=None, tiles=1):
    b, t, w = q.shape
    tq = rows * n_sub
    body, keys, key_args, key_specs, key_scratch = _key_operands(
        functools.partial(_band_kernel, rows=rows, n_sub=n_sub, valid_lo=valid_lo, valid_hi=valid_hi),
        1, k, v, past, total, tiles)
    assert t % tq == 0 and (t - rows) + BAND_WINDOW <= keys and w % (tiles * LANES) == 0
    qspec = pl.BlockSpec((None, tq, tiles * LANES), lambda bi, hp, qi: (bi, qi, hp))
    return pl.pallas_call(
        body,
        grid=(b, w // (tiles * LANES), t // tq),
        in_specs=[qspec] + key_specs +
                 [pl.BlockSpec((2 * tiles, rows, BAND_WINDOW), lambda bi, hp, qi: (hp, 0, 0))],
        out_specs=qspec,
        out_shape=jax.ShapeDtypeStruct((b, t, w), BF16),
        scratch_shapes=key_scratch,
        compiler_params=_params(3),
    )(q, *key_args, bias_tab)


def _rope_tables(pos, reps):
    half = HEAD_DIM // 2
    inv = ROPE_THETA ** (-jnp.arange(half, dtype=F32) / half)
    ang = pos.astype(F32)[:, None] * inv[None, :]
    cos, sin = jnp.cos(ang), jnp.sin(ang)
    cos = jnp.tile(cos, (reps, LANES // half))
    sin_signed = jnp.tile(jnp.concatenate([-sin, sin], axis=1), (reps, LANES // HEAD_DIM))
    return cos, sin_signed


def _trunk(x, p, past, weights, cfg):
    (norm_mix, w_in_even, w_out_even, diff_lambda, diff_norm, w_in_odd, w_out_odd, rel_bias,
     norm_ffn, w_gate, w_up, w_down, norm_ple, w_ple_gate, w_ple, norm_final) = weights
    b, t, d = x.shape
    n = b * t
    depth = norm_mix.shape[0]
    tm = cfg["tm"]
    q_off = 0 if past is None else past[0].shape[2]
    pos = q_off + jnp.arange(t, dtype=jnp.int32)
    if past is None:
        cos, sin_signed = _rope_tables(pos, 1)
    else:
        cos, sin_signed = _rope_tables(pos, tm // t)
    row = lambda a: a.reshape(1, -1)
    seq = lambda a: a.reshape(b, t, a.shape[-1])

    h = x.reshape(n, d)
    n_even, n_odd = w_in_even.shape[0], w_in_odd.shape[0]
    even_state = None
    odd_state = None
    for li in range(depth):
        if li % 2 == 0:
            e = li // 2
            qa, ka, kab, va, vab, qb, kb, kbb, vb, vbb = _pre_even(
                h, row(norm_mix[li]), w_in_even[e], cos, sin_signed, tm, e, n_even, even_state,
                batch=b if past is None else None)
            even_state = (ka, va, kb, vb)
            lam_init = 0.8 - 0.6 * math.exp(-0.3 * li)
            keys = [seq(a) for a in (kab, vab, kbb, vbb)]
            if past is None:
                sb_past = df_past = total = None
                kv_len = t
            else:
                total = cfg["keys_total"]
                sb_past = (_cache_view(past[0], e), _cache_view(past[1], e))
                df_past = (_cache_view(past[2], e), _cache_view(past[3], e))
                kv_len = q_off + t
            o_a = _sb_attn(seq(qa), keys[0], keys[1], cfg["sb_tq"], q_off, sb_past, total,
                           qa.shape[-1] // LANES if cfg["whole_width"] else cfg["sb_tiles"])
            o_b = _diff_attn(seq(qb), keys[2], keys[3], diff_lambda[e], diff_norm[e],
                             cfg["diff_tq"], cfg["diff_tk"], q_off, kv_len, lam_init, df_past, total,
                             qb.shape[-1] // LANES if cfg["whole_width"] else cfg["diff_tiles"])
            o_parts = [o_a.reshape(n, -1), o_b.reshape(n, -1)]
            w_out = w_out_even
        else:
            od = li // 2
            rows = cfg["band_rows"]
            bias_tab = _band_bias_table(rel_bias[od], rows)
            if past is None:
                assert t >= BAND_PAST
                q, k, kk, v, vv = _pre_odd(h, row(norm_mix[li]), w_in_odd[od], tm, od, n_odd,
                                           odd_state, batch=b)
                valid_lo, valid_hi = BAND_PAST, BAND_PAST + t
                bd_past = total = None
            else:
                q, k, kbf, v, vbf = _pre_odd(h, row(norm_mix[li]), w_in_odd[od], tm, od, n_odd,
                                             odd_state)
                kk, vv = seq(kbf), seq(vbf)
                bd_past = (_cache_view(past[4], od), _cache_view(past[5], od))
                cache_rows = bd_past[0][2]
                assert cache_rows == BAND_PAST
                valid_lo, valid_hi = 0, cache_rows + t
                total = BAND_WINDOW
            odd_state = (k, v)
            o = _band_attn(seq(q), kk, vv, bias_tab, rows, cfg["band_sub"], valid_lo, valid_hi,
                           bd_past, total,
                           q.shape[-1] // LANES if cfg["whole_width"] else cfg["band_tiles"])
            o_parts = [o.reshape(n, -1)]
            w_out = w_out_odd
        gfin = row(norm_final) if li == depth - 1 else None
        h = _post(h, o_parts, p.reshape(depth, n, -1), li, w_out, li // 2, row(norm_ffn[li]), w_gate,
                  w_up, w_down, row(norm_ple[li]), w_ple_gate, w_ple, gfin, tm)
    y = h.reshape(b, t, d)
    rows = lambda a, shp: a.reshape((a.shape[0], b, -1) + shp)

    def cols(a, shp):
        a = a.reshape(a.shape[:2] + shp + a.shape[3:])
        return jnp.moveaxis(a, -1, 2)

    heads = cols if past is None else rows
    sb_k, sb_v, df_k, df_v = even_state
    bd_k, bd_v = odd_state
    state = (heads(sb_k, (H_SB, HEAD_DIM)), heads(sb_v, (H_SB, HEAD_DIM)),
             heads(df_k, (H_DIFF, 2, HEAD_DIM)), rows(df_v, (H_DIFF, 2 * HEAD_DIM)),
             heads(bd_k, (H_BAND, HEAD_DIM)), heads(bd_v, (H_BAND, HEAD_DIM)))
    return y, state


def kernel(x_prompt, x_sample, cache_sb_k, cache_sb_v, cache_diff_k, cache_diff_v, cache_band_k, cache_band_v, p_prompt, p_sample, norm_mix, w_in_even, w_out_even, diff_lambda, diff_norm, w_in_odd, w_out_odd, rel_bias, norm_ffn, w_gate, w_up, w_down, norm_ple, w_ple_gate, w_ple, norm_final):
    bf = _to_bf16
    weights = (norm_mix, bf(w_in_even), bf(w_out_even), diff_lambda, diff_norm, bf(w_in_odd),
               bf(w_out_odd), rel_bias, norm_ffn, bf(w_gate), bf(w_up), bf(w_down), norm_ple,
               bf(w_ple_gate), bf(w_ple), norm_final)
    t_p = x_prompt.shape[1]
    t_s = x_sample.shape[1]
    past_len = cache_sb_k.shape[2]
    diff_tk = 512
    cfg_p = dict(tm=512, sb_tq=256, sb_tiles=2, diff_tq=min(1024, t_p), diff_tk=diff_tk, diff_tiles=2,
                 band_rows=LANES, band_tiles=2,
                 band_sub=min(8, t_p // LANES), whole_width=False)
    keys_total = -(-(past_len + t_s) // diff_tk) * diff_tk
    cfg_s = dict(tm=min(512, x_sample.shape[0] * t_s), sb_tq=t_s, diff_tq=t_s, diff_tk=diff_tk, band_rows=t_s, band_sub=1,
                 keys_total=keys_total, whole_width=True)
    y_p, st_p = _trunk(x_prompt, p_prompt, None, weights, cfg_p)
    past = (cache_sb_k, cache_sb_v, cache_diff_k, cache_diff_v, cache_band_k, cache_band_v)
    y_s, st_s = _trunk(x_sample, p_sample, past, weights, cfg_s)
    return (y_p, y_s) + tuple(st_p) + tuple(st_s)
```
